```python
import math
import jax, jax.numpy as jnp
from jax import lax
import numpy as np

D_MODEL = 1024
BATCH = 8
SEQ = 2048
DEPTH = 1
DEC_BATCH = 128
DEC_SEQ = 1
PAST_LEN = 16384
PAGE_SIZE = 128

MIX_W = D_MODEL
GROUP_W = MIX_W // 2
RET_HEADS = 4
RET_DK = GROUP_W // RET_HEADS
RET_DV = GROUP_W // RET_HEADS
HG_HEADS = 4
HG_DK = GROUP_W // HG_HEADS
HG_DV = GROUP_W // HG_HEADS
N_IN_COLS = 8 * GROUP_W
CHUNK = 64
ROPE_BASE = 10000.0
NORM_EPS = 1e-6

kernel_name = "retention_hgrn2_parallel_heads_step"


def rmsnorm(x, g):
    xf = x.astype(jnp.float32)
    y = xf * lax.rsqrt(jnp.mean(xf * xf, axis=-1, keepdims=True) + NORM_EPS)
    return (y * g.astype(jnp.float32)).astype(x.dtype)


def head_rmsnorm(o, g):
    B, L, H, dv = o.shape
    y = o * lax.rsqrt(jnp.mean(o * o, axis=-1, keepdims=True) + NORM_EPS)
    return y.reshape(B, L, H * dv) * g.astype(jnp.float32)


def rotary(x, pos):
    half = x.shape[-1] // 2
    freqs = 1.0 / (ROPE_BASE ** (jnp.arange(half, dtype=jnp.float32) / half))
    ang = pos[:, None] * freqs[None, :]
    c = jnp.cos(ang)[None, :, None, :]
    s = jnp.sin(ang)[None, :, None, :]
    x1, x2 = x[..., :half], x[..., half:]
    return jnp.concatenate([x1 * c - x2 * s, x1 * s + x2 * c], axis=-1)


def chunked_decay_recurrence(q, k, v, logf, s0):
    B, L, H, DK = q.shape
    DV = v.shape[-1]
    C = math.gcd(L, CHUNK)
    n = L // C

    def to_chunks(a):
        return a.astype(jnp.float32).reshape(B, n, C, H, a.shape[-1]).transpose(1, 0, 3, 2, 4)

    qs, ks, vs, gs = to_chunks(q), to_chunks(k), to_chunks(v), to_chunks(logf)
    mask = jnp.tril(jnp.ones((C, C), dtype=bool))[:, :, None]
    scalar_decay = logf.shape[-1] == 1

    def step(S, inp):
        qc, kc, vc, gc = inp
        b = jnp.cumsum(gc, axis=2)
        o_inter = jnp.einsum('bhtd,bhdv->bhtv', qc * jnp.exp(b), S)
        diff = b[:, :, :, None, :] - b[:, :, None, :, :]
        decay = jnp.exp(jnp.where(mask, diff, -jnp.inf))
        if scalar_decay:
            A = jnp.einsum('bhtd,bhsd->bhts', qc, kc) * decay[..., 0]
        else:
            A = jnp.einsum('bhtd,bhsd,bhtsd->bhts', qc, kc, decay)
        o = o_inter + jnp.einsum('bhts,bhsv->bhtv', A, vc)
        bC = b[:, :, -1:, :]
        S_new = jnp.exp(bC[:, :, 0, :])[..., None] * S + jnp.einsum(
            'bhsd,bhsv->bhdv', kc * jnp.exp(bC - b), vc)
        return S_new, o

    S, o = lax.scan(step, s0.astype(jnp.float32), (qs, ks, vs, gs))
    o = o.transpose(1, 0, 3, 2, 4).reshape(B, L, H, DV)
    return o, S


def mixer_layer(x, pos, s_ret, s_hg, norm_g, w_in, ret_g, hg_g, lb, w_out):
    B, L, _ = x.shape
    h = rmsnorm(x, norm_g)
    p = (h @ w_in).astype(jnp.float32)
    rq, rk, rv, rg, hq, hf, hi, hgate = jnp.split(p, 8, axis=-1)

    rq = rotary(rq.reshape(B, L, RET_HEADS, RET_DK), pos)
    rk = rotary(rk.reshape(B, L, RET_HEADS, RET_DK), pos) * (RET_DK ** -0.5)
    rv = rv.reshape(B, L, RET_HEADS, RET_DV)
    log_gamma = jnp.log1p(-jnp.exp2(-5.0 - jnp.arange(RET_HEADS, dtype=jnp.float32)))
    log_gamma = jnp.broadcast_to(log_gamma[None, None, :, None], (B, L, RET_HEADS, 1))
    o_ret, s_ret_new = chunked_decay_recurrence(rq, rk, rv, log_gamma, s_ret)
    o_ret = head_rmsnorm(o_ret, ret_g) * jax.nn.silu(rg)

    lbf = lb.astype(jnp.float32)
    logf = jnp.logaddexp(jnp.log(lbf), jnp.log1p(-lbf) + jax.nn.log_sigmoid(hf))
    kin = (1.0 - lbf) * jax.nn.sigmoid(-hf)
    hq_act = jax.nn.silu(hq)
    o_hg, s_hg_new = chunked_decay_recurrence(
        hq_act.reshape(B, L, HG_HEADS, HG_DK), kin.reshape(B, L, HG_HEADS, HG_DK),
        hi.reshape(B, L, HG_HEADS, HG_DV), logf.reshape(B, L, HG_HEADS, HG_DK), s_hg)
    o_hg = head_rmsnorm(o_hg, hg_g) * jax.nn.silu(hgate)

    o_cat = jnp.concatenate([o_ret, o_hg], axis=-1).astype(x.dtype)
    y = x + o_cat @ w_out
    return y, s_ret_new.astype(s_ret.dtype), s_hg_new.astype(s_hg.dtype)


def setup_inputs(seed: int = 0) -> dict:
    key = jax.random.key(seed)
    ks = jax.random.split(key, 12)
    f32 = jnp.float32
    return {
        "x_prompt": jax.random.normal(ks[0], (BATCH, SEQ, D_MODEL), f32),
        "x_sample": jax.random.normal(ks[1], (DEC_BATCH, DEC_SEQ, D_MODEL), f32),
        "state_ret": 0.3 * jax.random.normal(ks[2], (DEPTH, DEC_BATCH, RET_HEADS, RET_DK, RET_DV), f32),
        "state_hgrn": 0.3 * jax.random.normal(ks[3], (DEPTH, DEC_BATCH, HG_HEADS, HG_DK, HG_DV), f32),
        "norm_g": 1.0 + 0.01 * jax.random.normal(ks[4], (DEPTH, D_MODEL), f32),
        "w_in": jax.random.normal(ks[5], (DEPTH, D_MODEL, N_IN_COLS), f32) * D_MODEL ** -0.5,
        "ret_norm_g": 1.0 + 0.01 * jax.random.normal(ks[6], (DEPTH, GROUP_W), f32),
        "hg_norm_g": 1.0 + 0.01 * jax.random.normal(ks[7], (DEPTH, GROUP_W), f32),
        "hg_lb": 0.5 * jax.random.normal(ks[8], (DEPTH + 1, GROUP_W), f32),
        "w_out": jax.random.normal(ks[9], (DEPTH, 2 * GROUP_W, D_MODEL), f32) * (2 * GROUP_W) ** -0.5 * 0.5,
        "final_norm_g": 1.0 + 0.01 * jax.random.normal(ks[10], (D_MODEL,), f32),
    }


def reference(x_prompt, x_sample, state_ret, state_hgrn, norm_g, w_in, ret_norm_g, hg_norm_g,
              hg_lb, w_out, final_norm_g):
    Bp, Lp, _ = x_prompt.shape
    Bs, Ls, _ = x_sample.shape
    pos_prompt = jnp.arange(Lp, dtype=jnp.float32)
    pos_sample = PAST_LEN + jnp.arange(Ls, dtype=jnp.float32)
    lb_all = jnp.cumsum(jax.nn.softmax(hg_lb.astype(jnp.float32), axis=0), axis=0)

    hp, hs = x_prompt, x_sample
    rp_list, hp_list, rs_list, hs_list = [], [], [], []
    for l in range(DEPTH):
        zero_ret = jnp.zeros((Bp, RET_HEADS, RET_DK, RET_DV), state_ret.dtype)
        zero_hg = jnp.zeros((Bp, HG_HEADS, HG_DK, HG_DV), state_hgrn.dtype)
        hp, r_p, g_p = mixer_layer(hp, pos_prompt, zero_ret, zero_hg, norm_g[l], w_in[l],
                                   ret_norm_g[l], hg_norm_g[l], lb_all[l], w_out[l])
        hs, r_s, g_s = mixer_layer(hs, pos_sample, state_ret[l], state_hgrn[l], norm_g[l], w_in[l],
                                   ret_norm_g[l], hg_norm_g[l], lb_all[l], w_out[l])
        rp_list.append(r_p); hp_list.append(g_p); rs_list.append(r_s); hs_list.append(g_s)

    y_prompt = rmsnorm(hp, final_norm_g)
    y_sample = rmsnorm(hs, final_norm_g)
    new_ret_prompt = jnp.stack(rp_list, axis=0)
    new_hgrn_prompt = jnp.stack(hp_list, axis=0)
    new_ret_sample = jnp.stack(rs_list, axis=0)
    new_hgrn_sample = jnp.stack(hs_list, axis=0)
    return (y_prompt, y_sample, new_ret_prompt, new_hgrn_prompt, new_ret_sample, new_hgrn_sample)
```

```python
import functools
import math

import jax
import jax.numpy as jnp
from jax import lax
from jax.experimental import pallas as pl
from jax.experimental.pallas import tpu as pltpu

PAST_LEN = 16384
ROPE_BASE = 10000.0
NORM_EPS = 1e-6

PROMPT_TILE = 256
HGRN_CHUNK = 128
SAMPLE_BLOCK = 8
VMEM_LIMIT_BYTES = 56 * 1024 * 1024

_F32 = jnp.float32
_BF16 = jnp.bfloat16


def _rms(x, g):
    return x * lax.rsqrt(jnp.mean(x * x, axis=-1, keepdims=True) + NORM_EPS) * g


def _sigmoid(x):
    return 1.0 / (1.0 + jnp.exp(-x))


def _silu(x):
    return x * _sigmoid(x)


def _dot(a, b):
    return jnp.dot(a, b, preferred_element_type=_F32)


def _dot_nt(a, b):
    return lax.dot_general(a, b, (((1,), (1,)), ((), ())), preferred_element_type=_F32)


def _dot_tn(a, b):
    return lax.dot_general(a, b, (((0,), (0,)), ((), ())), preferred_element_type=_F32)


def _rotate(x, cos, sin_signed, half):
    return x * cos + pltpu.roll(x, half, 1) * sin_signed


def _lower_bound(lb_ref):
    a = lb_ref[...]
    m = jnp.max(a, axis=0, keepdims=True)
    e = jnp.exp(a - m)
    return e[0:1, :] / jnp.sum(e, axis=0, keepdims=True)


def _forget_gate(hf, lbv):
    z = jnp.exp(-jnp.abs(hf))
    r = 1.0 / (1.0 + z)
    zr = z * r
    pos = hf >= 0.0
    sig_p = jnp.where(pos, r, zr)
    sig_n = jnp.where(pos, zr, r)
    one_m = 1.0 - lbv
    return lbv + one_m * sig_p, one_m * sig_n


def _block_row(b, blk, r):
    c, w = b.shape
    if blk % 16 == 0:
        b3 = b.reshape(c // blk, blk, w)
        x3 = jnp.broadcast_to(b3[:, r:r + 1, :], b3.shape)
        return x3.reshape(c, w)
    off = lax.broadcasted_iota(jnp.int32, (c, w), 0) & (blk - 1)
    x = b
    for o in range(blk):
        if o == r:
            continue
        x = jnp.where(off == o, pltpu.roll(b, (o - r) % c, 0), x)
    return x


def _split3(x):
    hi = x.astype(_BF16)
    r1 = x - hi.astype(_F32)
    mid = r1.astype(_BF16)
    lo = (r1 - mid.astype(_F32)).astype(_BF16)
    return hi, mid, lo


def _prompt_kernel(x_ref, cos_ref, sin_ref, win_ref, wout_ref, ng_ref, rg_ref, hgg_ref, lb_ref, fg_ref,
                   y_ref, sret_ref, shg_ref, st_ref, *, heads, dh, tile, chunk):
    t_idx = pl.program_id(1)
    n_t = pl.num_programs(1)
    gw = heads * dh

    @pl.when(t_idx == 0)
    def _():
        sret_ref[...] = jnp.zeros_like(sret_ref)
        st_ref[...] = jnp.zeros_like(st_ref)

    x = x_ref[0]
    h = _rms(x, ng_ref[...]).astype(_BF16)

    def proj(sec):
        return _dot(h, win_ref[:, sec * gw:(sec + 1) * gw])

    rq, rk, rv = proj(0), proj(1), proj(2)
    cos, sin = cos_ref[...], sin_ref[...]
    row = lax.broadcasted_iota(jnp.int32, (tile, dh), 0).astype(_F32)
    ti = lax.broadcasted_iota(jnp.int32, (tile, tile), 0)
    si = lax.broadcasted_iota(jnp.int32, (tile, tile), 1)
    causal = ti >= si
    lag = jnp.maximum(ti - si, 0).astype(_F32)
    o_ret = []
    for hh in range(heads):
        sl = slice(hh * dh, (hh + 1) * dh)
        lg = math.log1p(-(2.0 ** (-5 - hh)))
        q = _rotate(rq[:, sl], cos, sin, dh // 2)
        k = _rotate(rk[:, sl], cos, sin, dh // 2) * (dh ** -0.5)
        v = rv[:, sl].astype(_BF16)
        s_old = sret_ref[0, hh]
        qd = (q * jnp.exp(lg * (row + 1.0))).astype(_BF16)
        kd = (k * jnp.exp(lg * (tile - 1.0 - row))).astype(_BF16)
        decay = jnp.where(causal, jnp.exp(lg * lag), 0.0)
        a = _dot_nt(q.astype(_BF16), k.astype(_BF16)) * decay
        o_ret.append(_dot(qd, s_old.astype(_BF16)) + _dot(a.astype(_BF16), v))
        sret_ref[0, hh] = math.exp(lg * tile) * s_old + _dot_tn(kd, v)
    rgate = proj(3)
    o_ret = [_rms(o, rg_ref[:, hh * dh:(hh + 1) * dh]) * _silu(rgate[:, hh * dh:(hh + 1) * dh])
             for hh, o in enumerate(o_ret)]

    lbv = _lower_bound(lb_ref)
    hq, hf, hi = proj(4), proj(5), proj(6)
    f, kin = _forget_gate(hf, lbv)
    g = jnp.log(f)
    qa = _silu(hq)
    ci = lax.broadcasted_iota(jnp.int32, (chunk, chunk), 0)
    cj = lax.broadcasted_iota(jnp.int32, (chunk, chunk), 1)
    cxor = ci ^ cj
    lower = ci > cj
    tri = (ci >= cj).astype(_BF16)
    crow = lax.broadcasted_iota(jnp.int32, (chunk, gw), 0)
    n_lev = chunk.bit_length() - 1
    o_hg_chunks = [[] for _ in range(heads)]
    for c in range(tile // chunk):
        rs = slice(c * chunk, (c + 1) * chunk)
        gq, gk, gv, gg = qa[rs], kin[rs], hi[rs].astype(_BF16), g[rs]
        ghi, gmid, glo = _split3(gg)
        b = _dot(tri, ghi) + _dot(tri, gmid) + _dot(tri, glo)
        b_end = jnp.broadcast_to(b[chunk - 1:chunk, :], b.shape)
        q_top = (gq * jnp.exp(b)).astype(_BF16)
        k_top = (gk * jnp.exp(b_end - b)).astype(_BF16)
        q_lv, k_lv = [], []
        for lev in range(1, n_lev + 1):
            blk, half = 1 << lev, 1 << (lev - 1)
            e = jnp.exp(-jnp.abs(b - _block_row(b, blk, half - 1)))
            upper = (crow & half) != 0
            q_lv.append(jnp.where(upper, gq * e, 0.0).astype(_BF16))
            k_lv.append(jnp.where(upper, 0.0, gk * e).astype(_BF16))
        for hh in range(heads):
            sl = slice(hh * dh, (hh + 1) * dh)
            qh, kh = gq[:, sl], gk[:, sl]
            a = jnp.where(ci == cj, jnp.sum(qh * kh, axis=-1, keepdims=True), 0.0)
            for lev in range(1, n_lev + 1):
                p = _dot_nt(q_lv[lev - 1][:, sl], k_lv[lev - 1][:, sl])
                a = jnp.where(lower & ((cxor >> (lev - 1)) == 1), p, a)
            st_old = st_ref[hh]
            o = _dot_nt(q_top[:, sl], st_old.astype(_BF16)) + _dot(a.astype(_BF16), gv[:, sl])
            o_hg_chunks[hh].append(o)
            st_ref[hh] = st_old * jnp.exp(b[chunk - 1:chunk, sl]) + _dot_tn(gv[:, sl], k_top[:, sl])
    hgate = proj(7)
    o_hg = []
    for hh in range(heads):
        o = jnp.concatenate(o_hg_chunks[hh], axis=0)
        o_hg.append(_rms(o, hgg_ref[:, hh * dh:(hh + 1) * dh]) * _silu(hgate[:, hh * dh:(hh + 1) * dh]))

    o_cat = jnp.concatenate(o_ret + o_hg, axis=-1).astype(_BF16)
    y = x + _dot(o_cat, wout_ref[...])
    y_ref[0] = _rms(y, fg_ref[...])

    @pl.when(t_idx == n_t - 1)
    def _():
        for hh in range(heads):
            shg_ref[0, hh] = st_ref[hh].T


def _prompt_call(x, cos, sin, win, wout, ng, rg, hgg, lb, fg, heads, dh):
    bsz, seq, d = x.shape
    tile = min(PROMPT_TILE, seq)
    chunk = min(HGRN_CHUNK, tile)
    assert seq % tile == 0 and tile % chunk == 0 and chunk & (chunk - 1) == 0
    gw = heads * dh
    const = lambda *shape: pl.BlockSpec(shape, lambda b, t: (0,) * len(shape))
    state_spec = pl.BlockSpec((1, heads, dh, dh), lambda b, t: (b, 0, 0, 0))
    body = functools.partial(_prompt_kernel, heads=heads, dh=dh, tile=tile, chunk=chunk)
    return pl.pallas_call(
        body,
        grid=(bsz, seq // tile),
        in_specs=[
            pl.BlockSpec((1, tile, d), lambda b, t: (b, t, 0)),
            pl.BlockSpec((tile, dh), lambda b, t: (t, 0)),
            pl.BlockSpec((tile, dh), lambda b, t: (t, 0)),
            const(d, 8 * gw), const(2 * gw, d), const(1, d), const(1, gw), const(1, gw),
            const(lb.shape[0], gw), const(1, d),
        ],
        out_specs=[pl.BlockSpec((1, tile, d), lambda b, t: (b, t, 0)), state_spec, state_spec],
        out_shape=[
            jax.ShapeDtypeStruct((bsz, seq, d), _F32),
            jax.ShapeDtypeStruct((bsz, heads, dh, dh), _F32),
            jax.ShapeDtypeStruct((bsz, heads, dh, dh), _F32),
        ],
        scratch_shapes=[pltpu.VMEM((heads, dh, dh), _F32)],
        compiler_params=pltpu.CompilerParams(
            dimension_semantics=("arbitrary", "arbitrary"), vmem_limit_bytes=VMEM_LIMIT_BYTES),
        name="prompt_mixer",
    )(x, cos, sin, win, wout, ng, rg, hgg, lb, fg)


def _sample_kernel(x_ref, cos_ref, sin_ref, win_ref, wout_ref, ng_ref, rg_ref, hgg_ref, lb_ref, fg_ref,
                   sret_in, shg_in, y_ref, sret_out, shg_out,
                   rv_s, rgate_s, hi_s, hgate_s, rqT_s, rkT_s, hqT_s, hfT_s, hkT_s, *, heads, dh, block):
    i = pl.program_id(0)
    gw = heads * dh
    nb = x_ref.shape[0]

    @pl.when(i == 0)
    def _():
        x = x_ref[...]
        h = _rms(x, ng_ref[...]).astype(_BF16)

        def proj(sec):
            return _dot(h, win_ref[:, sec * gw:(sec + 1) * gw])

        cos, sin = cos_ref[...], sin_ref[...]

        def rot(p):
            return jnp.concatenate(
                [_rotate(p[:, hh * dh:(hh + 1) * dh], cos, sin, dh // 2) for hh in range(heads)], axis=-1)

        rqT_s[...] = rot(proj(0)).T
        rkT_s[...] = (rot(proj(1)) * (dh ** -0.5)).T
        rv_s[...] = proj(2)
        rgate_s[...] = _silu(proj(3))
        f, kin = _forget_gate(proj(5), _lower_bound(lb_ref))
        hqT_s[...] = _silu(proj(4)).T
        hfT_s[...] = f.T
        hkT_s[...] = kin.T
        hi_s[...] = proj(6)
        hgate_s[...] = _silu(proj(7))

    rows = pl.ds(pl.multiple_of(i * block, block), block)
    shift = (nb - i * block) % nb
    rqT = pltpu.roll(rqT_s[...], shift, 1)
    rkT = pltpu.roll(rkT_s[...], shift, 1)
    hqT = pltpu.roll(hqT_s[...], shift, 1)
    hfT = pltpu.roll(hfT_s[...], shift, 1)
    hkT = pltpu.roll(hkT_s[...], shift, 1)
    rv, hi = rv_s[rows, :], hi_s[rows, :]

    def col(a, hh, bi):
        return jnp.broadcast_to(a[hh * dh:(hh + 1) * dh, bi:bi + 1], (dh, dh))

    o_ret_rows, o_hg_rows = [], []
    for bi in range(block):
        o_r, o_h = [], []
        for hh in range(heads):
            sl = slice(hh * dh, (hh + 1) * dh)
            gamma = 1.0 - 2.0 ** (-5 - hh)
            s_new = gamma * sret_in[bi, hh] + col(rkT, hh, bi) * rv[bi:bi + 1, sl]
            sret_out[bi, hh] = s_new
            o_r.append(jnp.sum(s_new * col(rqT, hh, bi), axis=0, keepdims=True))
            g_new = col(hfT, hh, bi) * shg_in[bi, hh] + col(hkT, hh, bi) * hi[bi:bi + 1, sl]
            shg_out[bi, hh] = g_new
            o_h.append(jnp.sum(g_new * col(hqT, hh, bi), axis=0, keepdims=True))
        o_ret_rows.append(o_r)
        o_hg_rows.append(o_h)

    def finish(o_rows, gain_ref, gate):
        outs = []
        for hh in range(heads):
            sl = slice(hh * dh, (hh + 1) * dh)
            o = jnp.concatenate([o_rows[bi][hh] for bi in range(block)], axis=0)
            outs.append(_rms(o, gain_ref[:, sl]) * gate[:, sl])
        return outs

    o_cat = jnp.concatenate(
        finish(o_ret_rows, rg_ref, rgate_s[rows, :]) + finish(o_hg_rows, hgg_ref, hgate_s[rows, :]),
        axis=-1).astype(_BF16)
    y = x_ref[rows, :] + _dot(o_cat, wout_ref[...])
    y_ref[...] = _rms(y, fg_ref[...])


def _sample_call(x, cos, sin, win, wout, ng, rg, hgg, lb, fg, sret, shg):
    nb, d = x.shape
    _, heads, dh, _ = sret.shape
    gw = heads * dh
    block = min(SAMPLE_BLOCK, nb)
    assert nb % block == 0 and nb % 128 == 0
    const = lambda *shape: pl.BlockSpec(shape, lambda i: (0,) * len(shape))
    state_spec = pl.BlockSpec((block, heads, dh, dh), lambda i: (i, 0, 0, 0))
    body = functools.partial(_sample_kernel, heads=heads, dh=dh, block=block)
    row_scratch = pltpu.VMEM((nb, gw), _F32)
    col_scratch = pltpu.VMEM((gw, nb), _F32)
    return pl.pallas_call(
        body,
        grid=(nb // block,),
        in_specs=[
            const(nb, d), const(1, dh), const(1, dh),
            const(d, 8 * gw), const(2 * gw, d), const(1, d), const(1, gw), const(1, gw),
            const(lb.shape[0], gw), const(1, d),
            state_spec, state_spec,
        ],
        out_specs=[pl.BlockSpec((block, d), lambda i: (i, 0)), state_spec, state_spec],
        out_shape=[
            jax.ShapeDtypeStruct((nb, d), _F32),
            jax.ShapeDtypeStruct(sret.shape, _F32),
            jax.ShapeDtypeStruct(shg.shape, _F32),
        ],
        scratch_shapes=[row_scratch] * 4 + [col_scratch] * 5,
        compiler_params=pltpu.CompilerParams(
            dimension_semantics=("arbitrary",), vmem_limit_bytes=VMEM_LIMIT_BYTES),
        name="sample_mixer",
    )(x, cos, sin, win, wout, ng, rg, hgg, lb, fg, sret, shg)


def _rope_tables(pos, dh):
    half = dh // 2
    freqs = 1.0 / (ROPE_BASE ** (jnp.arange(half, dtype=_F32) / half))
    ang = pos[:, None] * freqs[None, :]
    c, s = jnp.cos(ang), jnp.sin(ang)
    return jnp.concatenate([c, c], axis=-1), jnp.concatenate([-s, s], axis=-1)


def kernel(x_prompt, x_sample, state_ret, state_hgrn, norm_g, w_in, ret_norm_g, hg_norm_g, hg_lb, w_out,
           final_norm_g):
    depth, _, heads, dh, _ = state_ret.shape
    assert depth == 1 and x_sample.shape[1] == 1
    bp, lp, d = x_prompt.shape
    win = w_in[0].astype(_BF16)
    wout = w_out[0].astype(_BF16)
    ng, fg = norm_g[0][None, :], final_norm_g[None, :]
    rg, hgg = ret_norm_g[0][None, :], hg_norm_g[0][None, :]

    cos_p, sin_p = _rope_tables(jnp.arange(lp, dtype=_F32), dh)
    y_p, ret_p, hg_p = _prompt_call(x_prompt, cos_p, sin_p, win, wout, ng, rg, hgg, hg_lb, fg, heads, dh)

    cos_s, sin_s = _rope_tables(PAST_LEN + jnp.arange(1, dtype=_F32), dh)
    y_s, ret_s, hg_s = _sample_call(x_sample[:, 0, :], cos_s, sin_s, win, wout, ng, rg, hgg, hg_lb, fg,
                                    state_ret[0], state_hgrn[0])
    return (y_p, y_s[:, None, :], ret_p[None], hg_p[None], ret_s[None], hg_s[None])
```

```python
import functools
import math

import jax
import jax.numpy as jnp
from jax import lax
from jax.experimental import pallas as pl
from jax.experimental.pallas import tpu as pltpu

PAST_LEN = 16384
ROPE_BASE = 10000.0
NORM_EPS = 1e-6

PROMPT_TILE = 256
HGRN_CHUNK = 128
SAMPLE_BLOCK = 8
VMEM_LIMIT_BYTES = 56 * 1024 * 1024

_F32 = jnp.float32
_BF16 = jnp.bfloat16


def _rms(x, g):
    return x * lax.rsqrt(jnp.mean(x * x, axis=-1, keepdims=True) + NORM_EPS) * g


def _sigmoid(x):
    return 1.0 / (1.0 + jnp.exp(-x))


def _silu(x):
    return x * _sigmoid(x)


def _dot(a, b):
    return jnp.dot(a, b, preferred_element_type=_F32)


def _dot_nt(a, b):
    return lax.dot_general(a, b, (((1,), (1,)), ((), ())), preferred_element_type=_F32)


def _dot_tn(a, b):
    return lax.dot_general(a, b, (((0,), (0,)), ((), ())), preferred_element_type=_F32)


def _rotate(x, cos, sin_signed, half):
    return x * cos + pltpu.roll(x, half, 1) * sin_signed


def _lower_bound(lb_ref):
    a = lb_ref[...]
    m = jnp.max(a, axis=0, keepdims=True)
    e = jnp.exp(a - m)
    return e[0:1, :] / jnp.sum(e, axis=0, keepdims=True)


def _forget_gate(hf, lbv):
    z = jnp.exp(-jnp.abs(hf))
    r = 1.0 / (1.0 + z)
    zr = z * r
    pos = hf >= 0.0
    sig_p = jnp.where(pos, r, zr)
    sig_n = jnp.where(pos, zr, r)
    one_m = 1.0 - lbv
    return lbv + one_m * sig_p, one_m * sig_n


def _decay_levels(qa, kin, f):
    c, w = f.shape
    n_lev = c.bit_length() - 1
    rowi = lax.broadcasted_iota(jnp.int32, (c, w), 0)
    tot, q, k = f, qa * f, kin
    q_lv, k_lv = [], []
    for lev in range(3):
        half = 1 << lev
        up = (rowi & half) != 0
        q_lv.append(jnp.where(up, q, 0.0).astype(_BF16))
        k_lv.append(jnp.where(up, 0.0, k).astype(_BF16))
        sib = jnp.where(up, pltpu.roll(tot, half, 0), pltpu.roll(tot, c - half, 0))
        q = jnp.where(up, q * sib, q)
        k = jnp.where(up, k, k * sib)
        tot = tot * sib
    q_b = [q[i * 8:(i + 1) * 8] for i in range(c // 8)]
    k_b = [k[i * 8:(i + 1) * 8] for i in range(c // 8)]
    tot_b = [tot[i * 8:i * 8 + 1] for i in range(c // 8)]
    for lev in range(3, n_lev):
        half = 1 << lev
        zero = jnp.zeros((half, w), _F32)
        q_lv.append(jnp.concatenate([q_b[i] if i % 2 else zero for i in range(len(q_b))], axis=0).astype(_BF16))
        k_lv.append(jnp.concatenate([zero if i % 2 else k_b[i] for i in range(len(k_b))], axis=0).astype(_BF16))
        q_b = [jnp.concatenate([q_b[i], q_b[i + 1] * tot_b[i]], axis=0) for i in range(0, len(q_b), 2)]
        k_b = [jnp.concatenate([k_b[i] * tot_b[i + 1], k_b[i + 1]], axis=0) for i in range(0, len(k_b), 2)]
        tot_b = [tot_b[i] * tot_b[i + 1] for i in range(0, len(tot_b), 2)]
    return q_lv, k_lv, q_b[0], k_b[0], tot_b[0]


def _prompt_kernel(x_ref, cos_ref, sin_ref, win_ref, wout_ref, ng_ref, rg_ref, hgg_ref, lb_ref, fg_ref,
                   y_ref, sret_ref, shg_ref, st_ref, *, heads, dh, tile, chunk):
    t_idx = pl.program_id(1)
    n_t = pl.num_programs(1)
    gw = heads * dh

    @pl.when(t_idx == 0)
    def _():
        sret_ref[...] = jnp.zeros_like(sret_ref)
        st_ref[...] = jnp.zeros_like(st_ref)

    x = x_ref[0]
    h = _rms(x, ng_ref[...]).astype(_BF16)

    def proj(sec):
        return _dot(h, win_ref[:, sec * gw:(sec + 1) * gw])

    rq, rk, rv = proj(0), proj(1), proj(2)
    cos, sin = cos_ref[...], sin_ref[...]
    row = lax.broadcasted_iota(jnp.int32, (tile, dh), 0).astype(_F32)
    ti = lax.broadcasted_iota(jnp.int32, (tile, tile), 0)
    si = lax.broadcasted_iota(jnp.int32, (tile, tile), 1)
    causal = ti >= si
    lag = jnp.maximum(ti - si, 0).astype(_F32)
    o_ret = []
    for hh in range(heads):
        sl = slice(hh * dh, (hh + 1) * dh)
        lg = math.log1p(-(2.0 ** (-5 - hh)))
        q = _rotate(rq[:, sl], cos, sin, dh // 2)
        k = _rotate(rk[:, sl], cos, sin, dh // 2) * (dh ** -0.5)
        v = rv[:, sl].astype(_BF16)
        s_old = sret_ref[0, hh]
        qd = (q * jnp.exp(lg * (row + 1.0))).astype(_BF16)
        kd = (k * jnp.exp(lg * (tile - 1.0 - row))).astype(_BF16)
        decay = jnp.where(causal, jnp.exp(lg * lag), 0.0)
        a = _dot_nt(q.astype(_BF16), k.astype(_BF16)) * decay
        o_ret.append(_dot(qd, s_old.astype(_BF16)) + _dot(a.astype(_BF16), v))
        sret_ref[0, hh] = math.exp(lg * tile) * s_old + _dot_tn(kd, v)
    rgate = proj(3)
    o_ret = [_rms(o, rg_ref[:, hh * dh:(hh + 1) * dh]) * _silu(rgate[:, hh * dh:(hh + 1) * dh])
             for hh, o in enumerate(o_ret)]

    lbv = _lower_bound(lb_ref)
    hq, hf, hi = proj(4), proj(5), proj(6)
    f, kin = _forget_gate(hf, lbv)
    qa = _silu(hq)
    ci = lax.broadcasted_iota(jnp.int32, (chunk, chunk), 0)
    cj = lax.broadcasted_iota(jnp.int32, (chunk, chunk), 1)
    cxor = ci ^ cj
    n_lev = chunk.bit_length() - 1
    diag = ci == cj
    lev_mask = [(ci > cj) & ((cxor >> lev) == 1) for lev in range(n_lev)]
    o_hg_chunks = [[] for _ in range(heads)]
    for c in range(tile // chunk):
        rs = slice(c * chunk, (c + 1) * chunk)
        gq, gk, gv = qa[rs], kin[rs], hi[rs].astype(_BF16)
        q_lv, k_lv, q_top, k_top, f_tot = _decay_levels(gq, gk, f[rs])
        q_top, k_top = q_top.astype(_BF16), k_top.astype(_BF16)
        for hh in range(heads):
            sl = slice(hh * dh, (hh + 1) * dh)
            a = jnp.where(diag, jnp.sum(gq[:, sl] * gk[:, sl], axis=-1, keepdims=True), 0.0)
            for lev in range(n_lev):
                a = jnp.where(lev_mask[lev], _dot_nt(q_lv[lev][:, sl], k_lv[lev][:, sl]), a)
            st_old = st_ref[hh]
            o = _dot_nt(q_top[:, sl], st_old.astype(_BF16)) + _dot(a.astype(_BF16), gv[:, sl])
            o_hg_chunks[hh].append(o)
            st_ref[hh] = st_old * f_tot[:, sl] + _dot_tn(gv[:, sl], k_top[:, sl])
    hgate = proj(7)
    o_hg = []
    for hh in range(heads):
        o = jnp.concatenate(o_hg_chunks[hh], axis=0)
        o_hg.append(_rms(o, hgg_ref[:, hh * dh:(hh + 1) * dh]) * _silu(hgate[:, hh * dh:(hh + 1) * dh]))

    o_cat = jnp.concatenate(o_ret + o_hg, axis=-1).astype(_BF16)
    y = x + _dot(o_cat, wout_ref[...])
    y_ref[0] = _rms(y, fg_ref[...])

    @pl.when(t_idx == n_t - 1)
    def _():
        for hh in range(heads):
            shg_ref[0, hh] = st_ref[hh].T


def _prompt_call(x, cos, sin, win, wout, ng, rg, hgg, lb, fg, heads, dh):
    bsz, seq, d = x.shape
    tile = min(PROMPT_TILE, seq)
    chunk = min(HGRN_CHUNK, tile)
    assert seq % tile == 0 and tile % chunk == 0 and chunk & (chunk - 1) == 0
    gw = heads * dh
    const = lambda *shape: pl.BlockSpec(shape, lambda b, t: (0,) * len(shape))
    state_spec = pl.BlockSpec((1, heads, dh, dh), lambda b, t: (b, 0, 0, 0))
    body = functools.partial(_prompt_kernel, heads=heads, dh=dh, tile=tile, chunk=chunk)
    return pl.pallas_call(
        body,
        grid=(bsz, seq // tile),
        in_specs=[
            pl.BlockSpec((1, tile, d), lambda b, t: (b, t, 0)),
            pl.BlockSpec((tile, dh), lambda b, t: (t, 0)),
            pl.BlockSpec((tile, dh), lambda b, t: (t, 0)),
            const(d, 8 * gw), const(2 * gw, d), const(1, d), const(1, gw), const(1, gw),
            const(lb.shape[0], gw), const(1, d),
        ],
        out_specs=[pl.BlockSpec((1, tile, d), lambda b, t: (b, t, 0)), state_spec, state_spec],
        out_shape=[
            jax.ShapeDtypeStruct((bsz, seq, d), _F32),
            jax.ShapeDtypeStruct((bsz, heads, dh, dh), _F32),
            jax.ShapeDtypeStruct((bsz, heads, dh, dh), _F32),
        ],
        scratch_shapes=[pltpu.VMEM((heads, dh, dh), _F32)],
        compiler_params=pltpu.CompilerParams(
            dimension_semantics=("arbitrary", "arbitrary"), vmem_limit_bytes=VMEM_LIMIT_BYTES),
        name="prompt_mixer",
    )(x, cos, sin, win, wout, ng, rg, hgg, lb, fg)


def _sample_kernel(x_ref, cos_ref, sin_ref, win_ref, wout_ref, ng_ref, rg_ref, hgg_ref, lb_ref, fg_ref,
                   sret_in, shg_in, y_ref, sret_out, shg_out,
                   rv_s, rgate_s, hi_s, hgate_s, rqT_s, rkT_s, hqT_s, hfT_s, hkT_s, *, heads, dh, block):
    i = pl.program_id(0)
    gw = heads * dh
    nb = x_ref.shape[0]

    @pl.when(i == 0)
    def _():
        x = x_ref[...]
        h = _rms(x, ng_ref[...]).astype(_BF16)

        def proj(sec):
            return _dot(h, win_ref[:, sec * gw:(sec + 1) * gw])

        cos, sin = cos_ref[...], sin_ref[...]

        def rot(p):
            return jnp.concatenate(
                [_rotate(p[:, hh * dh:(hh + 1) * dh], cos, sin, dh // 2) for hh in range(heads)], axis=-1)

        rqT_s[...] = rot(proj(0)).T
        rkT_s[...] = (rot(proj(1)) * (dh ** -0.5)).T
        rv_s[...] = proj(2)
        rgate_s[...] = _silu(proj(3))
        f, kin = _forget_gate(proj(5), _lower_bound(lb_ref))
        hqT_s[...] = _silu(proj(4)).T
        hfT_s[...] = f.T
        hkT_s[...] = kin.T
        hi_s[...] = proj(6)
        hgate_s[...] = _silu(proj(7))

    rows = pl.ds(pl.multiple_of(i * block, block), block)
    shift = (nb - i * block) % nb
    rqT = pltpu.roll(rqT_s[...], shift, 1)
    rkT = pltpu.roll(rkT_s[...], shift, 1)
    hqT = pltpu.roll(hqT_s[...], shift, 1)
    hfT = pltpu.roll(hfT_s[...], shift, 1)
    hkT = pltpu.roll(hkT_s[...], shift, 1)
    rv, hi = rv_s[rows, :], hi_s[rows, :]

    def col(a, hh, bi):
        return jnp.broadcast_to(a[hh * dh:(hh + 1) * dh, bi:bi + 1], (dh, dh))

    o_ret_rows, o_hg_rows = [], []
    for bi in range(block):
        o_r, o_h = [], []
        for hh in range(heads):
            sl = slice(hh * dh, (hh + 1) * dh)
            gamma = 1.0 - 2.0 ** (-5 - hh)
            s_new = gamma * sret_in[bi, hh] + col(rkT, hh, bi) * rv[bi:bi + 1, sl]
            sret_out[bi, hh] = s_new
            o_r.append(jnp.sum(s_new * col(rqT, hh, bi), axis=0, keepdims=True))
            g_new = col(hfT, hh, bi) * shg_in[bi, hh] + col(hkT, hh, bi) * hi[bi:bi + 1, sl]
            shg_out[bi, hh] = g_new
            o_h.append(jnp.sum(g_new * col(hqT, hh, bi), axis=0, keepdims=True))
        o_ret_rows.append(o_r)
        o_hg_rows.append(o_h)

    def finish(o_rows, gain_ref, gate):
        outs = []
        for hh in range(heads):
            sl = slice(hh * dh, (hh + 1) * dh)
            o = jnp.concatenate([o_rows[bi][hh] for bi in range(block)], axis=0)
            outs.append(_rms(o, gain_ref[:, sl]) * gate[:, sl])
        return outs

    o_cat = jnp.concatenate(
        finish(o_ret_rows, rg_ref, rgate_s[rows, :]) + finish(o_hg_rows, hgg_ref, hgate_s[rows, :]),
        axis=-1).astype(_BF16)
    y = x_ref[rows, :] + _dot(o_cat, wout_ref[...])
    y_ref[...] = _rms(y, fg_ref[...])


def _sample_call(x, cos, sin, win, wout, ng, rg, hgg, lb, fg, sret, shg):
    nb, d = x.shape
    _, heads, dh, _ = sret.shape
    gw = heads * dh
    block = min(SAMPLE_BLOCK, nb)
    assert nb % block == 0 and nb % 128 == 0
    const = lambda *shape: pl.BlockSpec(shape, lambda i: (0,) * len(shape))
    state_spec = pl.BlockSpec((block, heads, dh, dh), lambda i: (i, 0, 0, 0))
    body = functools.partial(_sample_kernel, heads=heads, dh=dh, block=block)
    row_scratch = pltpu.VMEM((nb, gw), _F32)
    col_scratch = pltpu.VMEM((gw, nb), _F32)
    return pl.pallas_call(
        body,
        grid=(nb // block,),
        in_specs=[
            const(nb, d), const(1, dh), const(1, dh),
            const(d, 8 * gw), const(2 * gw, d), const(1, d), const(1, gw), const(1, gw),
            const(lb.shape[0], gw), const(1, d),
            state_spec, state_spec,
        ],
        out_specs=[pl.BlockSpec((block, d), lambda i: (i, 0)), state_spec, state_spec],
        out_shape=[
            jax.ShapeDtypeStruct((nb, d), _F32),
            jax.ShapeDtypeStruct(sret.shape, _F32),
            jax.ShapeDtypeStruct(shg.shape, _F32),
        ],
        scratch_shapes=[row_scratch] * 4 + [col_scratch] * 5,
        compiler_params=pltpu.CompilerParams(
            dimension_semantics=("arbitrary",), vmem_limit_bytes=VMEM_LIMIT_BYTES),
        name="sample_mixer",
    )(x, cos, sin, win, wout, ng, rg, hgg, lb, fg, sret, shg)


def _rope_tables(pos, dh):
    half = dh // 2
    freqs = 1.0 / (ROPE_BASE ** (jnp.arange(half, dtype=_F32) / half))
    ang = pos[:, None] * freqs[None, :]
    c, s = jnp.cos(ang), jnp.sin(ang)
    return jnp.concatenate([c, c], axis=-1), jnp.concatenate([-s, s], axis=-1)


def kernel(x_prompt, x_sample, state_ret, state_hgrn, norm_g, w_in, ret_norm_g, hg_norm_g, hg_lb, w_out,
           final_norm_g):
    depth, _, heads, dh, _ = state_ret.shape
    assert depth == 1 and x_sample.shape[1] == 1
    bp, lp, d = x_prompt.shape
    win = w_in[0].astype(_BF16)
    wout = w_out[0].astype(_BF16)
    ng, fg = norm_g[0][None, :], final_norm_g[None, :]
    rg, hgg = ret_norm_g[0][None, :], hg_norm_g[0][None, :]

    cos_p, sin_p = _rope_tables(jnp.arange(lp, dtype=_F32), dh)
    y_p, ret_p, hg_p = _prompt_call(x_prompt, cos_p, sin_p, win, wout, ng, rg, hgg, hg_lb, fg, heads, dh)

    cos_s, sin_s = _rope_tables(PAST_LEN + jnp.arange(1, dtype=_F32), dh)
    y_s, ret_s, hg_s = _sample_call(x_sample[:, 0, :], cos_s, sin_s, win, wout, ng, rg, hgg, hg_lb, fg,
                                    state_ret[0], state_hgrn[0])
    return (y_p, y_s[:, None, :], ret_p[None], hg_p[None], ret_s[None], hg_s[None])
```

```python
import functools
import math

import jax
import jax.numpy as jnp
from jax import lax
from jax.experimental import pallas as pl
from jax.experimental.pallas import tpu as pltpu

PAST_LEN = 16384
ROPE_BASE = 10000.0
NORM_EPS = 1e-6

PROMPT_TILE = 256
HGRN_CHUNK = 128
SAMPLE_BLOCK = 8
VMEM_LIMIT_BYTES = 56 * 1024 * 1024

_F32 = jnp.float32
_BF16 = jnp.bfloat16


def _rms(x, g):
    return x * lax.rsqrt(jnp.mean(x * x, axis=-1, keepdims=True) + NORM_EPS) * g


def _sigmoid(x):
    return 1.0 / (1.0 + jnp.exp(-x))


def _silu(x):
    return x * _sigmoid(x)


def _dot(a, b):
    return jnp.dot(a, b, preferred_element_type=_F32)


def _dot_nt(a, b):
    return lax.dot_general(a, b, (((1,), (1,)), ((), ())), preferred_element_type=_F32)


def _dot_tn(a, b):
    return lax.dot_general(a, b, (((0,), (0,)), ((), ())), preferred_element_type=_F32)


def _rotate(x, cos, sin_signed, half):
    return x * cos + pltpu.roll(x, half, 1) * sin_signed


def _lower_bound(lb_ref):
    a = lb_ref[...]
    m = jnp.max(a, axis=0, keepdims=True)
    e = jnp.exp(a - m)
    return e[0:1, :] / jnp.sum(e, axis=0, keepdims=True)


def _forget_gate(hf, lbv):
    z = jnp.exp(-jnp.abs(hf))
    r = 1.0 / (1.0 + z)
    zr = z * r
    pos = hf >= 0.0
    sig_p = jnp.where(pos, r, zr)
    sig_n = jnp.where(pos, zr, r)
    one_m = 1.0 - lbv
    return lbv + one_m * sig_p, one_m * sig_n


def _decay_levels(qa, kin, f):
    c, w = f.shape
    n_lev = c.bit_length() - 1
    rowi = lax.broadcasted_iota(jnp.int32, (c, w), 0)
    tot, q, k = f, qa * f, kin
    q_lv, k_lv = [], []
    for lev in range(3):
        half = 1 << lev
        up = (rowi & half) != 0
        q_lv.append(jnp.where(up, q, 0.0).astype(_BF16))
        k_lv.append(jnp.where(up, 0.0, k).astype(_BF16))
        sib = jnp.where(up, pltpu.roll(tot, half, 0), pltpu.roll(tot, c - half, 0))
        q = jnp.where(up, q * sib, q)
        k = jnp.where(up, k, k * sib)
        tot = tot * sib
    q_b = [q[i * 8:(i + 1) * 8] for i in range(c // 8)]
    k_b = [k[i * 8:(i + 1) * 8] for i in range(c // 8)]
    tot_b = [tot[i * 8:i * 8 + 1] for i in range(c // 8)]
    for lev in range(3, n_lev):
        half = 1 << lev
        zero = jnp.zeros((half, w), _F32)
        q_lv.append(jnp.concatenate([q_b[i] if i % 2 else zero for i in range(len(q_b))], axis=0).astype(_BF16))
        k_lv.append(jnp.concatenate([zero if i % 2 else k_b[i] for i in range(len(k_b))], axis=0).astype(_BF16))
        q_b = [jnp.concatenate([q_b[i], q_b[i + 1] * tot_b[i]], axis=0) for i in range(0, len(q_b), 2)]
        k_b = [jnp.concatenate([k_b[i] * tot_b[i + 1], k_b[i + 1]], axis=0) for i in range(0, len(k_b), 2)]
        tot_b = [tot_b[i] * tot_b[i + 1] for i in range(0, len(tot_b), 2)]
    return q_lv, k_lv, q_b[0], k_b[0], tot_b[0]


def _proj_stages(x_ref, rows, ng_ref, win_ref, p_ref, gw):
    h = _rms(x_ref[0, rows, :], ng_ref[...]).astype(_BF16)
    yield
    width = gw // 2
    for sec in range(8 * gw // width):
        cols = slice(sec * width, (sec + 1) * width)
        p_ref[:, cols] = _dot(h, win_ref[:, cols])
        yield


def _filler(stages):
    def fill(n):
        if n is None:
            for _ in stages:
                pass
        else:
            for _ in range(n):
                next(stages, None)
    return fill


def _log_gamma(hh):
    return math.log1p(-(2.0 ** (-5 - hh)))


def _retention_tables(rq_ref, rk_ref, rdec_ref, heads, tile):
    row = lax.broadcasted_iota(jnp.int32, rq_ref.shape[1:], 0).astype(_F32)
    ti = lax.broadcasted_iota(jnp.int32, (tile, tile), 0)
    si = lax.broadcasted_iota(jnp.int32, (tile, tile), 1)
    lag = jnp.maximum(ti - si, 0).astype(_F32)
    for hh in range(heads):
        lg = _log_gamma(hh)
        rq_ref[hh] = jnp.exp(lg * (row + 1.0))
        rk_ref[hh] = jnp.exp(lg * (tile - 1.0 - row))
        rdec_ref[hh] = jnp.where(ti >= si, jnp.exp(lg * lag), 0.0)


def _mixer(x_ref, rows, p_ref, fill, *, cos_ref, sin_ref, wout_ref, rg_ref, hgg_ref, lb_ref, fg_ref,
           y_ref, sret_ref, shg_ref, rq_ref, rk_ref, rdec_ref, heads, dh, tile, chunk):
    gw = heads * dh

    def p(sec, hh=None, rs=slice(None)):
        if hh is None:
            return p_ref[rs, sec * gw:(sec + 1) * gw]
        return p_ref[rs, sec * gw + hh * dh:sec * gw + (hh + 1) * dh]

    o_all = [None] * (2 * heads)

    cos, sin = cos_ref[rows, :], sin_ref[rows, :]

    def ret_issue(hh):
        q = _rotate(p(0, hh), cos, sin, dh // 2).astype(_BF16)
        kt = (_rotate(p(1, hh), cos, sin, dh // 2) * (dh ** -0.5)).astype(_BF16).T
        v = p(2, hh)
        s_old = sret_ref[0, hh]
        qk_qs = _dot(q, jnp.concatenate([kt, s_old.astype(_BF16)], axis=1))
        sret_ref[0, hh] = math.exp(tile * _log_gamma(hh)) * s_old + _dot(kt, (v * rk_ref[hh]).astype(_BF16))
        return hh, qk_qs, v

    def ret_finish(hh, qk_qs, v):
        sl = slice(hh * dh, (hh + 1) * dh)
        a = qk_qs[:, :tile] * rdec_ref[hh]
        o = qk_qs[:, tile:] * rq_ref[hh] + _dot(a.astype(_BF16), v.astype(_BF16))
        o_all[hh] = (_rms(o, rg_ref[:, sl]) * _silu(p(3, hh))).astype(_BF16)

    lbv = _lower_bound(lb_ref)
    ci = lax.broadcasted_iota(jnp.int32, (chunk, chunk), 0)
    cj = lax.broadcasted_iota(jnp.int32, (chunk, chunk), 1)
    cxor = ci ^ cj
    n_lev = chunk.bit_length() - 1
    diag = ci == cj
    lev_mask = [(ci > cj) & ((cxor >> lev) == 1) for lev in range(n_lev)]
    n_chunks = tile // chunk
    o_hg_chunks = [[None] * n_chunks for _ in range(heads)]

    def hg_levels(c):
        rs = slice(c * chunk, (c + 1) * chunk)
        f, gk = _forget_gate(p(5, rs=rs), lbv)
        gq = _silu(p(4, rs=rs))
        q_lv, k_lv, q_top, k_top, f_tot = _decay_levels(gq, gk, f)
        return rs, gq, gk, q_lv, k_lv, q_top.astype(_BF16), k_top.astype(_BF16), f_tot

    def hg_issue(c, hh, lv):
        rs, gq, gk, q_lv, k_lv, q_top, k_top, f_tot = lv
        sl = slice(hh * dh, (hh + 1) * dh)
        gv = p(6, hh, rs).astype(_BF16)
        a = jnp.where(diag, jnp.sum(gq[:, sl] * gk[:, sl], axis=-1, keepdims=True), 0.0)
        for lev in range(n_lev):
            a = jnp.where(lev_mask[lev], _dot_nt(q_lv[lev][:, sl], k_lv[lev][:, sl]), a)
        s_old = shg_ref[0, hh]
        o_inter = _dot(q_top[:, sl], s_old.astype(_BF16))
        f_col = jnp.broadcast_to(f_tot[:, sl], (dh, dh)).T
        shg_ref[0, hh] = s_old * f_col + _dot_tn(k_top[:, sl], gv)
        return c, hh, a.astype(_BF16), o_inter, gv

    def hg_finish(c, hh, a, o_inter, gv):
        o_hg_chunks[hh][c] = o_inter + _dot(a, gv)

    fill(1)
    pend = None
    for hh in range(heads):
        ctx = ret_issue(hh)
        fill(1)
        if pend is not None:
            ret_finish(*pend)
        pend = ctx
    lv = hg_levels(0)
    ret_finish(*pend)
    pend = None
    for c in range(n_chunks):
        fill(3)
        for hh in range(heads):
            ctx = hg_issue(c, hh, lv)
            if pend is not None:
                hg_finish(*pend)
            pend = ctx
            if hh == 1:
                fill(1)
        if c + 1 < n_chunks:
            lv = hg_levels(c + 1)
    hg_finish(*pend)
    fill(1)
    for hh in range(heads):
        sl = slice(hh * dh, (hh + 1) * dh)
        o = jnp.concatenate(o_hg_chunks[hh], axis=0)
        o_all[heads + hh] = (_rms(o, hgg_ref[:, sl]) * _silu(p(7, hh))).astype(_BF16)
    fill(None)
    o_cat = jnp.concatenate(o_all, axis=-1)
    y = x_ref[0, rows, :] + _dot(o_cat, wout_ref[...])
    y_ref[0, rows, :] = _rms(y, fg_ref[...])


def _prompt_kernel(xp_ref, xn_ref, cos_ref, sin_ref, win_ref, wout_ref, ng_ref, rg_ref, hgg_ref, lb_ref, fg_ref,
                   y_ref, sret_ref, shg_ref, pa_ref, pb_ref, rq_ref, rk_ref, rdec_ref,
                   *, heads, dh, tile, chunk, pairs):
    step = pl.program_id(0)
    pair = step % pairs
    gw = heads * dh
    rows_a, rows_b = slice(0, tile), slice(tile, 2 * tile)

    @pl.when(pair == 0)
    def _():
        sret_ref[...] = jnp.zeros_like(sret_ref)
        shg_ref[...] = jnp.zeros_like(shg_ref)

    @pl.when(step == 0)
    def _():
        _retention_tables(rq_ref, rk_ref, rdec_ref, heads, tile)
        for _ in _proj_stages(xp_ref, rows_a, ng_ref, win_ref, pa_ref, gw):
            pass

    mixer = functools.partial(
        _mixer, cos_ref=cos_ref, sin_ref=sin_ref, wout_ref=wout_ref, rg_ref=rg_ref, hgg_ref=hgg_ref,
        lb_ref=lb_ref, fg_ref=fg_ref, y_ref=y_ref, sret_ref=sret_ref, shg_ref=shg_ref,
        rq_ref=rq_ref, rk_ref=rk_ref, rdec_ref=rdec_ref, heads=heads, dh=dh, tile=tile, chunk=chunk)
    mixer(xp_ref, rows_a, pa_ref, _filler(_proj_stages(xp_ref, rows_b, ng_ref, win_ref, pb_ref, gw)))
    mixer(xp_ref, rows_b, pb_ref, _filler(_proj_stages(xn_ref, rows_a, ng_ref, win_ref, pa_ref, gw)))


def _prompt_call(x, cos, sin, win, wout, ng, rg, hgg, lb, fg, heads, dh):
    bsz, seq, d = x.shape
    tile = min(PROMPT_TILE, seq // 2)
    chunk = min(HGRN_CHUNK, tile)
    assert seq % (2 * tile) == 0 and tile % chunk == 0 and chunk & (chunk - 1) == 0 and chunk >= 16
    gw = heads * dh
    pairs = seq // (2 * tile)
    steps = bsz * pairs

    def nxt(s):
        s1 = jnp.minimum(s + 1, steps - 1)
        return (s1 // pairs, 2 * (s1 % pairs), 0)

    const = lambda *shape: pl.BlockSpec(shape, lambda s: (0,) * len(shape))
    state_spec = pl.BlockSpec((1, heads, dh, dh), lambda s: (s // pairs, 0, 0, 0))
    pair_spec = pl.BlockSpec((1, 2 * tile, d), lambda s: (s // pairs, s % pairs, 0))
    rope_spec = pl.BlockSpec((2 * tile, dh), lambda s: (s % pairs, 0))
    body = functools.partial(_prompt_kernel, heads=heads, dh=dh, tile=tile, chunk=chunk, pairs=pairs)
    return pl.pallas_call(
        body,
        grid=(steps,),
        in_specs=[
            pair_spec, pl.BlockSpec((1, tile, d), nxt), rope_spec, rope_spec,
            const(d, 8 * gw), const(2 * gw, d), const(1, d), const(1, gw), const(1, gw),
            const(lb.shape[0], gw), const(1, d),
        ],
        out_specs=[pair_spec, state_spec, state_spec],
        out_shape=[
            jax.ShapeDtypeStruct((bsz, seq, d), _F32),
            jax.ShapeDtypeStruct((bsz, heads, dh, dh), _F32),
            jax.ShapeDtypeStruct((bsz, heads, dh, dh), _F32),
        ],
        scratch_shapes=[pltpu.VMEM((tile, 8 * gw), _F32), pltpu.VMEM((tile, 8 * gw), _F32),
                        pltpu.VMEM((heads, tile, dh), _F32), pltpu.VMEM((heads, tile, dh), _F32),
                        pltpu.VMEM((heads, tile, tile), _F32)],
        compiler_params=pltpu.CompilerParams(
            dimension_semantics=("arbitrary",), vmem_limit_bytes=VMEM_LIMIT_BYTES),
        name="prompt_mixer",
    )(x, x, cos, sin, win, wout, ng, rg, hgg, lb, fg)


def _sample_kernel(x_ref, cos_ref, sin_ref, win_ref, wout_ref, ng_ref, rg_ref, hgg_ref, lb_ref, fg_ref,
                   sret_in, shg_in, y_ref, sret_out, shg_out,
                   rv_s, rgate_s, hi_s, hgate_s, rqT_s, rkT_s, hqT_s, hfT_s, hkT_s, *, heads, dh, block):
    i = pl.program_id(0)
    gw = heads * dh
    nb = x_ref.shape[0]

    @pl.when(i == 0)
    def _():
        x = x_ref[...]
        h = _rms(x, ng_ref[...]).astype(_BF16)

        def proj(sec):
            return _dot(h, win_ref[:, sec * gw:(sec + 1) * gw])

        cos, sin = cos_ref[...], sin_ref[...]

        def rot(p):
            return jnp.concatenate(
                [_rotate(p[:, hh * dh:(hh + 1) * dh], cos, sin, dh // 2) for hh in range(heads)], axis=-1)

        rqT_s[...] = rot(proj(0)).T
        rkT_s[...] = (rot(proj(1)) * (dh ** -0.5)).T
        rv_s[...] = proj(2)
        rgate_s[...] = _silu(proj(3))
        f, kin = _forget_gate(proj(5), _lower_bound(lb_ref))
        hqT_s[...] = _silu(proj(4)).T
        hfT_s[...] = f.T
        hkT_s[...] = kin.T
        hi_s[...] = proj(6)
        hgate_s[...] = _silu(proj(7))

    rows = pl.ds(pl.multiple_of(i * block, block), block)
    shift = (nb - i * block) % nb
    rqT = pltpu.roll(rqT_s[...], shift, 1)
    rkT = pltpu.roll(rkT_s[...], shift, 1)
    hqT = pltpu.roll(hqT_s[...], shift, 1)
    hfT = pltpu.roll(hfT_s[...], shift, 1)
    hkT = pltpu.roll(hkT_s[...], shift, 1)
    rv, hi = rv_s[rows, :], hi_s[rows, :]

    def col(a, hh, bi):
        return jnp.broadcast_to(a[hh * dh:(hh + 1) * dh, bi:bi + 1], (dh, dh))

    o_ret_rows, o_hg_rows = [], []
    for bi in range(block):
        o_r, o_h = [], []
        for hh in range(heads):
            sl = slice(hh * dh, (hh + 1) * dh)
            gamma = 1.0 - 2.0 ** (-5 - hh)
            s_new = gamma * sret_in[bi, hh] + col(rkT, hh, bi) * rv[bi:bi + 1, sl]
            sret_out[bi, hh] = s_new
            o_r.append(jnp.sum(s_new * col(rqT, hh, bi), axis=0, keepdims=True))
            g_new = col(hfT, hh, bi) * shg_in[bi, hh] + col(hkT, hh, bi) * hi[bi:bi + 1, sl]
            shg_out[bi, hh] = g_new
            o_h.append(jnp.sum(g_new * col(hqT, hh, bi), axis=0, keepdims=True))
        o_ret_rows.append(o_r)
        o_hg_rows.append(o_h)

    def finish(o_rows, gain_ref, gate):
        outs = []
        for hh in range(heads):
            sl = slice(hh * dh, (hh + 1) * dh)
            o = jnp.concatenate([o_rows[bi][hh] for bi in range(block)], axis=0)
            outs.append(_rms(o, gain_ref[:, sl]) * gate[:, sl])
        return outs

    o_cat = jnp.concatenate(
        finish(o_ret_rows, rg_ref, rgate_s[rows, :]) + finish(o_hg_rows, hgg_ref, hgate_s[rows, :]),
        axis=-1).astype(_BF16)
    y = x_ref[rows, :] + _dot(o_cat, wout_ref[...])
    y_ref[...] = _rms(y, fg_ref[...])


def _sample_call(x, cos, sin, win, wout, ng, rg, hgg, lb, fg, sret, shg):
    nb, d = x.shape
    _, heads, dh, _ = sret.shape
    gw = heads * dh
    block = min(SAMPLE_BLOCK, nb)
    assert nb % block == 0 and nb % 128 == 0
    const = lambda *shape: pl.BlockSpec(shape, lambda i: (0,) * len(shape))
    state_spec = pl.BlockSpec((block, heads, dh, dh), lambda i: (i, 0, 0, 0))
    body = functools.partial(_sample_kernel, heads=heads, dh=dh, block=block)
    row_scratch = pltpu.VMEM((nb, gw), _F32)
    col_scratch = pltpu.VMEM((gw, nb), _F32)
    return pl.pallas_call(
        body,
        grid=(nb // block,),
        in_specs=[
            const(nb, d), const(1, dh), const(1, dh),
            const(d, 8 * gw), const(2 * gw, d), const(1, d), const(1, gw), const(1, gw),
            const(lb.shape[0], gw), const(1, d),
            state_spec, state_spec,
        ],
        out_specs=[pl.BlockSpec((block, d), lambda i: (i, 0)), state_spec, state_spec],
        out_shape=[
            jax.ShapeDtypeStruct((nb, d), _F32),
            jax.ShapeDtypeStruct(sret.shape, _F32),
            jax.ShapeDtypeStruct(shg.shape, _F32),
        ],
        scratch_shapes=[row_scratch] * 4 + [col_scratch] * 5,
        compiler_params=pltpu.CompilerParams(
            dimension_semantics=("arbitrary",), vmem_limit_bytes=VMEM_LIMIT_BYTES),
        name="sample_mixer",
    )(x, cos, sin, win, wout, ng, rg, hgg, lb, fg, sret, shg)


def _rope_tables(pos, dh):
    half = dh // 2
    freqs = 1.0 / (ROPE_BASE ** (jnp.arange(half, dtype=_F32) / half))
    ang = pos[:, None] * freqs[None, :]
    c, s = jnp.cos(ang), jnp.sin(ang)
    return jnp.concatenate([c, c], axis=-1), jnp.concatenate([-s, s], axis=-1)


def kernel(x_prompt, x_sample, state_ret, state_hgrn, norm_g, w_in, ret_norm_g, hg_norm_g, hg_lb, w_out,
           final_norm_g):
    depth, _, heads, dh, _ = state_ret.shape
    assert depth == 1 and x_sample.shape[1] == 1
    bp, lp, d = x_prompt.shape
    win = w_in[0].astype(_BF16)
    wout = w_out[0].astype(_BF16)
    ng, fg = norm_g[0][None, :], final_norm_g[None, :]
    rg, hgg = ret_norm_g[0][None, :], hg_norm_g[0][None, :]

    cos_p, sin_p = _rope_tables(jnp.arange(lp, dtype=_F32), dh)
    y_p, ret_p, hg_p = _prompt_call(x_prompt, cos_p, sin_p, win, wout, ng, rg, hgg, hg_lb, fg, heads, dh)

    cos_s, sin_s = _rope_tables(PAST_LEN + jnp.arange(1, dtype=_F32), dh)
    y_s, ret_s, hg_s = _sample_call(x_sample[:, 0, :], cos_s, sin_s, win, wout, ng, rg, hgg, hg_lb, fg,
                                    state_ret[0], state_hgrn[0])
    return (y_p, y_s[:, None, :], ret_p[None], hg_p[None], ret_s[None], hg_s[None])
```

```python
import functools
import math

import jax
import jax.numpy as jnp
from jax import lax
from jax.experimental import pallas as pl
from jax.experimental.pallas import tpu as pltpu

PAST_LEN = 16384
ROPE_BASE = 10000.0
NORM_EPS = 1e-6

PROMPT_TILE = 256
HGRN_CHUNK = 128
SAMPLE_BLOCK = 8
VMEM_LIMIT_BYTES = 56 * 1024 * 1024

_F32 = jnp.float32
_BF16 = jnp.bfloat16


def _rms(x, g):
    return x * lax.rsqrt(jnp.mean(x * x, axis=-1, keepdims=True) + NORM_EPS) * g


def _sigmoid(x):
    return 1.0 / (1.0 + jnp.exp(-x))


def _silu(x):
    return x * _sigmoid(x)


def _dot(a, b):
    return jnp.dot(a, b, preferred_element_type=_F32)


def _dot_nt(a, b):
    return lax.dot_general(a, b, (((1,), (1,)), ((), ())), preferred_element_type=_F32)


def _dot_tn(a, b):
    return lax.dot_general(a, b, (((0,), (0,)), ((), ())), preferred_element_type=_F32)


def _rotate(x, cos, sin_signed, half):
    return x * cos + pltpu.roll(x, half, 1) * sin_signed


def _lower_bound(lb_ref):
    a = lb_ref[...]
    m = jnp.max(a, axis=0, keepdims=True)
    e = jnp.exp(a - m)
    return e[0:1, :] / jnp.sum(e, axis=0, keepdims=True)


def _forget_gate(hf, lbv):
    z = jnp.exp(-jnp.abs(hf))
    r = 1.0 / (1.0 + z)
    zr = z * r
    pos = hf >= 0.0
    sig_p = jnp.where(pos, r, zr)
    sig_n = jnp.where(pos, zr, r)
    one_m = 1.0 - lbv
    return lbv + one_m * sig_p, one_m * sig_n


def _decay_levels(qa, kin, f):
    c, w = f.shape
    n_lev = c.bit_length() - 1
    rowi = lax.broadcasted_iota(jnp.int32, (c, w), 0)
    tot, q, k = f, qa * f, kin
    q_lv, k_lv = [], []
    for lev in range(3):
        half = 1 << lev
        up = (rowi & half) != 0
        q_lv.append(jnp.where(up, q, 0.0).astype(_BF16))
        k_lv.append(jnp.where(up, 0.0, k).astype(_BF16))
        sib = jnp.where(up, pltpu.roll(tot, half, 0), pltpu.roll(tot, c - half, 0))
        q = jnp.where(up, q * sib, q)
        k = jnp.where(up, k, k * sib)
        tot = tot * sib
    q_b = [q[i * 8:(i + 1) * 8] for i in range(c // 8)]
    k_b = [k[i * 8:(i + 1) * 8] for i in range(c // 8)]
    tot_b = [tot[i * 8:i * 8 + 1] for i in range(c // 8)]
    for lev in range(3, n_lev):
        half = 1 << lev
        zero = jnp.zeros((half, w), _F32)
        q_lv.append(jnp.concatenate([q_b[i] if i % 2 else zero for i in range(len(q_b))], axis=0).astype(_BF16))
        k_lv.append(jnp.concatenate([zero if i % 2 else k_b[i] for i in range(len(k_b))], axis=0).astype(_BF16))
        q_b = [jnp.concatenate([q_b[i], q_b[i + 1] * tot_b[i]], axis=0) for i in range(0, len(q_b), 2)]
        k_b = [jnp.concatenate([k_b[i] * tot_b[i + 1], k_b[i + 1]], axis=0) for i in range(0, len(k_b), 2)]
        tot_b = [tot_b[i] * tot_b[i + 1] for i in range(0, len(tot_b), 2)]
    return q_lv, k_lv, q_b[0], k_b[0], tot_b[0]


def _proj_stages(x_ref, rows, ng_ref, win_ref, p_ref, gw):
    h = _rms(x_ref[0, rows, :], ng_ref[...]).astype(_BF16)
    yield
    width = gw // 2
    for sec in range(8 * gw // width):
        cols = slice(sec * width, (sec + 1) * width)
        p_ref[:, cols] = _dot(h, win_ref[:, cols])
        yield


def _filler(stages):
    def fill(n):
        if n is None:
            for _ in stages:
                pass
        else:
            for _ in range(n):
                next(stages, None)
    return fill


def _log_gamma(hh):
    return math.log1p(-(2.0 ** (-5 - hh)))


def _retention_tables(rq_ref, rk_ref, rdec_ref, heads, tile):
    row = lax.broadcasted_iota(jnp.int32, rq_ref.shape[1:], 0).astype(_F32)
    ti = lax.broadcasted_iota(jnp.int32, (tile, tile), 0)
    si = lax.broadcasted_iota(jnp.int32, (tile, tile), 1)
    lag = jnp.maximum(ti - si, 0).astype(_F32)
    for hh in range(heads):
        lg = _log_gamma(hh)
        rq_ref[hh] = jnp.exp(lg * (row + 1.0))
        rk_ref[hh] = jnp.exp(lg * (tile - 1.0 - row))
        rdec_ref[hh] = jnp.where(ti >= si, jnp.exp(lg * lag), 0.0)


def _mixer(x_ref, rows, p_ref, fill, *, cos_ref, sin_ref, wout_ref, rg_ref, hgg_ref, lb_ref, fg_ref,
           y_ref, sret_ref, shg_ref, rq_ref, rk_ref, rdec_ref, heads, dh, tile, chunk):
    gw = heads * dh

    def p(sec, hh=None, rs=slice(None)):
        if hh is None:
            return p_ref[rs, sec * gw:(sec + 1) * gw]
        return p_ref[rs, sec * gw + hh * dh:sec * gw + (hh + 1) * dh]

    o_all = [None] * (2 * heads)

    cos, sin = cos_ref[rows, :], sin_ref[rows, :]

    def ret_issue(hh):
        q = _rotate(p(0, hh), cos, sin, dh // 2).astype(_BF16)
        kt = (_rotate(p(1, hh), cos, sin, dh // 2) * (dh ** -0.5)).astype(_BF16).T
        v = p(2, hh)
        s_old = sret_ref[0, hh]
        qk_qs = _dot(q, jnp.concatenate([kt, s_old.astype(_BF16)], axis=1))
        sret_ref[0, hh] = math.exp(tile * _log_gamma(hh)) * s_old + _dot(kt, (v * rk_ref[hh]).astype(_BF16))
        return hh, qk_qs, v

    def ret_finish(hh, qk_qs, v):
        sl = slice(hh * dh, (hh + 1) * dh)
        a = qk_qs[:, :tile] * rdec_ref[hh]
        o = qk_qs[:, tile:] * rq_ref[hh] + _dot(a.astype(_BF16), v.astype(_BF16))
        o_all[hh] = (_rms(o, rg_ref[:, sl]) * _silu(p(3, hh))).astype(_BF16)

    lbv = _lower_bound(lb_ref)
    ci = lax.broadcasted_iota(jnp.int32, (chunk, chunk), 0)
    cj = lax.broadcasted_iota(jnp.int32, (chunk, chunk), 1)
    cxor = ci ^ cj
    n_lev = chunk.bit_length() - 1
    diag = ci == cj
    lev_mask = [(ci > cj) & ((cxor >> lev) == 1) for lev in range(n_lev)]
    n_chunks = tile // chunk
    o_hg_chunks = [[None] * n_chunks for _ in range(heads)]

    def hg_levels(c):
        rs = slice(c * chunk, (c + 1) * chunk)
        f, gk = _forget_gate(p(5, rs=rs), lbv)
        gq = _silu(p(4, rs=rs))
        q_lv, k_lv, q_top, k_top, f_tot = _decay_levels(gq, gk, f)
        return rs, gq, gk, q_lv, k_lv, q_top.astype(_BF16), k_top.astype(_BF16), f_tot

    def hg_issue(c, hh, lv):
        rs, gq, gk, q_lv, k_lv, q_top, k_top, f_tot = lv
        sl = slice(hh * dh, (hh + 1) * dh)
        gv = p(6, hh, rs).astype(_BF16)
        a = jnp.where(diag, jnp.sum(gq[:, sl] * gk[:, sl], axis=-1, keepdims=True), 0.0)
        for lev in range(n_lev):
            a = jnp.where(lev_mask[lev], _dot_nt(q_lv[lev][:, sl], k_lv[lev][:, sl]), a)
        s_old = shg_ref[0, hh]
        o_inter = _dot(q_top[:, sl], s_old.astype(_BF16))
        f_col = jnp.broadcast_to(f_tot[:, sl], (dh, dh)).T
        shg_ref[0, hh] = s_old * f_col + _dot_tn(k_top[:, sl], gv)
        return c, hh, a.astype(_BF16), o_inter, gv

    def hg_finish(c, hh, a, o_inter, gv):
        o_hg_chunks[hh][c] = o_inter + _dot(a, gv)

    fill(1)
    pend = None
    for hh in range(heads):
        ctx = ret_issue(hh)
        fill(1)
        if pend is not None:
            ret_finish(*pend)
        pend = ctx
    lv = hg_levels(0)
    ret_finish(*pend)
    pend = None
    for c in range(n_chunks):
        fill(3)
        for hh in range(heads):
            ctx = hg_issue(c, hh, lv)
            if pend is not None:
                hg_finish(*pend)
            pend = ctx
            if hh == 1:
                fill(1)
        if c + 1 < n_chunks:
            lv = hg_levels(c + 1)
    hg_finish(*pend)
    fill(1)
    for hh in range(heads):
        sl = slice(hh * dh, (hh + 1) * dh)
        o = jnp.concatenate(o_hg_chunks[hh], axis=0)
        o_all[heads + hh] = (_rms(o, hgg_ref[:, sl]) * _silu(p(7, hh))).astype(_BF16)
    fill(None)
    o_cat = jnp.concatenate(o_all, axis=-1)
    y = x_ref[0, rows, :] + _dot(o_cat, wout_ref[...])
    y_ref[0, rows, :] = _rms(y, fg_ref[...])


def _prompt_kernel(xp_ref, xn_ref, cos_ref, sin_ref, win_ref, wout_ref, ng_ref, rg_ref, hgg_ref, lb_ref, fg_ref,
                   y_ref, sret_ref, shg_ref, pa_ref, pb_ref, rq_ref, rk_ref, rdec_ref,
                   *, heads, dh, tile, chunk, pairs):
    step = pl.program_id(0)
    pair = step % pairs
    gw = heads * dh
    rows_a, rows_b = slice(0, tile), slice(tile, 2 * tile)

    @pl.when(pair == 0)
    def _():
        sret_ref[...] = jnp.zeros_like(sret_ref)
        shg_ref[...] = jnp.zeros_like(shg_ref)

    @pl.when(step == 0)
    def _():
        _retention_tables(rq_ref, rk_ref, rdec_ref, heads, tile)
        for _ in _proj_stages(xp_ref, rows_a, ng_ref, win_ref, pa_ref, gw):
            pass

    mixer = functools.partial(
        _mixer, cos_ref=cos_ref, sin_ref=sin_ref, wout_ref=wout_ref, rg_ref=rg_ref, hgg_ref=hgg_ref,
        lb_ref=lb_ref, fg_ref=fg_ref, y_ref=y_ref, sret_ref=sret_ref, shg_ref=shg_ref,
        rq_ref=rq_ref, rk_ref=rk_ref, rdec_ref=rdec_ref, heads=heads, dh=dh, tile=tile, chunk=chunk)
    mixer(xp_ref, rows_a, pa_ref, _filler(_proj_stages(xp_ref, rows_b, ng_ref, win_ref, pb_ref, gw)))
    mixer(xp_ref, rows_b, pb_ref, _filler(_proj_stages(xn_ref, rows_a, ng_ref, win_ref, pa_ref, gw)))


def _prompt_call(x, cos, sin, win, wout, ng, rg, hgg, lb, fg, heads, dh):
    bsz, seq, d = x.shape
    tile = min(PROMPT_TILE, seq // 2)
    chunk = min(HGRN_CHUNK, tile)
    assert seq % (2 * tile) == 0 and tile % chunk == 0 and chunk & (chunk - 1) == 0 and chunk >= 16
    gw = heads * dh
    pairs = seq // (2 * tile)
    steps = bsz * pairs

    def nxt(s):
        s1 = jnp.minimum(s + 1, steps - 1)
        return (s1 // pairs, 2 * (s1 % pairs), 0)

    const = lambda *shape: pl.BlockSpec(shape, lambda s: (0,) * len(shape))
    state_spec = pl.BlockSpec((1, heads, dh, dh), lambda s: (s // pairs, 0, 0, 0))
    pair_spec = pl.BlockSpec((1, 2 * tile, d), lambda s: (s // pairs, s % pairs, 0))
    rope_spec = pl.BlockSpec((2 * tile, dh), lambda s: (s % pairs, 0))
    body = functools.partial(_prompt_kernel, heads=heads, dh=dh, tile=tile, chunk=chunk, pairs=pairs)
    return pl.pallas_call(
        body,
        grid=(steps,),
        in_specs=[
            pair_spec, pl.BlockSpec((1, tile, d), nxt), rope_spec, rope_spec,
            const(d, 8 * gw), const(2 * gw, d), const(1, d), const(1, gw), const(1, gw),
            const(lb.shape[0], gw), const(1, d),
        ],
        out_specs=[pair_spec, state_spec, state_spec],
        out_shape=[
            jax.ShapeDtypeStruct((bsz, seq, d), _F32),
            jax.ShapeDtypeStruct((bsz, heads, dh, dh), _F32),
            jax.ShapeDtypeStruct((bsz, heads, dh, dh), _F32),
        ],
        scratch_shapes=[pltpu.VMEM((tile, 8 * gw), _F32), pltpu.VMEM((tile, 8 * gw), _F32),
                        pltpu.VMEM((heads, tile, dh), _F32), pltpu.VMEM((heads, tile, dh), _F32),
                        pltpu.VMEM((heads, tile, tile), _F32)],
        compiler_params=pltpu.CompilerParams(
            dimension_semantics=("arbitrary",), vmem_limit_bytes=VMEM_LIMIT_BYTES),
        name="prompt_mixer",
    )(x, x, cos, sin, win, wout, ng, rg, hgg, lb, fg)


def _sample_kernel(x_ref, cos_ref, sin_ref, win_ref, wout_ref, ng_ref, rg_ref, hgg_ref, lb_ref, fg_ref,
                   sret_in, shg_in, y_ref, sret_out, shg_out,
                   rq_s, rk_s, rv_s, rgate_s, hq_s, hk_s, hi_s, hgate_s, hft_s, *, heads, dh, block):
    i = pl.program_id(0)
    gw = heads * dh
    nb = x_ref.shape[0]

    @pl.when(i == 0)
    def _():
        h = _rms(x_ref[...], ng_ref[...]).astype(_BF16)

        def proj(sec):
            return _dot(h, win_ref[:, sec * gw:(sec + 1) * gw])

        cos, sin = cos_ref[...], sin_ref[...]

        def rot(pj):
            return jnp.concatenate(
                [_rotate(pj[:, hh * dh:(hh + 1) * dh], cos, sin, dh // 2) for hh in range(heads)], axis=-1)

        rq_s[...] = rot(proj(0))
        rk_s[...] = rot(proj(1)) * (dh ** -0.5)
        rv_s[...] = proj(2)
        rgate_s[...] = _silu(proj(3))
        f, kin = _forget_gate(proj(5), _lower_bound(lb_ref))
        hq_s[...] = _silu(proj(4))
        hk_s[...] = kin
        hft_s[...] = f.T
        hi_s[...] = proj(6)
        hgate_s[...] = _silu(proj(7))

    rows = pl.ds(pl.multiple_of(i * block, block), block)
    shift = (nb - i * block) % nb
    hft = pltpu.roll(hft_s[...], shift, 1)
    rq, rk, rv = rq_s[rows, :].astype(_BF16), rk_s[rows, :].astype(_BF16), rv_s[rows, :].astype(_BF16)
    hq, hk, hi = hq_s[rows, :].astype(_BF16), hk_s[rows, :].astype(_BF16), hi_s[rows, :].astype(_BF16)
    rowi = lax.broadcasted_iota(jnp.int32, (block, gw), 0)
    zero = jnp.zeros((block, gw), _BF16)

    o_ret = [[None] * block for _ in range(heads)]
    o_hg = [[None] * block for _ in range(heads)]
    for bi in range(block):
        rk_bi = jnp.where(rowi == bi, rk, zero)
        hk_bi = jnp.where(rowi == bi, hk, zero)
        for hh in range(heads):
            sl = slice(hh * dh, (hh + 1) * dh)
            gamma = 1.0 - 2.0 ** (-5 - hh)
            s_new = gamma * sret_in[bi, hh] + _dot_tn(rk_bi[:, sl], rv[:, sl])
            sret_out[bi, hh] = s_new
            o_ret[hh][bi] = _dot(rq[:, sl], s_new.astype(_BF16))[bi:bi + 1, :]
            f_col = jnp.broadcast_to(hft[hh * dh:(hh + 1) * dh, bi:bi + 1], (dh, dh))
            g_new = f_col * shg_in[bi, hh] + _dot_tn(hk_bi[:, sl], hi[:, sl])
            shg_out[bi, hh] = g_new
            o_hg[hh][bi] = _dot(hq[:, sl], g_new.astype(_BF16))[bi:bi + 1, :]

    def finish(o_rows, gain_ref, gate):
        outs = []
        for hh in range(heads):
            sl = slice(hh * dh, (hh + 1) * dh)
            o = jnp.concatenate(o_rows[hh], axis=0)
            outs.append(_rms(o, gain_ref[:, sl]) * gate[:, sl])
        return outs

    o_cat = jnp.concatenate(
        finish(o_ret, rg_ref, rgate_s[rows, :]) + finish(o_hg, hgg_ref, hgate_s[rows, :]),
        axis=-1).astype(_BF16)
    y = x_ref[rows, :] + _dot(o_cat, wout_ref[...])
    y_ref[...] = _rms(y, fg_ref[...])


def _sample_call(x, cos, sin, win, wout, ng, rg, hgg, lb, fg, sret, shg):
    nb, d = x.shape
    _, heads, dh, _ = sret.shape
    gw = heads * dh
    block = min(SAMPLE_BLOCK, nb)
    assert nb % block == 0 and nb % 128 == 0
    const = lambda *shape: pl.BlockSpec(shape, lambda i: (0,) * len(shape))
    state_spec = pl.BlockSpec((block, heads, dh, dh), lambda i: (i, 0, 0, 0))
    body = functools.partial(_sample_kernel, heads=heads, dh=dh, block=block)
    row_scratch = pltpu.VMEM((nb, gw), _F32)
    return pl.pallas_call(
        body,
        grid=(nb // block,),
        in_specs=[
            const(nb, d), const(1, dh), const(1, dh),
            const(d, 8 * gw), const(2 * gw, d), const(1, d), const(1, gw), const(1, gw),
            const(lb.shape[0], gw), const(1, d),
            state_spec, state_spec,
        ],
        out_specs=[pl.BlockSpec((block, d), lambda i: (i, 0)), state_spec, state_spec],
        out_shape=[
            jax.ShapeDtypeStruct((nb, d), _F32),
            jax.ShapeDtypeStruct(sret.shape, _F32),
            jax.ShapeDtypeStruct(shg.shape, _F32),
        ],
        scratch_shapes=[row_scratch] * 8 + [pltpu.VMEM((gw, nb), _F32)],
        compiler_params=pltpu.CompilerParams(
            dimension_semantics=("arbitrary",), vmem_limit_bytes=VMEM_LIMIT_BYTES),
        name="sample_mixer",
    )(x, cos, sin, win, wout, ng, rg, hgg, lb, fg, sret, shg)


def _rope_tables(pos, dh):
    half = dh // 2
    freqs = 1.0 / (ROPE_BASE ** (jnp.arange(half, dtype=_F32) / half))
    ang = pos[:, None] * freqs[None, :]
    c, s = jnp.cos(ang), jnp.sin(ang)
    return jnp.concatenate([c, c], axis=-1), jnp.concatenate([-s, s], axis=-1)


def kernel(x_prompt, x_sample, state_ret, state_hgrn, norm_g, w_in, ret_norm_g, hg_norm_g, hg_lb, w_out,
           final_norm_g):
    depth, _, heads, dh, _ = state_ret.shape
    assert depth == 1 and x_sample.shape[1] == 1
    bp, lp, d = x_prompt.shape
    win = w_in[0].astype(_BF16)
    wout = w_out[0].astype(_BF16)
    ng, fg = norm_g[0][None, :], final_norm_g[None, :]
    rg, hgg = ret_norm_g[0][None, :], hg_norm_g[0][None, :]

    cos_p, sin_p = _rope_tables(jnp.arange(lp, dtype=_F32), dh)
    y_p, ret_p, hg_p = _prompt_call(x_prompt, cos_p, sin_p, win, wout, ng, rg, hgg, hg_lb, fg, heads, dh)

    cos_s, sin_s = _rope_tables(PAST_LEN + jnp.arange(1, dtype=_F32), dh)
    y_s, ret_s, hg_s = _sample_call(x_sample[:, 0, :], cos_s, sin_s, win, wout, ng, rg, hgg, hg_lb, fg,
                                    state_ret[0], state_hgrn[0])
    return (y_p, y_s[:, None, :], ret_p[None], hg_p[None], ret_s[None], hg_s[None])
```

```python
import functools
import math

import jax
import jax.numpy as jnp
from jax import lax
from jax.experimental import pallas as pl
from jax.experimental.pallas import tpu as pltpu

PAST_LEN = 16384
ROPE_BASE = 10000.0
NORM_EPS = 1e-6

PROMPT_TILE = 256
HGRN_CHUNK = 128
SAMPLE_BLOCK = 8
VMEM_LIMIT_BYTES = 56 * 1024 * 1024

_F32 = jnp.float32
_BF16 = jnp.bfloat16


def _rms(x, g):
    return x * lax.rsqrt(jnp.mean(x * x, axis=-1, keepdims=True) + NORM_EPS) * g


def _sigmoid(x):
    return 1.0 / (1.0 + jnp.exp(-x))


def _silu(x):
    return x * _sigmoid(x)


def _dot(a, b):
    return jnp.dot(a, b, preferred_element_type=_F32)


def _dot_nt(a, b):
    return lax.dot_general(a, b, (((1,), (1,)), ((), ())), preferred_element_type=_F32)


def _dot_tn(a, b):
    return lax.dot_general(a, b, (((0,), (0,)), ((), ())), preferred_element_type=_F32)


def _pack_rows(w):
    k, n = w.shape
    return lax.bitcast_convert_type(w.reshape(k // 2, 2, n).transpose(0, 2, 1), jnp.int32)


def _weight(w_ref, cols=slice(None)):
    return pltpu.bitcast(w_ref[:, cols], _BF16)


def _rotate(x, cos, sin_signed, half):
    return x * cos + pltpu.roll(x, half, 1) * sin_signed


def _lower_bound(lb_ref):
    a = lb_ref[...]
    m = jnp.max(a, axis=0, keepdims=True)
    e = jnp.exp(a - m)
    return e[0:1, :] / jnp.sum(e, axis=0, keepdims=True)


def _forget_gate(hf, lbv):
    z = jnp.exp(-jnp.abs(hf))
    r = 1.0 / (1.0 + z)
    zr = z * r
    pos = hf >= 0.0
    sig_p = jnp.where(pos, r, zr)
    sig_n = jnp.where(pos, zr, r)
    one_m = 1.0 - lbv
    return lbv + one_m * sig_p, one_m * sig_n


def _decay_levels(qa, kin, f):
    c, w = f.shape
    n_lev = c.bit_length() - 1
    rowi = lax.broadcasted_iota(jnp.int32, (c, w), 0)
    tot, q, k = f, qa * f, kin
    for lev in range(3):
        half = 1 << lev
        up = (rowi & half) != 0
        yield jnp.where(up, q, 0.0).astype(_BF16), jnp.where(up, 0.0, k).astype(_BF16)
        sib = jnp.where(up, pltpu.roll(tot, half, 0), pltpu.roll(tot, c - half, 0))
        q = jnp.where(up, q * sib, q)
        k = jnp.where(up, k, k * sib)
        tot = tot * sib
    q_b = [q[i * 8:(i + 1) * 8] for i in range(c // 8)]
    k_b = [k[i * 8:(i + 1) * 8] for i in range(c // 8)]
    tot_b = [tot[i * 8:i * 8 + 1] for i in range(c // 8)]
    for lev in range(3, n_lev):
        half = 1 << lev
        zero = jnp.zeros((half, w), _F32)
        yield (jnp.concatenate([q_b[i] if i % 2 else zero for i in range(len(q_b))], axis=0).astype(_BF16),
               jnp.concatenate([zero if i % 2 else k_b[i] for i in range(len(k_b))], axis=0).astype(_BF16))
        q_b = [jnp.concatenate([q_b[i], q_b[i + 1] * tot_b[i]], axis=0) for i in range(0, len(q_b), 2)]
        k_b = [jnp.concatenate([k_b[i] * tot_b[i + 1], k_b[i + 1]], axis=0) for i in range(0, len(k_b), 2)]
        tot_b = [tot_b[i] * tot_b[i + 1] for i in range(0, len(tot_b), 2)]
    yield q_b[0].astype(_BF16), k_b[0].astype(_BF16), tot_b[0]


def _proj_stages(x_ref, rows, ng_ref, win_ref, p_ref, gw):
    h = _rms(x_ref[0, rows, :], ng_ref[...]).astype(_BF16)
    yield
    width = gw // 2
    for sec in range(8 * gw // width):
        cols = slice(sec * width, (sec + 1) * width)
        p_ref[:, cols] = _dot(h, _weight(win_ref, cols))
        yield


def _filler(stages):
    def fill(n):
        if n is None:
            for _ in stages:
                pass
        else:
            for _ in range(n):
                next(stages, None)
    return fill


def _log_gamma(hh):
    return math.log1p(-(2.0 ** (-5 - hh)))


def _retention_tables(rq_ref, rk_ref, rdec_ref, heads, tile):
    row = lax.broadcasted_iota(jnp.int32, rq_ref.shape[1:], 0).astype(_F32)
    ti = lax.broadcasted_iota(jnp.int32, (tile, tile), 0)
    si = lax.broadcasted_iota(jnp.int32, (tile, tile), 1)
    lag = jnp.maximum(ti - si, 0).astype(_F32)
    for hh in range(heads):
        lg = _log_gamma(hh)
        rq_ref[hh] = jnp.exp(lg * (row + 1.0))
        rk_ref[hh] = jnp.exp(lg * (tile - 1.0 - row))
        rdec_ref[hh] = jnp.where(ti >= si, jnp.exp(lg * lag), 0.0)


def _mixer(x_ref, rows, p_ref, fill, *, cos_ref, sin_ref, wout_ref, rg_ref, hgg_ref, lb_ref, fg_ref,
           y_ref, sret_ref, shg_ref, rq_ref, rk_ref, rdec_ref, heads, dh, tile, chunk):
    gw = heads * dh

    def p(sec, hh=None, rs=slice(None)):
        if hh is None:
            return p_ref[rs, sec * gw:(sec + 1) * gw]
        return p_ref[rs, sec * gw + hh * dh:sec * gw + (hh + 1) * dh]

    o_all = [None] * (2 * heads)

    cos, sin = cos_ref[rows, :], sin_ref[rows, :]

    def ret_issue(hh):
        q = _rotate(p(0, hh), cos, sin, dh // 2).astype(_BF16)
        kt = (_rotate(p(1, hh), cos, sin, dh // 2) * (dh ** -0.5)).astype(_BF16).T
        v = p(2, hh)
        s_old = sret_ref[0, hh]
        qk_qs = _dot(q, jnp.concatenate([kt, s_old.astype(_BF16)], axis=1))
        sret_ref[0, hh] = math.exp(tile * _log_gamma(hh)) * s_old + _dot(kt, (v * rk_ref[hh]).astype(_BF16))
        return hh, qk_qs, v

    def ret_finish(hh, qk_qs, v):
        sl = slice(hh * dh, (hh + 1) * dh)
        a = qk_qs[:, :tile] * rdec_ref[hh]
        o = qk_qs[:, tile:] * rq_ref[hh] + _dot(a.astype(_BF16), v.astype(_BF16))
        o_all[hh] = (_rms(o, rg_ref[:, sl]) * _silu(p(3, hh))).astype(_BF16)

    lbv = _lower_bound(lb_ref)
    ci = lax.broadcasted_iota(jnp.int32, (chunk, chunk), 0)
    cj = lax.broadcasted_iota(jnp.int32, (chunk, chunk), 1)
    cxor = ci ^ cj
    n_lev = chunk.bit_length() - 1
    diag = ci == cj
    lev_mask = [(ci > cj) & ((cxor >> lev) == 1) for lev in range(n_lev)]
    n_chunks = tile // chunk
    o_hg_chunks = [[None] * n_chunks for _ in range(heads)]

    def hg_issue(c, hh):
        rs = slice(c * chunk, (c + 1) * chunk)
        sl = slice(hh * dh, (hh + 1) * dh)
        f, gk = _forget_gate(p(5, hh, rs), lbv[:, sl])
        gq = _silu(p(4, hh, rs))
        gv = p(6, hh, rs).astype(_BF16)
        a = jnp.where(diag, jnp.sum(gq * gk, axis=-1, keepdims=True), 0.0)
        *levels, (q_top, k_top, f_tot) = _decay_levels(gq, gk, f)
        for lev, (q_lv, k_lv) in enumerate(levels):
            a = jnp.where(lev_mask[lev], _dot_nt(q_lv, k_lv), a)
        s_old = shg_ref[0, hh]
        o_inter = _dot(q_top, s_old.astype(_BF16))
        f_col = jnp.broadcast_to(f_tot, (dh, dh)).T
        shg_ref[0, hh] = s_old * f_col + _dot_tn(k_top, gv)
        return c, hh, a.astype(_BF16), o_inter, gv

    def hg_finish(c, hh, a, o_inter, gv):
        o_hg_chunks[hh][c] = o_inter + _dot(a, gv)

    fill(1)
    pend = None
    for hh in range(heads):
        ctx = ret_issue(hh)
        fill(1)
        if pend is not None:
            ret_finish(*pend)
        pend = ctx
    fill(1)
    ret_finish(*pend)
    pend = None
    for c in range(n_chunks):
        for hh in range(heads):
            ctx = hg_issue(c, hh)
            fill(1)
            if pend is not None:
                hg_finish(*pend)
            pend = ctx
    fill(2)
    hg_finish(*pend)
    fill(1)
    for hh in range(heads):
        sl = slice(hh * dh, (hh + 1) * dh)
        o = jnp.concatenate(o_hg_chunks[hh], axis=0)
        o_all[heads + hh] = (_rms(o, hgg_ref[:, sl]) * _silu(p(7, hh))).astype(_BF16)
    fill(None)
    o_cat = jnp.concatenate(o_all, axis=-1)
    y = x_ref[0, rows, :] + _dot(o_cat, _weight(wout_ref))
    y_ref[0, rows, :] = _rms(y, fg_ref[...])


def _prompt_kernel(xp_ref, xn_ref, cos_ref, sin_ref, win_ref, wout_ref, ng_ref, rg_ref, hgg_ref, lb_ref, fg_ref,
                   y_ref, sret_ref, shg_ref, pa_ref, pb_ref, rq_ref, rk_ref, rdec_ref,
                   *, heads, dh, tile, chunk, pairs):
    step = pl.program_id(0)
    pair = step % pairs
    gw = heads * dh
    rows_a, rows_b = slice(0, tile), slice(tile, 2 * tile)

    @pl.when(pair == 0)
    def _():
        sret_ref[...] = jnp.zeros_like(sret_ref)
        shg_ref[...] = jnp.zeros_like(shg_ref)

    @pl.when(step == 0)
    def _():
        _retention_tables(rq_ref, rk_ref, rdec_ref, heads, tile)
        for _ in _proj_stages(xp_ref, rows_a, ng_ref, win_ref, pa_ref, gw):
            pass

    mixer = functools.partial(
        _mixer, cos_ref=cos_ref, sin_ref=sin_ref, wout_ref=wout_ref, rg_ref=rg_ref, hgg_ref=hgg_ref,
        lb_ref=lb_ref, fg_ref=fg_ref, y_ref=y_ref, sret_ref=sret_ref, shg_ref=shg_ref,
        rq_ref=rq_ref, rk_ref=rk_ref, rdec_ref=rdec_ref, heads=heads, dh=dh, tile=tile, chunk=chunk)
    mixer(xp_ref, rows_a, pa_ref, _filler(_proj_stages(xp_ref, rows_b, ng_ref, win_ref, pb_ref, gw)))
    mixer(xp_ref, rows_b, pb_ref, _filler(_proj_stages(xn_ref, rows_a, ng_ref, win_ref, pa_ref, gw)))


def _prompt_call(x, cos, sin, win, wout, ng, rg, hgg, lb, fg, heads, dh):
    bsz, seq, d = x.shape
    tile = min(PROMPT_TILE, seq // 2)
    chunk = min(HGRN_CHUNK, tile)
    assert seq % (2 * tile) == 0 and tile % chunk == 0 and chunk & (chunk - 1) == 0 and chunk >= 16
    gw = heads * dh
    pairs = seq // (2 * tile)
    steps = bsz * pairs

    def nxt(s):
        s1 = jnp.minimum(s + 1, steps - 1)
        return (s1 // pairs, 2 * (s1 % pairs), 0)

    const = lambda *shape: pl.BlockSpec(shape, lambda s: (0,) * len(shape))
    state_spec = pl.BlockSpec((1, heads, dh, dh), lambda s: (s // pairs, 0, 0, 0))
    pair_spec = pl.BlockSpec((1, 2 * tile, d), lambda s: (s // pairs, s % pairs, 0))
    rope_spec = pl.BlockSpec((2 * tile, dh), lambda s: (s % pairs, 0))
    body = functools.partial(_prompt_kernel, heads=heads, dh=dh, tile=tile, chunk=chunk, pairs=pairs)
    return pl.pallas_call(
        body,
        grid=(steps,),
        in_specs=[
            pair_spec, pl.BlockSpec((1, tile, d), nxt), rope_spec, rope_spec,
            const(d // 2, 8 * gw), const(gw, d), const(1, d), const(1, gw), const(1, gw),
            const(lb.shape[0], gw), const(1, d),
        ],
        out_specs=[pair_spec, state_spec, state_spec],
        out_shape=[
            jax.ShapeDtypeStruct((bsz, seq, d), _F32),
            jax.ShapeDtypeStruct((bsz, heads, dh, dh), _F32),
            jax.ShapeDtypeStruct((bsz, heads, dh, dh), _F32),
        ],
        scratch_shapes=[pltpu.VMEM((tile, 8 * gw), _F32), pltpu.VMEM((tile, 8 * gw), _F32),
                        pltpu.VMEM((heads, tile, dh), _F32), pltpu.VMEM((heads, tile, dh), _F32),
                        pltpu.VMEM((heads, tile, tile), _F32)],
        compiler_params=pltpu.CompilerParams(
            dimension_semantics=("arbitrary",), vmem_limit_bytes=VMEM_LIMIT_BYTES),
        name="prompt_mixer",
    )(x, x, cos, sin, win, wout, ng, rg, hgg, lb, fg)


def _sample_kernel(x_ref, cos_ref, sin_ref, win_ref, wout_ref, ng_ref, rg_ref, hgg_ref, lb_ref, fg_ref,
                   sret_in, shg_in, y_ref, sret_out, shg_out,
                   rq_s, rk_s, rv_s, rgate_s, hq_s, hk_s, hi_s, hgate_s, hft_s, *, heads, dh, block):
    i = pl.program_id(0)
    gw = heads * dh
    nb = x_ref.shape[0]

    @pl.when(i == 0)
    def _():
        h = _rms(x_ref[...], ng_ref[...]).astype(_BF16)

        def proj(sec):
            return _dot(h, _weight(win_ref, slice(sec * gw, (sec + 1) * gw)))

        cos, sin = cos_ref[...], sin_ref[...]

        def rot(pj):
            return jnp.concatenate(
                [_rotate(pj[:, hh * dh:(hh + 1) * dh], cos, sin, dh // 2) for hh in range(heads)], axis=-1)

        rq_s[...] = rot(proj(0))
        rk_s[...] = rot(proj(1)) * (dh ** -0.5)
        rv_s[...] = proj(2)
        rgate_s[...] = _silu(proj(3))
        f, kin = _forget_gate(proj(5), _lower_bound(lb_ref))
        hq_s[...] = _silu(proj(4))
        hk_s[...] = kin
        hft_s[...] = f.T
        hi_s[...] = proj(6)
        hgate_s[...] = _silu(proj(7))

    rows = pl.ds(pl.multiple_of(i * block, block), block)
    shift = (nb - i * block) % nb
    hft = pltpu.roll(hft_s[...], shift, 1)
    rq, rk, rv = rq_s[rows, :].astype(_BF16), rk_s[rows, :].astype(_BF16), rv_s[rows, :].astype(_BF16)
    hq, hk, hi = hq_s[rows, :].astype(_BF16), hk_s[rows, :].astype(_BF16), hi_s[rows, :].astype(_BF16)
    rowi = lax.broadcasted_iota(jnp.int32, (block, gw), 0)
    zero = jnp.zeros((block, gw), _BF16)

    o_ret = [[None] * block for _ in range(heads)]
    o_hg = [[None] * block for _ in range(heads)]
    for bi in range(block):
        rk_bi = jnp.where(rowi == bi, rk, zero)
        hk_bi = jnp.where(rowi == bi, hk, zero)
        for hh in range(heads):
            sl = slice(hh * dh, (hh + 1) * dh)
            gamma = 1.0 - 2.0 ** (-5 - hh)
            s_new = gamma * sret_in[bi, hh] + _dot_tn(rk_bi[:, sl], rv[:, sl])
            sret_out[bi, hh] = s_new
            o_ret[hh][bi] = _dot(rq[:, sl], s_new.astype(_BF16))[bi:bi + 1, :]
            f_col = jnp.broadcast_to(hft[hh * dh:(hh + 1) * dh, bi:bi + 1], (dh, dh))
            g_new = f_col * shg_in[bi, hh] + _dot_tn(hk_bi[:, sl], hi[:, sl])
            shg_out[bi, hh] = g_new
            o_hg[hh][bi] = _dot(hq[:, sl], g_new.astype(_BF16))[bi:bi + 1, :]

    def finish(o_rows, gain_ref, gate):
        outs = []
        for hh in range(heads):
            sl = slice(hh * dh, (hh + 1) * dh)
            o = jnp.concatenate(o_rows[hh], axis=0)
            outs.append(_rms(o, gain_ref[:, sl]) * gate[:, sl])
        return outs

    o_cat = jnp.concatenate(
        finish(o_ret, rg_ref, rgate_s[rows, :]) + finish(o_hg, hgg_ref, hgate_s[rows, :]),
        axis=-1).astype(_BF16)
    y = x_ref[rows, :] + _dot(o_cat, _weight(wout_ref))
    y_ref[...] = _rms(y, fg_ref[...])


def _sample_call(x, cos, sin, win, wout, ng, rg, hgg, lb, fg, sret, shg):
    nb, d = x.shape
    _, heads, dh, _ = sret.shape
    gw = heads * dh
    block = min(SAMPLE_BLOCK, nb)
    assert nb % block == 0 and nb % 128 == 0
    const = lambda *shape: pl.BlockSpec(shape, lambda i: (0,) * len(shape))
    state_spec = pl.BlockSpec((block, heads, dh, dh), lambda i: (i, 0, 0, 0))
    body = functools.partial(_sample_kernel, heads=heads, dh=dh, block=block)
    row_scratch = pltpu.VMEM((nb, gw), _F32)
    return pl.pallas_call(
        body,
        grid=(nb // block,),
        in_specs=[
            const(nb, d), const(1, dh), const(1, dh),
            const(d // 2, 8 * gw), const(gw, d), const(1, d), const(1, gw), const(1, gw),
            const(lb.shape[0], gw), const(1, d),
            state_spec, state_spec,
        ],
        out_specs=[pl.BlockSpec((block, d), lambda i: (i, 0)), state_spec, state_spec],
        out_shape=[
            jax.ShapeDtypeStruct((nb, d), _F32),
            jax.ShapeDtypeStruct(sret.shape, _F32),
            jax.ShapeDtypeStruct(shg.shape, _F32),
        ],
        scratch_shapes=[row_scratch] * 8 + [pltpu.VMEM((gw, nb), _F32)],
        compiler_params=pltpu.CompilerParams(
            dimension_semantics=("arbitrary",), vmem_limit_bytes=VMEM_LIMIT_BYTES),
        name="sample_mixer",
    )(x, cos, sin, win, wout, ng, rg, hgg, lb, fg, sret, shg)


def _rope_tables(pos, dh):
    half = dh // 2
    freqs = 1.0 / (ROPE_BASE ** (jnp.arange(half, dtype=_F32) / half))
    ang = pos[:, None] * freqs[None, :]
    c, s = jnp.cos(ang), jnp.sin(ang)
    return jnp.concatenate([c, c], axis=-1), jnp.concatenate([-s, s], axis=-1)


def kernel(x_prompt, x_sample, state_ret, state_hgrn, norm_g, w_in, ret_norm_g, hg_norm_g, hg_lb, w_out,
           final_norm_g):
    depth, _, heads, dh, _ = state_ret.shape
    assert depth == 1 and x_sample.shape[1] == 1
    bp, lp, d = x_prompt.shape
    win = _pack_rows(w_in[0].astype(_BF16))
    wout = _pack_rows(w_out[0].astype(_BF16))
    ng, fg = norm_g[0][None, :], final_norm_g[None, :]
    rg, hgg = ret_norm_g[0][None, :], hg_norm_g[0][None, :]

    cos_p, sin_p = _rope_tables(jnp.arange(lp, dtype=_F32), dh)
    y_p, ret_p, hg_p = _prompt_call(x_prompt, cos_p, sin_p, win, wout, ng, rg, hgg, hg_lb, fg, heads, dh)

    cos_s, sin_s = _rope_tables(PAST_LEN + jnp.arange(1, dtype=_F32), dh)
    y_s, ret_s, hg_s = _sample_call(x_sample[:, 0, :], cos_s, sin_s, win, wout, ng, rg, hgg, hg_lb, fg,
                                    state_ret[0], state_hgrn[0])
    return (y_p, y_s[:, None, :], ret_p[None], hg_p[None], ret_s[None], hg_s[None])
```

```python
import functools
import math

import jax
import jax.numpy as jnp
from jax import lax
from jax.experimental import pallas as pl
from jax.experimental.pallas import tpu as pltpu

PAST_LEN = 16384
ROPE_BASE = 10000.0
NORM_EPS = 1e-6

PROMPT_TILE = 256
HGRN_CHUNK = 128
SAMPLE_BLOCK = 8
VMEM_LIMIT_BYTES = 56 * 1024 * 1024

_F32 = jnp.float32
_BF16 = jnp.bfloat16


def _rms(x, g):
    return x * lax.rsqrt(jnp.mean(x * x, axis=-1, keepdims=True) + NORM_EPS) * g


def _sigmoid(x):
    return 1.0 / (1.0 + jnp.exp(-x))


def _silu(x):
    return x * _sigmoid(x)


def _dot(a, b):
    return jnp.dot(a, b, preferred_element_type=_F32)


def _dot_nt(a, b):
    return lax.dot_general(a, b, (((1,), (1,)), ((), ())), preferred_element_type=_F32)


def _dot_tn(a, b):
    return lax.dot_general(a, b, (((0,), (0,)), ((), ())), preferred_element_type=_F32)


def _rotate(x, cos, sin_signed, half):
    return x * cos + pltpu.roll(x, half, 1) * sin_signed


def _lower_bound(lb_ref):
    a = lb_ref[...]
    m = jnp.max(a, axis=0, keepdims=True)
    e = jnp.exp(a - m)
    return e[0:1, :] / jnp.sum(e, axis=0, keepdims=True)


def _forget_gate(hf, lbv):
    z = jnp.exp(-jnp.abs(hf))
    r = 1.0 / (1.0 + z)
    zr = z * r
    pos = hf >= 0.0
    sig_p = jnp.where(pos, r, zr)
    sig_n = jnp.where(pos, zr, r)
    one_m = 1.0 - lbv
    return lbv + one_m * sig_p, one_m * sig_n


def _decay_levels(qa, kin, f):
    c, w = f.shape
    n_lev = c.bit_length() - 1
    rowi = lax.broadcasted_iota(jnp.int32, (c, w), 0)
    tot, q, k = f, qa * f, kin
    for lev in range(3):
        half = 1 << lev
        up = (rowi & half) != 0
        yield jnp.where(up, q, 0.0).astype(_BF16), jnp.where(up, 0.0, k).astype(_BF16)
        sib = jnp.where(up, pltpu.roll(tot, half, 0), pltpu.roll(tot, c - half, 0))
        q = jnp.where(up, q * sib, q)
        k = jnp.where(up, k, k * sib)
        tot = tot * sib
    q_b = [q[i * 8:(i + 1) * 8] for i in range(c // 8)]
    k_b = [k[i * 8:(i + 1) * 8] for i in range(c // 8)]
    tot_b = [tot[i * 8:i * 8 + 1] for i in range(c // 8)]
    for lev in range(3, n_lev):
        half = 1 << lev
        zero = jnp.zeros((half, w), _F32)
        yield (jnp.concatenate([q_b[i] if i % 2 else zero for i in range(len(q_b))], axis=0).astype(_BF16),
               jnp.concatenate([zero if i % 2 else k_b[i] for i in range(len(k_b))], axis=0).astype(_BF16))
        q_b = [jnp.concatenate([q_b[i], q_b[i + 1] * tot_b[i]], axis=0) for i in range(0, len(q_b), 2)]
        k_b = [jnp.concatenate([k_b[i] * tot_b[i + 1], k_b[i + 1]], axis=0) for i in range(0, len(k_b), 2)]
        tot_b = [tot_b[i] * tot_b[i + 1] for i in range(0, len(tot_b), 2)]
    yield q_b[0].astype(_BF16), k_b[0].astype(_BF16), tot_b[0]


def _proj_stages(x_ref, rows, ng_ref, win_ref, p_ref, gw):
    h = _rms(x_ref[0, rows, :], ng_ref[...]).astype(_BF16)
    yield
    width = gw // 2
    for sec in range(8 * gw // width):
        cols = slice(sec * width, (sec + 1) * width)
        p_ref[:, cols] = _dot(h, win_ref[:, cols])
        yield


def _filler(stages):
    def fill(n):
        if n is None:
            for _ in stages:
                pass
        else:
            for _ in range(n):
                next(stages, None)
    return fill


def _log_gamma(hh):
    return math.log1p(-(2.0 ** (-5 - hh)))


def _retention_tables(rq_ref, rk_ref, rdec_ref, heads, tile):
    row = lax.broadcasted_iota(jnp.int32, rq_ref.shape[1:], 0).astype(_F32)
    ti = lax.broadcasted_iota(jnp.int32, (tile, tile), 0)
    si = lax.broadcasted_iota(jnp.int32, (tile, tile), 1)
    lag = jnp.maximum(ti - si, 0).astype(_F32)
    for hh in range(heads):
        lg = _log_gamma(hh)
        rq_ref[hh] = jnp.exp(lg * (row + 1.0))
        rk_ref[hh] = jnp.exp(lg * (tile - 1.0 - row))
        rdec_ref[hh] = jnp.where(ti >= si, jnp.exp(lg * lag), 0.0)


def _mixer(x_ref, rows, p_ref, fill, *, cos_ref, sin_ref, wout_ref, rg_ref, hgg_ref, lb_ref, fg_ref,
           y_ref, sret_ref, shg_ref, rq_ref, rk_ref, rdec_ref, heads, dh, tile, chunk):
    gw = heads * dh

    def p(sec, hh=None, rs=slice(None)):
        if hh is None:
            return p_ref[rs, sec * gw:(sec + 1) * gw]
        return p_ref[rs, sec * gw + hh * dh:sec * gw + (hh + 1) * dh]

    o_all = [None] * (2 * heads)

    cos, sin = cos_ref[rows, :], sin_ref[rows, :]

    def ret_issue(hh):
        q = _rotate(p(0, hh), cos, sin, dh // 2).astype(_BF16)
        kt = (_rotate(p(1, hh), cos, sin, dh // 2) * (dh ** -0.5)).astype(_BF16).T
        v = p(2, hh)
        s_old = sret_ref[0, hh]
        qk_qs = _dot(q, jnp.concatenate([kt, s_old.astype(_BF16)], axis=1))
        sret_ref[0, hh] = math.exp(tile * _log_gamma(hh)) * s_old + _dot(kt, (v * rk_ref[hh]).astype(_BF16))
        return hh, qk_qs, v

    def ret_finish(hh, qk_qs, v):
        sl = slice(hh * dh, (hh + 1) * dh)
        a = qk_qs[:, :tile] * rdec_ref[hh]
        o = qk_qs[:, tile:] * rq_ref[hh] + _dot(a.astype(_BF16), v.astype(_BF16))
        o_all[hh] = (_rms(o, rg_ref[:, sl]) * _silu(p(3, hh))).astype(_BF16)

    lbv = _lower_bound(lb_ref)
    ci = lax.broadcasted_iota(jnp.int32, (chunk, chunk), 0)
    cj = lax.broadcasted_iota(jnp.int32, (chunk, chunk), 1)
    cxor = ci ^ cj
    n_lev = chunk.bit_length() - 1
    diag = ci == cj
    lev_mask = [(ci > cj) & ((cxor >> lev) == 1) for lev in range(n_lev)]
    n_chunks = tile // chunk
    o_hg_chunks = [[None] * n_chunks for _ in range(heads)]

    def hg_issue(c, hh):
        rs = slice(c * chunk, (c + 1) * chunk)
        sl = slice(hh * dh, (hh + 1) * dh)
        f, gk = _forget_gate(p(5, hh, rs), lbv[:, sl])
        gq = _silu(p(4, hh, rs))
        gv = p(6, hh, rs).astype(_BF16)
        a = jnp.where(diag, jnp.sum(gq * gk, axis=-1, keepdims=True), 0.0)
        *levels, (q_top, k_top, f_tot) = _decay_levels(gq, gk, f)
        for lev, (q_lv, k_lv) in enumerate(levels):
            a = jnp.where(lev_mask[lev], _dot_nt(q_lv, k_lv), a)
        s_old = shg_ref[0, hh]
        o_inter = _dot(q_top, s_old.astype(_BF16))
        f_col = jnp.broadcast_to(f_tot, (dh, dh)).T
        shg_ref[0, hh] = s_old * f_col + _dot_tn(k_top, gv)
        return c, hh, a.astype(_BF16), o_inter, gv

    def hg_finish(c, hh, a, o_inter, gv):
        o_hg_chunks[hh][c] = o_inter + _dot(a, gv)

    fill(1)
    pend = None
    for hh in range(heads):
        ctx = ret_issue(hh)
        fill(1)
        if pend is not None:
            ret_finish(*pend)
        pend = ctx
    fill(1)
    ret_finish(*pend)
    pend = None
    for c in range(n_chunks):
        for hh in range(heads):
            ctx = hg_issue(c, hh)
            fill(1)
            if pend is not None:
                hg_finish(*pend)
            pend = ctx
    fill(2)
    hg_finish(*pend)
    fill(1)
    for hh in range(heads):
        sl = slice(hh * dh, (hh + 1) * dh)
        o = jnp.concatenate(o_hg_chunks[hh], axis=0)
        o_all[heads + hh] = (_rms(o, hgg_ref[:, sl]) * _silu(p(7, hh))).astype(_BF16)
    fill(None)
    o_cat = jnp.concatenate(o_all, axis=-1)
    y = x_ref[0, rows, :] + _dot(o_cat, wout_ref[...])
    y_ref[0, rows, :] = _rms(y, fg_ref[...])


def _prompt_kernel(xp_ref, xn_ref, cos_ref, sin_ref, win_ref, wout_ref, ng_ref, rg_ref, hgg_ref, lb_ref, fg_ref,
                   y_ref, sret_ref, shg_ref, pa_ref, pb_ref, rq_ref, rk_ref, rdec_ref, win_s, wout_s,
                   *, heads, dh, tile, chunk, pairs):
    step = pl.program_id(0)
    pair = step % pairs
    gw = heads * dh
    rows_a, rows_b = slice(0, tile), slice(tile, 2 * tile)

    @pl.when(pair == 0)
    def _():
        sret_ref[...] = jnp.zeros_like(sret_ref)
        shg_ref[...] = jnp.zeros_like(shg_ref)

    @pl.when(step == 0)
    def _():
        _retention_tables(rq_ref, rk_ref, rdec_ref, heads, tile)
        for sec in range(win_ref.shape[1] // gw):
            win_s[:, sec * gw:(sec + 1) * gw] = win_ref[:, sec * gw:(sec + 1) * gw]
        wout_s[...] = wout_ref[...]
        for _ in _proj_stages(xp_ref, rows_a, ng_ref, win_s, pa_ref, gw):
            pass

    mixer = functools.partial(
        _mixer, cos_ref=cos_ref, sin_ref=sin_ref, wout_ref=wout_s, rg_ref=rg_ref, hgg_ref=hgg_ref,
        lb_ref=lb_ref, fg_ref=fg_ref, y_ref=y_ref, sret_ref=sret_ref, shg_ref=shg_ref,
        rq_ref=rq_ref, rk_ref=rk_ref, rdec_ref=rdec_ref, heads=heads, dh=dh, tile=tile, chunk=chunk)
    mixer(xp_ref, rows_a, pa_ref, _filler(_proj_stages(xp_ref, rows_b, ng_ref, win_s, pb_ref, gw)))
    mixer(xp_ref, rows_b, pb_ref, _filler(_proj_stages(xn_ref, rows_a, ng_ref, win_s, pa_ref, gw)))


def _prompt_call(x, cos, sin, win, wout, ng, rg, hgg, lb, fg, heads, dh):
    bsz, seq, d = x.shape
    tile = min(PROMPT_TILE, seq // 2)
    chunk = min(HGRN_CHUNK, tile)
    assert seq % (2 * tile) == 0 and tile % chunk == 0 and chunk & (chunk - 1) == 0 and chunk >= 16
    gw = heads * dh
    pairs = seq // (2 * tile)
    steps = bsz * pairs

    def nxt(s):
        s1 = jnp.minimum(s + 1, steps - 1)
        return (s1 // pairs, 2 * (s1 % pairs), 0)

    const = lambda *shape: pl.BlockSpec(shape, lambda s: (0,) * len(shape))
    once = lambda *shape: pl.BlockSpec(shape, lambda s: (0,) * len(shape), pipeline_mode=pl.Buffered(1))
    state_spec = pl.BlockSpec((1, heads, dh, dh), lambda s: (s // pairs, 0, 0, 0))
    pair_spec = pl.BlockSpec((1, 2 * tile, d), lambda s: (s // pairs, s % pairs, 0))
    rope_spec = pl.BlockSpec((2 * tile, dh), lambda s: (s % pairs, 0))
    body = functools.partial(_prompt_kernel, heads=heads, dh=dh, tile=tile, chunk=chunk, pairs=pairs)
    return pl.pallas_call(
        body,
        grid=(steps,),
        in_specs=[
            pair_spec, pl.BlockSpec((1, tile, d), nxt), rope_spec, rope_spec,
            once(d, 8 * gw), once(2 * gw, d), const(1, d), const(1, gw), const(1, gw),
            const(lb.shape[0], gw), const(1, d),
        ],
        out_specs=[pair_spec, state_spec, state_spec],
        out_shape=[
            jax.ShapeDtypeStruct((bsz, seq, d), _F32),
            jax.ShapeDtypeStruct((bsz, heads, dh, dh), _F32),
            jax.ShapeDtypeStruct((bsz, heads, dh, dh), _F32),
        ],
        scratch_shapes=[pltpu.VMEM((tile, 8 * gw), _F32), pltpu.VMEM((tile, 8 * gw), _F32),
                        pltpu.VMEM((heads, tile, dh), _F32), pltpu.VMEM((heads, tile, dh), _F32),
                        pltpu.VMEM((heads, tile, tile), _F32),
                        pltpu.VMEM((d, 8 * gw), _BF16), pltpu.VMEM((2 * gw, d), _BF16)],
        compiler_params=pltpu.CompilerParams(
            dimension_semantics=("arbitrary",), vmem_limit_bytes=VMEM_LIMIT_BYTES),
        name="prompt_mixer",
    )(x, x, cos, sin, win, wout, ng, rg, hgg, lb, fg)


def _sample_kernel(x_ref, cos_ref, sin_ref, win_ref, wout_ref, ng_ref, rg_ref, hgg_ref, lb_ref, fg_ref,
                   sret_in, shg_in, y_ref, sret_out, shg_out,
                   rq_s, rk_s, rv_s, rgate_s, hq_s, hk_s, hi_s, hgate_s, hft_s, *, heads, dh, block):
    i = pl.program_id(0)
    gw = heads * dh
    nb = x_ref.shape[0]

    @pl.when(i == 0)
    def _():
        h = _rms(x_ref[...], ng_ref[...]).astype(_BF16)

        def proj(sec):
            return _dot(h, win_ref[:, sec * gw:(sec + 1) * gw])

        cos, sin = cos_ref[...], sin_ref[...]

        def rot(pj):
            return jnp.concatenate(
                [_rotate(pj[:, hh * dh:(hh + 1) * dh], cos, sin, dh // 2) for hh in range(heads)], axis=-1)

        rq_s[...] = rot(proj(0))
        rk_s[...] = rot(proj(1)) * (dh ** -0.5)
        rv_s[...] = proj(2)
        rgate_s[...] = _silu(proj(3))
        f, kin = _forget_gate(proj(5), _lower_bound(lb_ref))
        hq_s[...] = _silu(proj(4))
        hk_s[...] = kin
        hft_s[...] = f.T
        hi_s[...] = proj(6)
        hgate_s[...] = _silu(proj(7))

    rows = pl.ds(pl.multiple_of(i * block, block), block)
    shift = (nb - i * block) % nb
    hft = pltpu.roll(hft_s[...], shift, 1)
    rq, rk, rv = rq_s[rows, :].astype(_BF16), rk_s[rows, :].astype(_BF16), rv_s[rows, :].astype(_BF16)
    hq, hk, hi = hq_s[rows, :].astype(_BF16), hk_s[rows, :].astype(_BF16), hi_s[rows, :].astype(_BF16)
    rowi = lax.broadcasted_iota(jnp.int32, (block, gw), 0)
    zero = jnp.zeros((block, gw), _BF16)

    o_ret = [[None] * block for _ in range(heads)]
    o_hg = [[None] * block for _ in range(heads)]
    for bi in range(block):
        rk_bi = jnp.where(rowi == bi, rk, zero)
        hk_bi = jnp.where(rowi == bi, hk, zero)
        for hh in range(heads):
            sl = slice(hh * dh, (hh + 1) * dh)
            gamma = 1.0 - 2.0 ** (-5 - hh)
            s_new = gamma * sret_in[bi, hh] + _dot_tn(rk_bi[:, sl], rv[:, sl])
            sret_out[bi, hh] = s_new
            o_ret[hh][bi] = _dot(rq[:, sl], s_new.astype(_BF16))[bi:bi + 1, :]
            f_col = jnp.broadcast_to(hft[hh * dh:(hh + 1) * dh, bi:bi + 1], (dh, dh))
            g_new = f_col * shg_in[bi, hh] + _dot_tn(hk_bi[:, sl], hi[:, sl])
            shg_out[bi, hh] = g_new
            o_hg[hh][bi] = _dot(hq[:, sl], g_new.astype(_BF16))[bi:bi + 1, :]

    def finish(o_rows, gain_ref, gate):
        outs = []
        for hh in range(heads):
            sl = slice(hh * dh, (hh + 1) * dh)
            o = jnp.concatenate(o_rows[hh], axis=0)
            outs.append(_rms(o, gain_ref[:, sl]) * gate[:, sl])
        return outs

    o_cat = jnp.concatenate(
        finish(o_ret, rg_ref, rgate_s[rows, :]) + finish(o_hg, hgg_ref, hgate_s[rows, :]),
        axis=-1).astype(_BF16)
    y = x_ref[rows, :] + _dot(o_cat, wout_ref[...])
    y_ref[...] = _rms(y, fg_ref[...])


def _sample_call(x, cos, sin, win, wout, ng, rg, hgg, lb, fg, sret, shg):
    nb, d = x.shape
    _, heads, dh, _ = sret.shape
    gw = heads * dh
    block = min(SAMPLE_BLOCK, nb)
    assert nb % block == 0 and nb % 128 == 0
    const = lambda *shape: pl.BlockSpec(shape, lambda i: (0,) * len(shape))
    state_spec = pl.BlockSpec((block, heads, dh, dh), lambda i: (i, 0, 0, 0))
    body = functools.partial(_sample_kernel, heads=heads, dh=dh, block=block)
    row_scratch = pltpu.VMEM((nb, gw), _F32)
    return pl.pallas_call(
        body,
        grid=(nb // block,),
        in_specs=[
            const(nb, d), const(1, dh), const(1, dh),
            const(d, 8 * gw), const(2 * gw, d), const(1, d), const(1, gw), const(1, gw),
            const(lb.shape[0], gw), const(1, d),
            state_spec, state_spec,
        ],
        out_specs=[pl.BlockSpec((block, d), lambda i: (i, 0)), state_spec, state_spec],
        out_shape=[
            jax.ShapeDtypeStruct((nb, d), _F32),
            jax.ShapeDtypeStruct(sret.shape, _F32),
            jax.ShapeDtypeStruct(shg.shape, _F32),
        ],
        scratch_shapes=[row_scratch] * 8 + [pltpu.VMEM((gw, nb), _F32)],
        compiler_params=pltpu.CompilerParams(
            dimension_semantics=("arbitrary",), vmem_limit_bytes=VMEM_LIMIT_BYTES),
        name="sample_mixer",
    )(x, cos, sin, win, wout, ng, rg, hgg, lb, fg, sret, shg)


def _rope_tables(pos, dh):
    half = dh // 2
    freqs = 1.0 / (ROPE_BASE ** (jnp.arange(half, dtype=_F32) / half))
    ang = pos[:, None] * freqs[None, :]
    c, s = jnp.cos(ang), jnp.sin(ang)
    return jnp.concatenate([c, c], axis=-1), jnp.concatenate([-s, s], axis=-1)


def kernel(x_prompt, x_sample, state_ret, state_hgrn, norm_g, w_in, ret_norm_g, hg_norm_g, hg_lb, w_out,
           final_norm_g):
    depth, _, heads, dh, _ = state_ret.shape
    assert depth == 1 and x_sample.shape[1] == 1
    bp, lp, d = x_prompt.shape
    win = w_in[0].astype(_BF16)
    wout = w_out[0].astype(_BF16)
    ng, fg = norm_g[0][None, :], final_norm_g[None, :]
    rg, hgg = ret_norm_g[0][None, :], hg_norm_g[0][None, :]

    cos_p, sin_p = _rope_tables(jnp.arange(lp, dtype=_F32), dh)
    y_p, ret_p, hg_p = _prompt_call(x_prompt, cos_p, sin_p, win, wout, ng, rg, hgg, hg_lb, fg, heads, dh)

    cos_s, sin_s = _rope_tables(PAST_LEN + jnp.arange(1, dtype=_F32), dh)
    y_s, ret_s, hg_s = _sample_call(x_sample[:, 0, :], cos_s, sin_s, win, wout, ng, rg, hgg, hg_lb, fg,
                                    state_ret[0], state_hgrn[0])
    return (y_p, y_s[:, None, :], ret_p[None], hg_p[None], ret_s[None], hg_s[None])
```

```python
import functools
import math

import jax
import jax.numpy as jnp
from jax import lax
from jax.experimental import pallas as pl
from jax.experimental.pallas import tpu as pltpu

PAST_LEN = 16384
ROPE_BASE = 10000.0
NORM_EPS = 1e-6

PROMPT_TILE = 256
HGRN_CHUNK = 128
SAMPLE_BLOCK = 16
PACK_ROWS = 256
VMEM_LIMIT_BYTES = 56 * 1024 * 1024

_F32 = jnp.float32
_BF16 = jnp.bfloat16


def _rms(x, g):
    return x * lax.rsqrt(jnp.mean(x * x, axis=-1, keepdims=True) + NORM_EPS) * g


def _sigmoid(x):
    return 1.0 / (1.0 + jnp.exp(-x))


def _silu(x):
    return x * _sigmoid(x)


def _dot(a, b):
    return jnp.dot(a, b, preferred_element_type=_F32)


def _dot_nt(a, b):
    return lax.dot_general(a, b, (((1,), (1,)), ((), ())), preferred_element_type=_F32)


def _dot_tn(a, b):
    return lax.dot_general(a, b, (((0,), (0,)), ((), ())), preferred_element_type=_F32)


def _weight(w_ref, cols=slice(None)):
    return pltpu.bitcast(w_ref[:, cols], _BF16)


def _pack_kernel(w_ref, o_ref):
    o_ref[...] = pltpu.bitcast(w_ref[...].astype(_BF16), jnp.int32)


def _pack_weight(w):
    k, n = w.shape
    rows = min(k, PACK_ROWS)
    assert k % rows == 0 and rows % 16 == 0
    return pl.pallas_call(
        _pack_kernel,
        grid=(k // rows,),
        in_specs=[pl.BlockSpec((rows, n), lambda i: (i, 0))],
        out_specs=pl.BlockSpec((rows // 2, n), lambda i: (i, 0)),
        out_shape=jax.ShapeDtypeStruct((k // 2, n), jnp.int32),
        compiler_params=pltpu.CompilerParams(dimension_semantics=("arbitrary",)),
        name="pack_weight",
    )(w)


def _rotate(x, cos, sin_signed, half):
    return x * cos + pltpu.roll(x, half, 1) * sin_signed


def _lower_bound(lb_ref):
    a = lb_ref[...]
    m = jnp.max(a, axis=0, keepdims=True)
    e = jnp.exp(a - m)
    return e[0:1, :] / jnp.sum(e, axis=0, keepdims=True)


def _forget_gate(hf, lbv):
    z = jnp.exp(-jnp.abs(hf))
    r = 1.0 / (1.0 + z)
    zr = z * r
    pos = hf >= 0.0
    sig_p = jnp.where(pos, r, zr)
    sig_n = jnp.where(pos, zr, r)
    one_m = 1.0 - lbv
    return lbv + one_m * sig_p, one_m * sig_n


def _decay_levels(qa, kin, f):
    c, w = f.shape
    n_lev = c.bit_length() - 1
    rowi = lax.broadcasted_iota(jnp.int32, (c, w), 0)
    tot, q, k = f, qa * f, kin
    for lev in range(3):
        half = 1 << lev
        up = (rowi & half) != 0
        yield jnp.where(up, q, 0.0).astype(_BF16), jnp.where(up, 0.0, k).astype(_BF16)
        sib = jnp.where(up, pltpu.roll(tot, half, 0), pltpu.roll(tot, c - half, 0))
        q = jnp.where(up, q * sib, q)
        k = jnp.where(up, k, k * sib)
        tot = tot * sib
    q_b = [q[i * 8:(i + 1) * 8] for i in range(c // 8)]
    k_b = [k[i * 8:(i + 1) * 8] for i in range(c // 8)]
    tot_b = [tot[i * 8:i * 8 + 1] for i in range(c // 8)]
    for lev in range(3, n_lev):
        half = 1 << lev
        zero = jnp.zeros((half, w), _F32)
        yield (jnp.concatenate([q_b[i] if i % 2 else zero for i in range(len(q_b))], axis=0).astype(_BF16),
               jnp.concatenate([zero if i % 2 else k_b[i] for i in range(len(k_b))], axis=0).astype(_BF16))
        q_b = [jnp.concatenate([q_b[i], q_b[i + 1] * tot_b[i]], axis=0) for i in range(0, len(q_b), 2)]
        k_b = [jnp.concatenate([k_b[i] * tot_b[i + 1], k_b[i + 1]], axis=0) for i in range(0, len(k_b), 2)]
        tot_b = [tot_b[i] * tot_b[i + 1] for i in range(0, len(tot_b), 2)]
    yield q_b[0].astype(_BF16), k_b[0].astype(_BF16), tot_b[0]


def _proj_stages(x_ref, rows, ng_ref, win_ref, p_ref, gw):
    h = _rms(x_ref[0, rows, :], ng_ref[...]).astype(_BF16)
    yield
    width = gw // 2
    for sec in range(8 * gw // width):
        cols = slice(sec * width, (sec + 1) * width)
        p_ref[:, cols] = _dot(h, _weight(win_ref, cols))
        yield


def _filler(stages):
    def fill(n):
        if n is None:
            for _ in stages:
                pass
        else:
            for _ in range(n):
                next(stages, None)
    return fill


def _log_gamma(hh):
    return math.log1p(-(2.0 ** (-5 - hh)))


def _retention_tables(rq_ref, rk_ref, rdec_ref, heads, tile):
    row = lax.broadcasted_iota(jnp.int32, rq_ref.shape[1:], 0).astype(_F32)
    ti = lax.broadcasted_iota(jnp.int32, (tile, tile), 0)
    si = lax.broadcasted_iota(jnp.int32, (tile, tile), 1)
    lag = jnp.maximum(ti - si, 0).astype(_F32)
    for hh in range(heads):
        lg = _log_gamma(hh)
        rq_ref[hh] = jnp.exp(lg * (row + 1.0))
        rk_ref[hh] = jnp.exp(lg * (tile - 1.0 - row))
        rdec_ref[hh] = jnp.where(ti >= si, jnp.exp(lg * lag), 0.0)


def _mixer(x_ref, rows, p_ref, fill, *, cos_ref, sin_ref, wout_ref, rg_ref, hgg_ref, lb_ref, fg_ref,
           y_ref, sret_ref, shg_ref, rq_ref, rk_ref, rdec_ref, heads, dh, tile, chunk):
    gw = heads * dh

    def p(sec, hh=None, rs=slice(None)):
        if hh is None:
            return p_ref[rs, sec * gw:(sec + 1) * gw]
        return p_ref[rs, sec * gw + hh * dh:sec * gw + (hh + 1) * dh]

    o_all = [None] * (2 * heads)

    cos, sin = cos_ref[rows, :], sin_ref[rows, :]

    def ret_issue(hh):
        q = _rotate(p(0, hh), cos, sin, dh // 2).astype(_BF16)
        kt = (_rotate(p(1, hh), cos, sin, dh // 2) * (dh ** -0.5)).astype(_BF16).T
        v = p(2, hh)
        s_old = sret_ref[0, hh]
        qk_qs = _dot(q, jnp.concatenate([kt, s_old.astype(_BF16)], axis=1))
        sret_ref[0, hh] = math.exp(tile * _log_gamma(hh)) * s_old + _dot(kt, (v * rk_ref[hh]).astype(_BF16))
        return hh, qk_qs, v

    def ret_finish(hh, qk_qs, v):
        sl = slice(hh * dh, (hh + 1) * dh)
        a = qk_qs[:, :tile] * rdec_ref[hh]
        o = qk_qs[:, tile:] * rq_ref[hh] + _dot(a.astype(_BF16), v.astype(_BF16))
        o_all[hh] = (_rms(o, rg_ref[:, sl]) * _silu(p(3, hh))).astype(_BF16)

    lbv = _lower_bound(lb_ref)
    ci = lax.broadcasted_iota(jnp.int32, (chunk, chunk), 0)
    cj = lax.broadcasted_iota(jnp.int32, (chunk, chunk), 1)
    cxor = ci ^ cj
    n_lev = chunk.bit_length() - 1
    diag = ci == cj
    lev_mask = [(ci > cj) & ((cxor >> lev) == 1) for lev in range(n_lev)]
    n_chunks = tile // chunk
    o_hg_chunks = [[None] * n_chunks for _ in range(heads)]

    def hg_issue(c, hh):
        rs = slice(c * chunk, (c + 1) * chunk)
        sl = slice(hh * dh, (hh + 1) * dh)
        f, gk = _forget_gate(p(5, hh, rs), lbv[:, sl])
        gq = _silu(p(4, hh, rs))
        gv = p(6, hh, rs).astype(_BF16)
        a = jnp.where(diag, jnp.sum(gq * gk, axis=-1, keepdims=True), 0.0)
        *levels, (q_top, k_top, f_tot) = _decay_levels(gq, gk, f)
        for lev, (q_lv, k_lv) in enumerate(levels):
            a = jnp.where(lev_mask[lev], _dot_nt(q_lv, k_lv), a)
        s_old = shg_ref[0, hh]
        o_inter = _dot(q_top, s_old.astype(_BF16))
        f_col = jnp.broadcast_to(f_tot, (dh, dh)).T
        shg_ref[0, hh] = s_old * f_col + _dot_tn(k_top, gv)
        return c, hh, a.astype(_BF16), o_inter, gv

    def hg_finish(c, hh, a, o_inter, gv):
        o_hg_chunks[hh][c] = o_inter + _dot(a, gv)

    fill(1)
    pend = None
    for hh in range(heads):
        ctx = ret_issue(hh)
        fill(1)
        if pend is not None:
            ret_finish(*pend)
        pend = ctx
    fill(1)
    ret_finish(*pend)
    pend = None
    for c in range(n_chunks):
        for hh in range(heads):
            ctx = hg_issue(c, hh)
            fill(1)
            if pend is not None:
                hg_finish(*pend)
            pend = ctx
    fill(2)
    hg_finish(*pend)
    fill(1)
    for hh in range(heads):
        sl = slice(hh * dh, (hh + 1) * dh)
        o = jnp.concatenate(o_hg_chunks[hh], axis=0)
        o_all[heads + hh] = (_rms(o, hgg_ref[:, sl]) * _silu(p(7, hh))).astype(_BF16)
    fill(None)
    o_cat = jnp.concatenate(o_all, axis=-1)
    y = x_ref[0, rows, :] + _dot(o_cat, _weight(wout_ref))
    y_ref[0, rows, :] = _rms(y, fg_ref[...])


def _prompt_kernel(xp_ref, xn_ref, cos_ref, sin_ref, win_ref, wout_ref, ng_ref, rg_ref, hgg_ref, lb_ref, fg_ref,
                   y_ref, sret_ref, shg_ref, pa_ref, pb_ref, rq_ref, rk_ref, rdec_ref,
                   *, heads, dh, tile, chunk, pairs):
    step = pl.program_id(0)
    pair = step % pairs
    gw = heads * dh
    rows_a, rows_b = slice(0, tile), slice(tile, 2 * tile)

    @pl.when(pair == 0)
    def _():
        sret_ref[...] = jnp.zeros_like(sret_ref)
        shg_ref[...] = jnp.zeros_like(shg_ref)

    @pl.when(step == 0)
    def _():
        _retention_tables(rq_ref, rk_ref, rdec_ref, heads, tile)
        for _ in _proj_stages(xp_ref, rows_a, ng_ref, win_ref, pa_ref, gw):
            pass

    mixer = functools.partial(
        _mixer, cos_ref=cos_ref, sin_ref=sin_ref, wout_ref=wout_ref, rg_ref=rg_ref, hgg_ref=hgg_ref,
        lb_ref=lb_ref, fg_ref=fg_ref, y_ref=y_ref, sret_ref=sret_ref, shg_ref=shg_ref,
        rq_ref=rq_ref, rk_ref=rk_ref, rdec_ref=rdec_ref, heads=heads, dh=dh, tile=tile, chunk=chunk)
    mixer(xp_ref, rows_a, pa_ref, _filler(_proj_stages(xp_ref, rows_b, ng_ref, win_ref, pb_ref, gw)))
    mixer(xp_ref, rows_b, pb_ref, _filler(_proj_stages(xn_ref, rows_a, ng_ref, win_ref, pa_ref, gw)))


def _prompt_call(x, cos, sin, win, wout, ng, rg, hgg, lb, fg, heads, dh):
    bsz, seq, d = x.shape
    tile = min(PROMPT_TILE, seq // 2)
    chunk = min(HGRN_CHUNK, tile)
    assert seq % (2 * tile) == 0 and tile % chunk == 0 and chunk & (chunk - 1) == 0 and chunk >= 16
    gw = heads * dh
    pairs = seq // (2 * tile)
    steps = bsz * pairs

    def nxt(s):
        s1 = jnp.minimum(s + 1, steps - 1)
        return (s1 // pairs, 2 * (s1 % pairs), 0)

    const = lambda *shape: pl.BlockSpec(shape, lambda s: (0,) * len(shape))
    state_spec = pl.BlockSpec((1, heads, dh, dh), lambda s: (s // pairs, 0, 0, 0))
    pair_spec = pl.BlockSpec((1, 2 * tile, d), lambda s: (s // pairs, s % pairs, 0))
    rope_spec = pl.BlockSpec((2 * tile, dh), lambda s: (s % pairs, 0))
    body = functools.partial(_prompt_kernel, heads=heads, dh=dh, tile=tile, chunk=chunk, pairs=pairs)
    return pl.pallas_call(
        body,
        grid=(steps,),
        in_specs=[
            pair_spec, pl.BlockSpec((1, tile, d), nxt), rope_spec, rope_spec,
            const(d // 2, 8 * gw), const(gw, d), const(1, d), const(1, gw), const(1, gw),
            const(lb.shape[0], gw), const(1, d),
        ],
        out_specs=[pair_spec, state_spec, state_spec],
        out_shape=[
            jax.ShapeDtypeStruct((bsz, seq, d), _F32),
            jax.ShapeDtypeStruct((bsz, heads, dh, dh), _F32),
            jax.ShapeDtypeStruct((bsz, heads, dh, dh), _F32),
        ],
        scratch_shapes=[pltpu.VMEM((tile, 8 * gw), _F32), pltpu.VMEM((tile, 8 * gw), _F32),
                        pltpu.VMEM((heads, tile, dh), _F32), pltpu.VMEM((heads, tile, dh), _F32),
                        pltpu.VMEM((heads, tile, tile), _F32)],
        compiler_params=pltpu.CompilerParams(
            dimension_semantics=("arbitrary",), vmem_limit_bytes=VMEM_LIMIT_BYTES),
        name="prompt_mixer",
    )(x, x, cos, sin, win, wout, ng, rg, hgg, lb, fg)


def _sample_kernel(x_ref, cos_ref, sin_ref, win_ref, wout_ref, ng_ref, rg_ref, hgg_ref, lb_ref, fg_ref,
                   sret_in, shg_in, y_ref, sret_out, shg_out,
                   rq_s, rk_s, rv_s, rgate_s, hq_s, hk_s, hi_s, hgate_s, hft_s, *, heads, dh, block):
    i = pl.program_id(0)
    gw = heads * dh
    nb = x_ref.shape[0]

    @pl.when(i == 0)
    def _():
        h = _rms(x_ref[...], ng_ref[...]).astype(_BF16)

        def proj(sec):
            return _dot(h, _weight(win_ref, slice(sec * gw, (sec + 1) * gw)))

        cos, sin = cos_ref[...], sin_ref[...]

        def rot(pj):
            return jnp.concatenate(
                [_rotate(pj[:, hh * dh:(hh + 1) * dh], cos, sin, dh // 2) for hh in range(heads)], axis=-1)

        rq_s[...] = rot(proj(0))
        rk_s[...] = rot(proj(1)) * (dh ** -0.5)
        rv_s[...] = proj(2)
        rgate_s[...] = _silu(proj(3))
        f, kin = _forget_gate(proj(5), _lower_bound(lb_ref))
        hq_s[...] = _silu(proj(4))
        hk_s[...] = kin
        hft_s[...] = f.T
        hi_s[...] = proj(6)
        hgate_s[...] = _silu(proj(7))

    rows = pl.ds(pl.multiple_of(i * block, block), block)
    shift = (nb - i * block) % nb
    hft = pltpu.roll(hft_s[...], shift, 1)
    rq, rk, rv = rq_s[rows, :].astype(_BF16), rk_s[rows, :].astype(_BF16), rv_s[rows, :].astype(_BF16)
    hq, hk, hi = hq_s[rows, :].astype(_BF16), hk_s[rows, :].astype(_BF16), hi_s[rows, :].astype(_BF16)
    rowi = lax.broadcasted_iota(jnp.int32, (block, gw), 0)
    zero = jnp.zeros((block, gw), _BF16)

    o_ret = [[None] * block for _ in range(heads)]
    o_hg = [[None] * block for _ in range(heads)]
    for bi in range(block):
        rk_bi = jnp.where(rowi == bi, rk, zero)
        hk_bi = jnp.where(rowi == bi, hk, zero)
        for hh in range(heads):
            sl = slice(hh * dh, (hh + 1) * dh)
            gamma = 1.0 - 2.0 ** (-5 - hh)
            s_new = gamma * sret_in[bi, hh] + _dot_tn(rk_bi[:, sl], rv[:, sl])
            sret_out[bi, hh] = s_new
            o_ret[hh][bi] = _dot(rq[:, sl], s_new.astype(_BF16))[bi:bi + 1, :]
            f_col = jnp.broadcast_to(hft[hh * dh:(hh + 1) * dh, bi:bi + 1], (dh, dh))
            g_new = f_col * shg_in[bi, hh] + _dot_tn(hk_bi[:, sl], hi[:, sl])
            shg_out[bi, hh] = g_new
            o_hg[hh][bi] = _dot(hq[:, sl], g_new.astype(_BF16))[bi:bi + 1, :]

    def finish(o_rows, gain_ref, gate):
        outs = []
        for hh in range(heads):
            sl = slice(hh * dh, (hh + 1) * dh)
            o = jnp.concatenate(o_rows[hh], axis=0)
            outs.append(_rms(o, gain_ref[:, sl]) * gate[:, sl])
        return outs

    o_cat = jnp.concatenate(
        finish(o_ret, rg_ref, rgate_s[rows, :]) + finish(o_hg, hgg_ref, hgate_s[rows, :]),
        axis=-1).astype(_BF16)
    y = x_ref[rows, :] + _dot(o_cat, _weight(wout_ref))
    y_ref[...] = _rms(y, fg_ref[...])


def _sample_call(x, cos, sin, win, wout, ng, rg, hgg, lb, fg, sret, shg):
    nb, d = x.shape
    _, heads, dh, _ = sret.shape
    gw = heads * dh
    block = min(SAMPLE_BLOCK, nb)
    assert nb % block == 0 and nb % 128 == 0
    const = lambda *shape: pl.BlockSpec(shape, lambda i: (0,) * len(shape))
    once = lambda *shape: pl.BlockSpec(shape, lambda i: (0,) * len(shape), pipeline_mode=pl.Buffered(1))
    state_spec = pl.BlockSpec((block, heads, dh, dh), lambda i: (i, 0, 0, 0))
    body = functools.partial(_sample_kernel, heads=heads, dh=dh, block=block)
    row_scratch = pltpu.VMEM((nb, gw), _F32)
    return pl.pallas_call(
        body,
        grid=(nb // block,),
        in_specs=[
            const(nb, d), const(1, dh), const(1, dh),
            once(d // 2, 8 * gw), once(gw, d), const(1, d), const(1, gw), const(1, gw),
            const(lb.shape[0], gw), const(1, d),
            state_spec, state_spec,
        ],
        out_specs=[pl.BlockSpec((block, d), lambda i: (i, 0)), state_spec, state_spec],
        out_shape=[
            jax.ShapeDtypeStruct((nb, d), _F32),
            jax.ShapeDtypeStruct(sret.shape, _F32),
            jax.ShapeDtypeStruct(shg.shape, _F32),
        ],
        scratch_shapes=[row_scratch] * 8 + [pltpu.VMEM((gw, nb), _F32)],
        compiler_params=pltpu.CompilerParams(
            dimension_semantics=("arbitrary",), vmem_limit_bytes=VMEM_LIMIT_BYTES),
        name="sample_mixer",
    )(x, cos, sin, win, wout, ng, rg, hgg, lb, fg, sret, shg)


def _rope_tables(pos, dh):
    half = dh // 2
    freqs = 1.0 / (ROPE_BASE ** (jnp.arange(half, dtype=_F32) / half))
    ang = pos[:, None] * freqs[None, :]
    c, s = jnp.cos(ang), jnp.sin(ang)
    return jnp.concatenate([c, c], axis=-1), jnp.concatenate([-s, s], axis=-1)


def kernel(x_prompt, x_sample, state_ret, state_hgrn, norm_g, w_in, ret_norm_g, hg_norm_g, hg_lb, w_out,
           final_norm_g):
    depth, _, heads, dh, _ = state_ret.shape
    assert depth == 1 and x_sample.shape[1] == 1
    bp, lp, d = x_prompt.shape
    win = _pack_weight(w_in[0])
    wout = _pack_weight(w_out[0])
    ng, fg = norm_g[0][None, :], final_norm_g[None, :]
    rg, hgg = ret_norm_g[0][None, :], hg_norm_g[0][None, :]

    cos_p, sin_p = _rope_tables(jnp.arange(lp, dtype=_F32), dh)
    y_p, ret_p, hg_p = _prompt_call(x_prompt, cos_p, sin_p, win, wout, ng, rg, hgg, hg_lb, fg, heads, dh)

    cos_s, sin_s = _rope_tables(PAST_LEN + jnp.arange(1, dtype=_F32), dh)
    y_s, ret_s, hg_s = _sample_call(x_sample[:, 0, :], cos_s, sin_s, win, wout, ng, rg, hgg, hg_lb, fg,
                                    state_ret[0], state_hgrn[0])
    return (y_p, y_s[:, None, :], ret_p[None], hg_p[None], ret_s[None], hg_s[None])
```

```python
import functools
import math

import jax
import jax.numpy as jnp
from jax import lax
from jax.experimental import pallas as pl
from jax.experimental.pallas import tpu as pltpu

PAST_LEN = 16384
ROPE_BASE = 10000.0
NORM_EPS = 1e-6

PROMPT_TILE = 256
HGRN_CHUNK = 128
SAMPLE_BLOCK = 16
PACK_ROWS = 256
VMEM_LIMIT_BYTES = 56 * 1024 * 1024

_F32 = jnp.float32
_BF16 = jnp.bfloat16


def _rms(x, g):
    return x * lax.rsqrt(jnp.mean(x * x, axis=-1, keepdims=True) + NORM_EPS) * g


def _sigmoid(x):
    return 1.0 / (1.0 + jnp.exp(-x))


def _silu(x):
    return x * _sigmoid(x)


def _dot(a, b):
    return jnp.dot(a, b, preferred_element_type=_F32)


def _dot_nt(a, b):
    return lax.dot_general(a, b, (((1,), (1,)), ((), ())), preferred_element_type=_F32)


def _dot_tn(a, b):
    return lax.dot_general(a, b, (((0,), (0,)), ((), ())), preferred_element_type=_F32)


def _weight(w_ref, cols=slice(None)):
    return pltpu.bitcast(w_ref[:, cols], _BF16)


def _pack_kernel(*refs):
    n = len(refs) // 2
    for w_ref, o_ref in zip(refs[:n], refs[n:]):
        o_ref[...] = pltpu.bitcast(w_ref[...].astype(_BF16), jnp.int32)


def _pack_weights(*ws):
    steps = max(w.shape[0] for w in ws) // PACK_ROWS
    rows = [w.shape[0] // steps for w in ws]
    assert all(w.shape[0] == r * steps and r % 16 == 0 for w, r in zip(ws, rows))
    return pl.pallas_call(
        _pack_kernel,
        grid=(steps,),
        in_specs=[pl.BlockSpec((r, w.shape[1]), lambda i: (i, 0)) for w, r in zip(ws, rows)],
        out_specs=[pl.BlockSpec((r // 2, w.shape[1]), lambda i: (i, 0)) for w, r in zip(ws, rows)],
        out_shape=[jax.ShapeDtypeStruct((w.shape[0] // 2, w.shape[1]), jnp.int32) for w in ws],
        compiler_params=pltpu.CompilerParams(dimension_semantics=("arbitrary",)),
        name="pack_weights",
    )(*ws)


def _rotate(x, cos, sin_signed, half):
    return x * cos + pltpu.roll(x, half, 1) * sin_signed


def _lower_bound(lb_ref):
    a = lb_ref[...]
    m = jnp.max(a, axis=0, keepdims=True)
    e = jnp.exp(a - m)
    return e[0:1, :] / jnp.sum(e, axis=0, keepdims=True)


def _forget_gate(hf, lbv):
    z = jnp.exp(-jnp.abs(hf))
    r = 1.0 / (1.0 + z)
    zr = z * r
    pos = hf >= 0.0
    sig_p = jnp.where(pos, r, zr)
    sig_n = jnp.where(pos, zr, r)
    one_m = 1.0 - lbv
    return lbv + one_m * sig_p, one_m * sig_n


def _decay_levels(qa, kin, f):
    c, w = f.shape
    n_lev = c.bit_length() - 1
    rowi = lax.broadcasted_iota(jnp.int32, (c, w), 0)
    tot, q, k = f, qa * f, kin
    for lev in range(3):
        half = 1 << lev
        up = (rowi & half) != 0
        yield jnp.where(up, q, 0.0).astype(_BF16), jnp.where(up, 0.0, k).astype(_BF16)
        sib = jnp.where(up, pltpu.roll(tot, half, 0), pltpu.roll(tot, c - half, 0))
        q = jnp.where(up, q * sib, q)
        k = jnp.where(up, k, k * sib)
        tot = tot * sib
    q_b = [q[i * 8:(i + 1) * 8] for i in range(c // 8)]
    k_b = [k[i * 8:(i + 1) * 8] for i in range(c // 8)]
    tot_b = [tot[i * 8:i * 8 + 1] for i in range(c // 8)]
    for lev in range(3, n_lev):
        half = 1 << lev
        zero = jnp.zeros((half, w), _F32)
        yield (jnp.concatenate([q_b[i] if i % 2 else zero for i in range(len(q_b))], axis=0).astype(_BF16),
               jnp.concatenate([zero if i % 2 else k_b[i] for i in range(len(k_b))], axis=0).astype(_BF16))
        q_b = [jnp.concatenate([q_b[i], q_b[i + 1] * tot_b[i]], axis=0) for i in range(0, len(q_b), 2)]
        k_b = [jnp.concatenate([k_b[i] * tot_b[i + 1], k_b[i + 1]], axis=0) for i in range(0, len(k_b), 2)]
        tot_b = [tot_b[i] * tot_b[i + 1] for i in range(0, len(tot_b), 2)]
    yield q_b[0].astype(_BF16), k_b[0].astype(_BF16), tot_b[0]


def _proj_stages(x_ref, rows, ng_ref, win_ref, p_ref, gw):
    h = _rms(x_ref[0, rows, :], ng_ref[...]).astype(_BF16)
    yield
    width = gw // 2
    for sec in range(8 * gw // width):
        cols = slice(sec * width, (sec + 1) * width)
        p_ref[:, cols] = _dot(h, _weight(win_ref, cols))
        yield


def _filler(stages):
    def fill(n):
        if n is None:
            for _ in stages:
                pass
        else:
            for _ in range(n):
                next(stages, None)
    return fill


def _log_gamma(hh):
    return math.log1p(-(2.0 ** (-5 - hh)))


def _retention_tables(rq_ref, rk_ref, rdec_ref, heads, tile):
    row = lax.broadcasted_iota(jnp.int32, rq_ref.shape[1:], 0).astype(_F32)
    ti = lax.broadcasted_iota(jnp.int32, (tile, tile), 0)
    si = lax.broadcasted_iota(jnp.int32, (tile, tile), 1)
    lag = jnp.maximum(ti - si, 0).astype(_F32)
    for hh in range(heads):
        lg = _log_gamma(hh)
        rq_ref[hh] = jnp.exp(lg * (row + 1.0))
        rk_ref[hh] = jnp.exp(lg * (tile - 1.0 - row))
        rdec_ref[hh] = jnp.where(ti >= si, jnp.exp(lg * lag), 0.0)


def _mixer(x_ref, rows, p_ref, fill, *, cos_ref, sin_ref, wout_ref, rg_ref, hgg_ref, lb_ref, fg_ref,
           y_ref, sret_ref, shg_ref, rq_ref, rk_ref, rdec_ref, heads, dh, tile, chunk):
    gw = heads * dh

    def p(sec, hh=None, rs=slice(None)):
        if hh is None:
            return p_ref[rs, sec * gw:(sec + 1) * gw]
        return p_ref[rs, sec * gw + hh * dh:sec * gw + (hh + 1) * dh]

    o_all = [None] * (2 * heads)

    cos, sin = cos_ref[rows, :], sin_ref[rows, :]

    def ret_issue(hh):
        q = _rotate(p(0, hh), cos, sin, dh // 2).astype(_BF16)
        kt = (_rotate(p(1, hh), cos, sin, dh // 2) * (dh ** -0.5)).astype(_BF16).T
        v = p(2, hh)
        s_old = sret_ref[0, hh]
        qk_qs = _dot(q, jnp.concatenate([kt, s_old.astype(_BF16)], axis=1))
        sret_ref[0, hh] = math.exp(tile * _log_gamma(hh)) * s_old + _dot(kt, (v * rk_ref[hh]).astype(_BF16))
        return hh, qk_qs, v

    def ret_finish(hh, qk_qs, v):
        sl = slice(hh * dh, (hh + 1) * dh)
        a = qk_qs[:, :tile] * rdec_ref[hh]
        o = qk_qs[:, tile:] * rq_ref[hh] + _dot(a.astype(_BF16), v.astype(_BF16))
        o_all[hh] = (_rms(o, rg_ref[:, sl]) * _silu(p(3, hh))).astype(_BF16)

    lbv = _lower_bound(lb_ref)
    ci = lax.broadcasted_iota(jnp.int32, (chunk, chunk), 0)
    cj = lax.broadcasted_iota(jnp.int32, (chunk, chunk), 1)
    cxor = ci ^ cj
    n_lev = chunk.bit_length() - 1
    diag = ci == cj
    lev_mask = [(ci > cj) & ((cxor >> lev) == 1) for lev in range(n_lev)]
    n_chunks = tile // chunk
    o_hg_chunks = [[None] * n_chunks for _ in range(heads)]

    def hg_issue(c, hh):
        rs = slice(c * chunk, (c + 1) * chunk)
        sl = slice(hh * dh, (hh + 1) * dh)
        f, gk = _forget_gate(p(5, hh, rs), lbv[:, sl])
        gq = _silu(p(4, hh, rs))
        gv = p(6, hh, rs).astype(_BF16)
        a = jnp.where(diag, jnp.sum(gq * gk, axis=-1, keepdims=True), 0.0)
        *levels, (q_top, k_top, f_tot) = _decay_levels(gq, gk, f)
        for lev, (q_lv, k_lv) in enumerate(levels):
            a = jnp.where(lev_mask[lev], _dot_nt(q_lv, k_lv), a)
        s_old = shg_ref[0, hh]
        o_inter = _dot(q_top, s_old.astype(_BF16))
        f_col = jnp.broadcast_to(f_tot, (dh, dh)).T
        shg_ref[0, hh] = s_old * f_col + _dot_tn(k_top, gv)
        return c, hh, a.astype(_BF16), o_inter, gv

    def hg_finish(c, hh, a, o_inter, gv):
        o_hg_chunks[hh][c] = o_inter + _dot(a, gv)

    fill(1)
    pend = None
    for hh in range(heads):
        ctx = ret_issue(hh)
        fill(1)
        if pend is not None:
            ret_finish(*pend)
        pend = ctx
    fill(1)
    ret_finish(*pend)
    pend = None
    for c in range(n_chunks):
        for hh in range(heads):
            ctx = hg_issue(c, hh)
            fill(1)
            if pend is not None:
                hg_finish(*pend)
            pend = ctx
    fill(2)
    hg_finish(*pend)
    fill(1)
    for hh in range(heads):
        sl = slice(hh * dh, (hh + 1) * dh)
        o = jnp.concatenate(o_hg_chunks[hh], axis=0)
        o_all[heads + hh] = (_rms(o, hgg_ref[:, sl]) * _silu(p(7, hh))).astype(_BF16)
    fill(None)
    o_cat = jnp.concatenate(o_all, axis=-1)
    y = x_ref[0, rows, :] + _dot(o_cat, _weight(wout_ref))
    y_ref[0, rows, :] = _rms(y, fg_ref[...])


def _prompt_kernel(xp_ref, xn_ref, cos_ref, sin_ref, win_ref, wout_ref, ng_ref, rg_ref, hgg_ref, lb_ref, fg_ref,
                   y_ref, sret_ref, shg_ref, pa_ref, pb_ref, rq_ref, rk_ref, rdec_ref,
                   *, heads, dh, tile, chunk, pairs):
    step = pl.program_id(0)
    pair = step % pairs
    gw = heads * dh
    rows_a, rows_b = slice(0, tile), slice(tile, 2 * tile)

    @pl.when(pair == 0)
    def _():
        sret_ref[...] = jnp.zeros_like(sret_ref)
        shg_ref[...] = jnp.zeros_like(shg_ref)

    @pl.when(step == 0)
    def _():
        _retention_tables(rq_ref, rk_ref, rdec_ref, heads, tile)
        for _ in _proj_stages(xp_ref, rows_a, ng_ref, win_ref, pa_ref, gw):
            pass

    mixer = functools.partial(
        _mixer, cos_ref=cos_ref, sin_ref=sin_ref, wout_ref=wout_ref, rg_ref=rg_ref, hgg_ref=hgg_ref,
        lb_ref=lb_ref, fg_ref=fg_ref, y_ref=y_ref, sret_ref=sret_ref, shg_ref=shg_ref,
        rq_ref=rq_ref, rk_ref=rk_ref, rdec_ref=rdec_ref, heads=heads, dh=dh, tile=tile, chunk=chunk)
    mixer(xp_ref, rows_a, pa_ref, _filler(_proj_stages(xp_ref, rows_b, ng_ref, win_ref, pb_ref, gw)))
    mixer(xp_ref, rows_b, pb_ref, _filler(_proj_stages(xn_ref, rows_a, ng_ref, win_ref, pa_ref, gw)))


def _prompt_call(x, cos, sin, win, wout, ng, rg, hgg, lb, fg, heads, dh):
    bsz, seq, d = x.shape
    tile = min(PROMPT_TILE, seq // 2)
    chunk = min(HGRN_CHUNK, tile)
    assert seq % (2 * tile) == 0 and tile % chunk == 0 and chunk & (chunk - 1) == 0 and chunk >= 16
    gw = heads * dh
    pairs = seq // (2 * tile)
    steps = bsz * pairs

    def nxt(s):
        s1 = jnp.minimum(s + 1, steps - 1)
        return (s1 // pairs, 2 * (s1 % pairs), 0)

    const = lambda *shape: pl.BlockSpec(shape, lambda s: (0,) * len(shape))
    state_spec = pl.BlockSpec((1, heads, dh, dh), lambda s: (s // pairs, 0, 0, 0))
    pair_spec = pl.BlockSpec((1, 2 * tile, d), lambda s: (s // pairs, s % pairs, 0))
    rope_spec = pl.BlockSpec((2 * tile, dh), lambda s: (s % pairs, 0))
    body = functools.partial(_prompt_kernel, heads=heads, dh=dh, tile=tile, chunk=chunk, pairs=pairs)
    return pl.pallas_call(
        body,
        grid=(steps,),
        in_specs=[
            pair_spec, pl.BlockSpec((1, tile, d), nxt), rope_spec, rope_spec,
            const(d // 2, 8 * gw), const(gw, d), const(1, d), const(1, gw), const(1, gw),
            const(lb.shape[0], gw), const(1, d),
        ],
        out_specs=[pair_spec, state_spec, state_spec],
        out_shape=[
            jax.ShapeDtypeStruct((bsz, seq, d), _F32),
            jax.ShapeDtypeStruct((bsz, heads, dh, dh), _F32),
            jax.ShapeDtypeStruct((bsz, heads, dh, dh), _F32),
        ],
        scratch_shapes=[pltpu.VMEM((tile, 8 * gw), _F32), pltpu.VMEM((tile, 8 * gw), _F32),
                        pltpu.VMEM((heads, tile, dh), _F32), pltpu.VMEM((heads, tile, dh), _F32),
                        pltpu.VMEM((heads, tile, tile), _F32)],
        compiler_params=pltpu.CompilerParams(
            dimension_semantics=("arbitrary",), vmem_limit_bytes=VMEM_LIMIT_BYTES),
        name="prompt_mixer",
    )(x, x, cos, sin, win, wout, ng, rg, hgg, lb, fg)


def _sample_kernel(x_ref, cos_ref, sin_ref, win_ref, wout_ref, ng_ref, rg_ref, hgg_ref, lb_ref, fg_ref,
                   sret_in, shg_in, y_ref, sret_out, shg_out,
                   rq_s, rk_s, rv_s, rgate_s, hq_s, hk_s, hi_s, hgate_s, hft_s, *, heads, dh, block):
    i = pl.program_id(0)
    gw = heads * dh
    nb = x_ref.shape[0]

    @pl.when(i == 0)
    def _():
        h = _rms(x_ref[...], ng_ref[...]).astype(_BF16)

        def proj(sec):
            return _dot(h, _weight(win_ref, slice(sec * gw, (sec + 1) * gw)))

        cos, sin = cos_ref[...], sin_ref[...]

        def rot(pj):
            return jnp.concatenate(
                [_rotate(pj[:, hh * dh:(hh + 1) * dh], cos, sin, dh // 2) for hh in range(heads)], axis=-1)

        rq_s[...] = rot(proj(0))
        rk_s[...] = rot(proj(1)) * (dh ** -0.5)
        rv_s[...] = proj(2)
        rgate_s[...] = _silu(proj(3))
        f, kin = _forget_gate(proj(5), _lower_bound(lb_ref))
        hq_s[...] = _silu(proj(4))
        hk_s[...] = kin
        hft_s[...] = f.T
        hi_s[...] = proj(6)
        hgate_s[...] = _silu(proj(7))

    rows = pl.ds(pl.multiple_of(i * block, block), block)
    shift = (nb - i * block) % nb
    hft = pltpu.roll(hft_s[...], shift, 1)
    rq, rk, rv = rq_s[rows, :].astype(_BF16), rk_s[rows, :].astype(_BF16), rv_s[rows, :].astype(_BF16)
    hq, hk, hi = hq_s[rows, :].astype(_BF16), hk_s[rows, :].astype(_BF16), hi_s[rows, :].astype(_BF16)
    rowi = lax.broadcasted_iota(jnp.int32, (block, gw), 0)
    zero = jnp.zeros((block, gw), _BF16)

    o_ret = [[None] * block for _ in range(heads)]
    o_hg = [[None] * block for _ in range(heads)]
    for bi in range(block):
        rk_bi = jnp.where(rowi == bi, rk, zero)
        hk_bi = jnp.where(rowi == bi, hk, zero)
        for hh in range(heads):
            sl = slice(hh * dh, (hh + 1) * dh)
            gamma = 1.0 - 2.0 ** (-5 - hh)
            s_new = gamma * sret_in[bi, hh] + _dot_tn(rk_bi[:, sl], rv[:, sl])
            sret_out[bi, hh] = s_new
            o_ret[hh][bi] = _dot(rq[:, sl], s_new.astype(_BF16))[bi:bi + 1, :]
            f_col = jnp.broadcast_to(hft[hh * dh:(hh + 1) * dh, bi:bi + 1], (dh, dh))
            g_new = f_col * shg_in[bi, hh] + _dot_tn(hk_bi[:, sl], hi[:, sl])
            shg_out[bi, hh] = g_new
            o_hg[hh][bi] = _dot(hq[:, sl], g_new.astype(_BF16))[bi:bi + 1, :]

    def finish(o_rows, gain_ref, gate):
        outs = []
        for hh in range(heads):
            sl = slice(hh * dh, (hh + 1) * dh)
            o = jnp.concatenate(o_rows[hh], axis=0)
            outs.append(_rms(o, gain_ref[:, sl]) * gate[:, sl])
        return outs

    o_cat = jnp.concatenate(
        finish(o_ret, rg_ref, rgate_s[rows, :]) + finish(o_hg, hgg_ref, hgate_s[rows, :]),
        axis=-1).astype(_BF16)
    y = x_ref[rows, :] + _dot(o_cat, _weight(wout_ref))
    y_ref[...] = _rms(y, fg_ref[...])


def _sample_call(x, cos, sin, win, wout, ng, rg, hgg, lb, fg, sret, shg):
    nb, d = x.shape
    _, heads, dh, _ = sret.shape
    gw = heads * dh
    block = min(SAMPLE_BLOCK, nb)
    assert nb % block == 0 and nb % 128 == 0
    const = lambda *shape: pl.BlockSpec(shape, lambda i: (0,) * len(shape))
    once = lambda *shape: pl.BlockSpec(shape, lambda i: (0,) * len(shape), pipeline_mode=pl.Buffered(1))
    state_spec = pl.BlockSpec((block, heads, dh, dh), lambda i: (i, 0, 0, 0))
    body = functools.partial(_sample_kernel, heads=heads, dh=dh, block=block)
    row_scratch = pltpu.VMEM((nb, gw), _F32)
    return pl.pallas_call(
        body,
        grid=(nb // block,),
        in_specs=[
            const(nb, d), const(1, dh), const(1, dh),
            once(d // 2, 8 * gw), once(gw, d), const(1, d), const(1, gw), const(1, gw),
            const(lb.shape[0], gw), const(1, d),
            state_spec, state_spec,
        ],
        out_specs=[pl.BlockSpec((block, d), lambda i: (i, 0)), state_spec, state_spec],
        out_shape=[
            jax.ShapeDtypeStruct((nb, d), _F32),
            jax.ShapeDtypeStruct(sret.shape, _F32),
            jax.ShapeDtypeStruct(shg.shape, _F32),
        ],
        scratch_shapes=[row_scratch] * 8 + [pltpu.VMEM((gw, nb), _F32)],
        compiler_params=pltpu.CompilerParams(
            dimension_semantics=("arbitrary",), vmem_limit_bytes=VMEM_LIMIT_BYTES),
        name="sample_mixer",
    )(x, cos, sin, win, wout, ng, rg, hgg, lb, fg, sret, shg)


def _rope_tables(pos, dh):
    half = dh // 2
    lane = jnp.arange(dh)
    freqs = 1.0 / (ROPE_BASE ** ((lane % half).astype(_F32) / half))
    ang = pos[:, None] * freqs[None, :]
    return jnp.cos(ang), jnp.sin(ang) * jnp.where(lane < half, -1.0, 1.0)[None, :]


def kernel(x_prompt, x_sample, state_ret, state_hgrn, norm_g, w_in, ret_norm_g, hg_norm_g, hg_lb, w_out,
           final_norm_g):
    depth, _, heads, dh, _ = state_ret.shape
    assert depth == 1 and x_sample.shape[1] == 1
    bp, lp, d = x_prompt.shape
    win, wout = _pack_weights(w_in[0], w_out[0])
    ng, fg = norm_g[0][None, :], final_norm_g[None, :]
    rg, hgg = ret_norm_g[0][None, :], hg_norm_g[0][None, :]

    cos_p, sin_p = _rope_tables(jnp.arange(lp, dtype=_F32), dh)
    y_p, ret_p, hg_p = _prompt_call(x_prompt, cos_p, sin_p, win, wout, ng, rg, hgg, hg_lb, fg, heads, dh)

    cos_s, sin_s = _rope_tables(PAST_LEN + jnp.arange(1, dtype=_F32), dh)
    y_s, ret_s, hg_s = _sample_call(x_sample[:, 0, :], cos_s, sin_s, win, wout, ng, rg, hgg, hg_lb, fg,
                                    state_ret[0], state_hgrn[0])
    return (y_p, y_s[:, None, :], ret_p[None], hg_p[None], ret_s[None], hg_s[None])
```

```python
import functools
import math

import jax
import jax.numpy as jnp
from jax import lax
from jax.experimental import pallas as pl
from jax.experimental.pallas import tpu as pltpu

PAST_LEN = 16384
ROPE_BASE = 10000.0
NORM_EPS = 1e-6

PROMPT_TILE = 256
HGRN_CHUNK = 128
PROJ_GROUPS = 16
PACK_ROWS = 256
VMEM_LIMIT_BYTES = 56 * 1024 * 1024

_F32 = jnp.float32
_BF16 = jnp.bfloat16


def _rms(x, g):
    return x * lax.rsqrt(jnp.mean(x * x, axis=-1, keepdims=True) + NORM_EPS) * g


def _sigmoid(x):
    return 1.0 / (1.0 + jnp.exp(-x))


def _silu(x):
    return x * _sigmoid(x)


def _dot(a, b):
    return jnp.dot(a, b, preferred_element_type=_F32)


def _dot_nt(a, b):
    return lax.dot_general(a, b, (((1,), (1,)), ((), ())), preferred_element_type=_F32)


def _dot_tn(a, b):
    return lax.dot_general(a, b, (((0,), (0,)), ((), ())), preferred_element_type=_F32)


def _weight(w_ref, cols=slice(None)):
    return pltpu.bitcast(w_ref[:, cols], _BF16)


def _pack_kernel(*refs):
    n = len(refs) // 2
    for w_ref, o_ref in zip(refs[:n], refs[n:]):
        o_ref[...] = pltpu.bitcast(w_ref[...].astype(_BF16), jnp.int32)


def _pack_weights(*ws):
    steps = max(w.shape[0] for w in ws) // PACK_ROWS
    rows = [w.shape[0] // steps for w in ws]
    assert all(w.shape[0] == r * steps and r % 16 == 0 for w, r in zip(ws, rows))
    return pl.pallas_call(
        _pack_kernel,
        grid=(steps,),
        in_specs=[pl.BlockSpec((r, w.shape[1]), lambda i: (i, 0)) for w, r in zip(ws, rows)],
        out_specs=[pl.BlockSpec((r // 2, w.shape[1]), lambda i: (i, 0)) for w, r in zip(ws, rows)],
        out_shape=[jax.ShapeDtypeStruct((w.shape[0] // 2, w.shape[1]), jnp.int32) for w in ws],
        compiler_params=pltpu.CompilerParams(dimension_semantics=("arbitrary",)),
        name="pack_weights",
    )(*ws)


def _rotate(x, cos, sin_signed, half):
    return x * cos + pltpu.roll(x, half, 1) * sin_signed


def _lower_bound(lb_ref):
    a = lb_ref[...]
    m = jnp.max(a, axis=0, keepdims=True)
    e = jnp.exp(a - m)
    return e[0:1, :] / jnp.sum(e, axis=0, keepdims=True)


def _forget_gate(hf, lbv):
    z = jnp.exp(-jnp.abs(hf))
    r = 1.0 / (1.0 + z)
    zr = z * r
    pos = hf >= 0.0
    sig_p = jnp.where(pos, r, zr)
    sig_n = jnp.where(pos, zr, r)
    one_m = 1.0 - lbv
    return lbv + one_m * sig_p, one_m * sig_n


def _decay_levels(qa, kin, f):
    c, w = f.shape
    n_lev = c.bit_length() - 1
    rowi = lax.broadcasted_iota(jnp.int32, (c, w), 0)
    tot, q, k = f, qa * f, kin
    for lev in range(3):
        half = 1 << lev
        up = (rowi & half) != 0
        yield jnp.where(up, q, 0.0).astype(_BF16), jnp.where(up, 0.0, k).astype(_BF16)
        sib = jnp.where(up, pltpu.roll(tot, half, 0), pltpu.roll(tot, c - half, 0))
        q = jnp.where(up, q * sib, q)
        k = jnp.where(up, k, k * sib)
        tot = tot * sib
    q_b = [q[i * 8:(i + 1) * 8] for i in range(c // 8)]
    k_b = [k[i * 8:(i + 1) * 8] for i in range(c // 8)]
    tot_b = [tot[i * 8:i * 8 + 1] for i in range(c // 8)]
    for lev in range(3, n_lev):
        half = 1 << lev
        zero = jnp.zeros((half, w), _F32)
        yield (jnp.concatenate([q_b[i] if i % 2 else zero for i in range(len(q_b))], axis=0).astype(_BF16),
               jnp.concatenate([zero if i % 2 else k_b[i] for i in range(len(k_b))], axis=0).astype(_BF16))
        q_b = [jnp.concatenate([q_b[i], q_b[i + 1] * tot_b[i]], axis=0) for i in range(0, len(q_b), 2)]
        k_b = [jnp.concatenate([k_b[i] * tot_b[i + 1], k_b[i + 1]], axis=0) for i in range(0, len(k_b), 2)]
        tot_b = [tot_b[i] * tot_b[i + 1] for i in range(0, len(tot_b), 2)]
    yield q_b[0].astype(_BF16), k_b[0].astype(_BF16), tot_b[0]


def _proj_stages(x_ref, rows, ng_ref, win_ref, p_ref):
    h = _rms(x_ref[0, rows, :], ng_ref[...]).astype(_BF16)
    yield
    width = p_ref.shape[1] // PROJ_GROUPS
    for sec in range(PROJ_GROUPS):
        cols = slice(sec * width, (sec + 1) * width)
        p_ref[:, cols] = _dot(h, _weight(win_ref, cols))
        yield


def _zip_stages(main, side, every):
    for n, _ in enumerate(main, 1):
        if n % every == 0:
            next(side, None)
        yield


def _filler(stages):
    def fill(n):
        if n is None:
            for _ in stages:
                pass
        else:
            for _ in range(n):
                next(stages, None)
    return fill


def _log_gamma(hh):
    return math.log1p(-(2.0 ** (-5 - hh)))


def _retention_tables(rq_ref, rk_ref, rdec_ref, heads, tile):
    row = lax.broadcasted_iota(jnp.int32, rq_ref.shape[1:], 0).astype(_F32)
    ti = lax.broadcasted_iota(jnp.int32, (tile, tile), 0)
    si = lax.broadcasted_iota(jnp.int32, (tile, tile), 1)
    lag = jnp.maximum(ti - si, 0).astype(_F32)
    for hh in range(heads):
        lg = _log_gamma(hh)
        rq_ref[hh] = jnp.exp(lg * (row + 1.0))
        rk_ref[hh] = jnp.exp(lg * (tile - 1.0 - row))
        rdec_ref[hh] = jnp.where(ti >= si, jnp.exp(lg * lag), 0.0)


def _mixer(x_ref, rows, p_ref, fill, *, cos_ref, sin_ref, wout_ref, rg_ref, hgg_ref, lb_ref, fg_ref,
           y_ref, sret_ref, shg_ref, rq_ref, rk_ref, rdec_ref, heads, dh, tile, chunk):
    gw = heads * dh

    def p(sec, hh=None, rs=slice(None)):
        if hh is None:
            return p_ref[rs, sec * gw:(sec + 1) * gw]
        return p_ref[rs, sec * gw + hh * dh:sec * gw + (hh + 1) * dh]

    o_all = [None] * (2 * heads)

    cos, sin = cos_ref[rows, :], sin_ref[rows, :]

    def ret_issue(hh):
        q = _rotate(p(0, hh), cos, sin, dh // 2).astype(_BF16)
        kt = (_rotate(p(1, hh), cos, sin, dh // 2) * (dh ** -0.5)).astype(_BF16).T
        v = p(2, hh)
        s_old = sret_ref[0, hh]
        qk_qs = _dot(q, jnp.concatenate([kt, s_old.astype(_BF16)], axis=1))
        sret_ref[0, hh] = math.exp(tile * _log_gamma(hh)) * s_old + _dot(kt, (v * rk_ref[hh]).astype(_BF16))
        return hh, qk_qs, v

    def ret_finish(hh, qk_qs, v):
        sl = slice(hh * dh, (hh + 1) * dh)
        a = qk_qs[:, :tile] * rdec_ref[hh]
        o = qk_qs[:, tile:] * rq_ref[hh] + _dot(a.astype(_BF16), v.astype(_BF16))
        o_all[hh] = (_rms(o, rg_ref[:, sl]) * _silu(p(3, hh))).astype(_BF16)

    lbv = _lower_bound(lb_ref)
    ci = lax.broadcasted_iota(jnp.int32, (chunk, chunk), 0)
    cj = lax.broadcasted_iota(jnp.int32, (chunk, chunk), 1)
    cxor = ci ^ cj
    n_lev = chunk.bit_length() - 1
    diag = ci == cj
    lev_mask = [(ci > cj) & ((cxor >> lev) == 1) for lev in range(n_lev)]
    n_chunks = tile // chunk
    o_hg_chunks = [[None] * n_chunks for _ in range(heads)]

    def hg_issue(c, hh):
        rs = slice(c * chunk, (c + 1) * chunk)
        sl = slice(hh * dh, (hh + 1) * dh)
        f, gk = _forget_gate(p(5, hh, rs), lbv[:, sl])
        gq = _silu(p(4, hh, rs))
        gv = p(6, hh, rs).astype(_BF16)
        a = jnp.where(diag, jnp.sum(gq * gk, axis=-1, keepdims=True), 0.0)
        *levels, (q_top, k_top, f_tot) = _decay_levels(gq, gk, f)
        for lev, (q_lv, k_lv) in enumerate(levels):
            a = jnp.where(lev_mask[lev], _dot_nt(q_lv, k_lv), a)
        s_old = shg_ref[0, hh]
        o_inter = _dot(q_top, s_old.astype(_BF16))
        f_col = jnp.broadcast_to(f_tot, (dh, dh)).T
        shg_ref[0, hh] = s_old * f_col + _dot_tn(k_top, gv)
        return c, hh, a.astype(_BF16), o_inter, gv

    def hg_finish(c, hh, a, o_inter, gv):
        o_hg_chunks[hh][c] = o_inter + _dot(a, gv)

    fill(1)
    pend = None
    for hh in range(heads):
        ctx = ret_issue(hh)
        fill(1)
        if pend is not None:
            ret_finish(*pend)
        pend = ctx
    fill(1)
    ret_finish(*pend)
    pend = None
    for c in range(n_chunks):
        for hh in range(heads):
            ctx = hg_issue(c, hh)
            fill(1)
            if pend is not None:
                hg_finish(*pend)
            pend = ctx
    fill(2)
    hg_finish(*pend)
    fill(1)
    for hh in range(heads):
        sl = slice(hh * dh, (hh + 1) * dh)
        o = jnp.concatenate(o_hg_chunks[hh], axis=0)
        o_all[heads + hh] = (_rms(o, hgg_ref[:, sl]) * _silu(p(7, hh))).astype(_BF16)
    fill(None)
    o_cat = jnp.concatenate(o_all, axis=-1)
    y = x_ref[0, rows, :] + _dot(o_cat, _weight(wout_ref))
    y_ref[0, rows, :] = _rms(y, fg_ref[...])


def _decode_project(x_ref, cos_ref, sin_ref, win_ref, ng_ref, lb_ref, dec, *, heads, dh):
    rqt_s, rkt_s, rv_s, rgate_s, hqt_s, hkt_s, hi_s, hgate_s, hft_s = dec
    gw = heads * dh
    h = _rms(x_ref[...], ng_ref[...]).astype(_BF16)

    def proj(sec):
        return _dot(h, _weight(win_ref, slice(sec * gw, (sec + 1) * gw)))

    cos, sin = cos_ref[...], sin_ref[...]

    def rot(pj):
        return jnp.concatenate(
            [_rotate(pj[:, hh * dh:(hh + 1) * dh], cos, sin, dh // 2) for hh in range(heads)], axis=-1)

    rqt_s[...] = rot(proj(0)).T
    rkt_s[...] = (rot(proj(1)) * (dh ** -0.5)).T
    rv_s[...] = proj(2)
    rgate_s[...] = _silu(proj(3))
    f, kin = _forget_gate(proj(5), _lower_bound(lb_ref))
    hqt_s[...] = _silu(proj(4)).T
    hkt_s[...] = kin.T
    hft_s[...] = f.T
    hi_s[...] = proj(6)
    hgate_s[...] = _silu(proj(7))


def _decode_stages(step, x_ref, wout_ref, rg_ref, hgg_ref, fg_ref, sret_in, shg_in, y_ref, sret_out, shg_out, dec,
                   *, heads, dh, per_step):
    rqt_s, rkt_s, rv_s, rgate_s, hqt_s, hkt_s, hi_s, hgate_s, hft_s = dec
    nb = x_ref.shape[0]
    wrows = max(per_step, 8)
    share = wrows // per_step
    assert share in (1, 2) and wrows % 8 == 0
    win = pl.ds(pl.multiple_of((step // share) * wrows, 8), wrows)
    base = (step % share) * per_step
    shift = (nb - step * per_step) % nb
    rolled = {}

    def col(ref_s, hh, bl):
        if id(ref_s) not in rolled:
            rolled[id(ref_s)] = pltpu.roll(ref_s[...], shift, 1)
        return jnp.broadcast_to(rolled[id(ref_s)][hh * dh:(hh + 1) * dh, bl:bl + 1], (dh, dh))

    def row_of(ref_s, sl, mine):
        return jnp.sum(jnp.where(mine, ref_s[win, sl], 0.0), axis=0, keepdims=True)

    rowi = lax.broadcasted_iota(jnp.int32, (wrows, dh), 0)
    o_ret = [jnp.zeros((wrows, dh), _F32)] * heads
    o_hg = [jnp.zeros((wrows, dh), _F32)] * heads
    for bl in range(per_step):
        mine = rowi == base + bl
        for hh in range(heads):
            sl = slice(hh * dh, (hh + 1) * dh)
            gamma = 1.0 - 2.0 ** (-5 - hh)
            s_new = gamma * sret_in[bl, hh] + col(rkt_s, hh, bl) * row_of(rv_s, sl, mine)
            sret_out[bl, hh] = s_new
            o = jnp.sum(s_new * col(rqt_s, hh, bl), axis=0, keepdims=True)
            o_ret[hh] = jnp.where(mine, o, o_ret[hh])
            g_new = col(hft_s, hh, bl) * shg_in[bl, hh] + col(hkt_s, hh, bl) * row_of(hi_s, sl, mine)
            shg_out[bl, hh] = g_new
            o = jnp.sum(g_new * col(hqt_s, hh, bl), axis=0, keepdims=True)
            o_hg[hh] = jnp.where(mine, o, o_hg[hh])
            yield

    def finish(o_rows, gain_ref, gate):
        return [_rms(o, gain_ref[:, hh * dh:(hh + 1) * dh]) * gate[:, hh * dh:(hh + 1) * dh]
                for hh, o in enumerate(o_rows)]

    o_cat = jnp.concatenate(
        finish(o_ret, rg_ref, rgate_s[win, :]) + finish(o_hg, hgg_ref, hgate_s[win, :]), axis=-1).astype(_BF16)
    y = _rms(x_ref[win, :] + _dot(o_cat, _weight(wout_ref)), fg_ref[...])
    if share == 1:
        y_ref[win, :] = y
    else:
        @pl.when(base == 0)
        def _():
            y_ref[win, :] = y

        @pl.when(base != 0)
        def _():
            keep = lax.broadcasted_iota(jnp.int32, y.shape, 0) >= base
            y_ref[win, :] = jnp.where(keep, y, y_ref[win, :])
    yield


def _mixer_kernel(xp_ref, xn_ref, cos_ref, sin_ref, win_ref, wout_ref, ng_ref, rg_ref, hgg_ref, lb_ref, fg_ref,
                  xs_ref, cos_s_ref, sin_s_ref, dret_in, dhg_in,
                  y_ref, sret_ref, shg_ref, ys_ref, dret_out, dhg_out,
                  pa_ref, pb_ref, rq_ref, rk_ref, rdec_ref, *dec,
                  heads, dh, tile, chunk, pairs, per_step):
    step = pl.program_id(0)
    pair = step % pairs
    rows_a, rows_b = slice(0, tile), slice(tile, 2 * tile)

    @pl.when(pair == 0)
    def _():
        sret_ref[...] = jnp.zeros_like(sret_ref)
        shg_ref[...] = jnp.zeros_like(shg_ref)

    @pl.when(step == 0)
    def _():
        _retention_tables(rq_ref, rk_ref, rdec_ref, heads, tile)
        _decode_project(xs_ref, cos_s_ref, sin_s_ref, win_ref, ng_ref, lb_ref, dec, heads=heads, dh=dh)
        for _ in _proj_stages(xp_ref, rows_a, ng_ref, win_ref, pa_ref):
            pass

    mixer = functools.partial(
        _mixer, cos_ref=cos_ref, sin_ref=sin_ref, wout_ref=wout_ref, rg_ref=rg_ref, hgg_ref=hgg_ref,
        lb_ref=lb_ref, fg_ref=fg_ref, y_ref=y_ref, sret_ref=sret_ref, shg_ref=shg_ref,
        rq_ref=rq_ref, rk_ref=rk_ref, rdec_ref=rdec_ref, heads=heads, dh=dh, tile=tile, chunk=chunk)
    decode = _decode_stages(step, xs_ref, wout_ref, rg_ref, hgg_ref, fg_ref, dret_in, dhg_in, ys_ref,
                            dret_out, dhg_out, dec, heads=heads, dh=dh, per_step=per_step)
    every = max(1, 2 * (1 + PROJ_GROUPS) // (per_step * heads + 1))
    mixer(xp_ref, rows_a, pa_ref,
          _filler(_zip_stages(_proj_stages(xp_ref, rows_b, ng_ref, win_ref, pb_ref), decode, every)))
    mixer(xp_ref, rows_b, pb_ref,
          _filler(_zip_stages(_proj_stages(xn_ref, rows_a, ng_ref, win_ref, pa_ref), decode, every)))
    for _ in decode:
        pass


def _mixer_call(x, cos, sin, win, wout, ng, rg, hgg, lb, fg, xs, cos_s, sin_s, dret, dhg):
    bsz, seq, d = x.shape
    nb, heads, dh, _ = dret.shape
    tile = min(PROMPT_TILE, seq // 2)
    chunk = min(HGRN_CHUNK, tile)
    assert seq % (2 * tile) == 0 and tile % chunk == 0 and chunk & (chunk - 1) == 0 and chunk >= 16
    gw = heads * dh
    pairs = seq // (2 * tile)
    steps = bsz * pairs
    per_step = nb // steps
    assert nb == per_step * steps and nb % 128 == 0

    def nxt(s):
        s1 = jnp.minimum(s + 1, steps - 1)
        return (s1 // pairs, 2 * (s1 % pairs), 0)

    const = lambda *shape: pl.BlockSpec(shape, lambda s: (0,) * len(shape))
    state_spec = pl.BlockSpec((1, heads, dh, dh), lambda s: (s // pairs, 0, 0, 0))
    dstate_spec = pl.BlockSpec((per_step, heads, dh, dh), lambda s: (s, 0, 0, 0))
    pair_spec = pl.BlockSpec((1, 2 * tile, d), lambda s: (s // pairs, s % pairs, 0))
    rope_spec = pl.BlockSpec((2 * tile, dh), lambda s: (s % pairs, 0))
    body = functools.partial(_mixer_kernel, heads=heads, dh=dh, tile=tile, chunk=chunk, pairs=pairs,
                             per_step=per_step)
    row_scratch = pltpu.VMEM((nb, gw), _F32)
    col_scratch = pltpu.VMEM((gw, nb), _F32)
    return pl.pallas_call(
        body,
        grid=(steps,),
        in_specs=[
            pair_spec, pl.BlockSpec((1, tile, d), nxt), rope_spec, rope_spec,
            const(d // 2, 8 * gw), const(gw, d), const(1, d), const(1, gw), const(1, gw),
            const(lb.shape[0], gw), const(1, d),
            const(nb, d), const(1, dh), const(1, dh), dstate_spec, dstate_spec,
        ],
        out_specs=[pair_spec, state_spec, state_spec, const(nb, d), dstate_spec, dstate_spec],
        out_shape=[
            jax.ShapeDtypeStruct((bsz, seq, d), _F32),
            jax.ShapeDtypeStruct((bsz, heads, dh, dh), _F32),
            jax.ShapeDtypeStruct((bsz, heads, dh, dh), _F32),
            jax.ShapeDtypeStruct((nb, d), _F32),
            jax.ShapeDtypeStruct(dret.shape, _F32),
            jax.ShapeDtypeStruct(dhg.shape, _F32),
        ],
        scratch_shapes=[pltpu.VMEM((tile, 8 * gw), _F32), pltpu.VMEM((tile, 8 * gw), _F32),
                        pltpu.VMEM((heads, tile, dh), _F32), pltpu.VMEM((heads, tile, dh), _F32),
                        pltpu.VMEM((heads, tile, tile), _F32)]
                       + [col_scratch, col_scratch, row_scratch, row_scratch, col_scratch, col_scratch,
                          row_scratch, row_scratch, col_scratch],
        compiler_params=pltpu.CompilerParams(
            dimension_semantics=("arbitrary",), vmem_limit_bytes=VMEM_LIMIT_BYTES),
        name="mixer_step",
    )(x, x, cos, sin, win, wout, ng, rg, hgg, lb, fg, xs, cos_s, sin_s, dret, dhg)


def _rope_tables(pos, dh):
    half = dh // 2
    lane = jnp.arange(dh)
    freqs = 1.0 / (ROPE_BASE ** ((lane % half).astype(_F32) / half))
    ang = pos[:, None] * freqs[None, :]
    return jnp.cos(ang), jnp.sin(ang) * jnp.where(lane < half, -1.0, 1.0)[None, :]


def kernel(x_prompt, x_sample, state_ret, state_hgrn, norm_g, w_in, ret_norm_g, hg_norm_g, hg_lb, w_out,
           final_norm_g):
    depth, _, heads, dh, _ = state_ret.shape
    assert depth == 1 and x_sample.shape[1] == 1
    lp = x_prompt.shape[1]
    win, wout = _pack_weights(w_in[0], w_out[0])
    ng, fg = norm_g[0][None, :], final_norm_g[None, :]
    rg, hgg = ret_norm_g[0][None, :], hg_norm_g[0][None, :]

    cos_p, sin_p = _rope_tables(jnp.arange(lp, dtype=_F32), dh)
    cos_s, sin_s = _rope_tables(PAST_LEN + jnp.arange(1, dtype=_F32), dh)
    y_p, ret_p, hg_p, y_s, ret_s, hg_s = _mixer_call(
        x_prompt, cos_p, sin_p, win, wout, ng, rg, hgg, hg_lb, fg,
        x_sample[:, 0, :], cos_s, sin_s, state_ret[0], state_hgrn[0])
    return (y_p, y_s[:, None, :], ret_p[None], hg_p[None], ret_s[None], hg_s[None])
```

```python
import functools
import math

import jax
import jax.numpy as jnp
from jax import lax
from jax.experimental import pallas as pl
from jax.experimental.pallas import tpu as pltpu

PAST_LEN = 16384
ROPE_BASE = 10000.0
NORM_EPS = 1e-6

PROMPT_TILE = 256
HGRN_CHUNK = 128
PROJ_GROUPS = 16
PACK_ROWS = 256
VMEM_LIMIT_BYTES = 56 * 1024 * 1024

_F32 = jnp.float32
_BF16 = jnp.bfloat16


def _rms(x, g):
    return x * lax.rsqrt(jnp.mean(x * x, axis=-1, keepdims=True) + NORM_EPS) * g


def _sigmoid(x):
    return 1.0 / (1.0 + jnp.exp(-x))


def _silu(x):
    return x * _sigmoid(x)


def _dot(a, b):
    return jnp.dot(a, b, preferred_element_type=_F32)


def _dot_nt(a, b):
    return lax.dot_general(a, b, (((1,), (1,)), ((), ())), preferred_element_type=_F32)


def _dot_tn(a, b):
    return lax.dot_general(a, b, (((0,), (0,)), ((), ())), preferred_element_type=_F32)


def _weight(w_ref, cols=slice(None)):
    return pltpu.bitcast(w_ref[:, cols], _BF16)


def _pack_kernel(*refs):
    n = len(refs) // 2
    for w_ref, o_ref in zip(refs[:n], refs[n:]):
        o_ref[...] = pltpu.bitcast(w_ref[...].astype(_BF16), jnp.int32)


def _pack_weights(*ws):
    steps = max(w.shape[0] for w in ws) // PACK_ROWS
    rows = [w.shape[0] // steps for w in ws]
    assert all(w.shape[0] == r * steps and r % 16 == 0 for w, r in zip(ws, rows))
    return pl.pallas_call(
        _pack_kernel,
        grid=(steps,),
        in_specs=[pl.BlockSpec((r, w.shape[1]), lambda i: (i, 0)) for w, r in zip(ws, rows)],
        out_specs=[pl.BlockSpec((r // 2, w.shape[1]), lambda i: (i, 0)) for w, r in zip(ws, rows)],
        out_shape=[jax.ShapeDtypeStruct((w.shape[0] // 2, w.shape[1]), jnp.int32) for w in ws],
        compiler_params=pltpu.CompilerParams(dimension_semantics=("arbitrary",)),
        name="pack_weights",
    )(*ws)


def _rotate(x, cos, sin_signed, half):
    return x * cos + pltpu.roll(x, half, 1) * sin_signed


def _lower_bound(lb_ref):
    a = lb_ref[...]
    m = jnp.max(a, axis=0, keepdims=True)
    e = jnp.exp(a - m)
    return e[0:1, :] / jnp.sum(e, axis=0, keepdims=True)


def _forget_gate(hf, lbv):
    z = jnp.exp(-jnp.abs(hf))
    r = 1.0 / (1.0 + z)
    zr = z * r
    pos = hf >= 0.0
    sig_p = jnp.where(pos, r, zr)
    sig_n = jnp.where(pos, zr, r)
    one_m = 1.0 - lbv
    return lbv + one_m * sig_p, one_m * sig_n


def _decay_levels(qa, kin, f):
    c, w = f.shape
    n_lev = c.bit_length() - 1
    rowi = lax.broadcasted_iota(jnp.int32, (c, w), 0)
    tot, q, k = f, qa * f, kin
    for lev in range(3):
        half = 1 << lev
        up = (rowi & half) != 0
        yield jnp.where(up, q, 0.0).astype(_BF16), jnp.where(up, 0.0, k).T.astype(_BF16)
        sib = jnp.where(up, pltpu.roll(tot, half, 0), pltpu.roll(tot, c - half, 0))
        q = jnp.where(up, q * sib, q)
        k = jnp.where(up, k, k * sib)
        tot = tot * sib
    q_b = [q[i * 8:(i + 1) * 8] for i in range(c // 8)]
    k_b = [k[i * 8:(i + 1) * 8] for i in range(c // 8)]
    tot_b = [tot[i * 8:i * 8 + 1] for i in range(c // 8)]
    for lev in range(3, n_lev):
        half = 1 << lev
        zero = jnp.zeros((half, w), _F32)
        yield (jnp.concatenate([q_b[i] if i % 2 else zero for i in range(len(q_b))], axis=0).astype(_BF16),
               jnp.concatenate([zero if i % 2 else k_b[i] for i in range(len(k_b))], axis=0).T.astype(_BF16))
        q_b = [jnp.concatenate([q_b[i], q_b[i + 1] * tot_b[i]], axis=0) for i in range(0, len(q_b), 2)]
        k_b = [jnp.concatenate([k_b[i] * tot_b[i + 1], k_b[i + 1]], axis=0) for i in range(0, len(k_b), 2)]
        tot_b = [tot_b[i] * tot_b[i + 1] for i in range(0, len(tot_b), 2)]
    yield q_b[0].astype(_BF16), k_b[0].astype(_BF16), tot_b[0]


def _proj_stages(x_ref, rows, ng_ref, win_ref, p_ref):
    h = _rms(x_ref[0, rows, :], ng_ref[...]).astype(_BF16)
    yield
    width = p_ref.shape[1] // PROJ_GROUPS
    for sec in range(PROJ_GROUPS):
        cols = slice(sec * width, (sec + 1) * width)
        p_ref[:, cols] = _dot(h, _weight(win_ref, cols))
        yield


def _zip_stages(main, side, every):
    for n, _ in enumerate(main, 1):
        if n % every == 0:
            next(side, None)
        yield


def _filler(stages):
    def fill(n):
        if n is None:
            for _ in stages:
                pass
        else:
            for _ in range(n):
                next(stages, None)
    return fill


def _log_gamma(hh):
    return math.log1p(-(2.0 ** (-5 - hh)))


def _retention_tables(rq_ref, rk_ref, rdec_ref, heads, tile):
    row = lax.broadcasted_iota(jnp.int32, rq_ref.shape[1:], 0).astype(_F32)
    ti = lax.broadcasted_iota(jnp.int32, (tile, tile), 0)
    si = lax.broadcasted_iota(jnp.int32, (tile, tile), 1)
    lag = jnp.maximum(ti - si, 0).astype(_F32)
    for hh in range(heads):
        lg = _log_gamma(hh)
        rq_ref[hh] = jnp.exp(lg * (row + 1.0))
        rk_ref[hh] = jnp.exp(lg * (tile - 1.0 - row))
        rdec_ref[hh] = jnp.where(ti >= si, jnp.exp(lg * lag), 0.0)


def _mixer(x_ref, rows, p_ref, fill, *, cos_ref, sin_ref, wout_ref, rg_ref, hgg_ref, lb_ref, fg_ref,
           y_ref, sret_ref, shg_ref, rq_ref, rk_ref, rdec_ref, heads, dh, tile, chunk):
    gw = heads * dh

    def p(sec, hh=None, rs=slice(None)):
        if hh is None:
            return p_ref[rs, sec * gw:(sec + 1) * gw]
        return p_ref[rs, sec * gw + hh * dh:sec * gw + (hh + 1) * dh]

    o_all = [None] * (2 * heads)

    cos, sin = cos_ref[rows, :], sin_ref[rows, :]

    def ret_issue(hh):
        q = _rotate(p(0, hh), cos, sin, dh // 2).astype(_BF16)
        kt = (_rotate(p(1, hh), cos, sin, dh // 2) * (dh ** -0.5)).astype(_BF16).T
        v = p(2, hh)
        s_old = sret_ref[0, hh]
        qk_qs = _dot(q, jnp.concatenate([kt, s_old.astype(_BF16)], axis=1))
        sret_ref[0, hh] = math.exp(tile * _log_gamma(hh)) * s_old + _dot(kt, (v * rk_ref[hh]).astype(_BF16))
        return hh, qk_qs, v

    def ret_finish(hh, qk_qs, v):
        sl = slice(hh * dh, (hh + 1) * dh)
        a = qk_qs[:, :tile] * rdec_ref[hh]
        o = qk_qs[:, tile:] * rq_ref[hh] + _dot(a.astype(_BF16), v.astype(_BF16))
        o_all[hh] = (_rms(o, rg_ref[:, sl]) * _silu(p(3, hh))).astype(_BF16)

    lbv = _lower_bound(lb_ref)
    ci = lax.broadcasted_iota(jnp.int32, (chunk, chunk), 0)
    cj = lax.broadcasted_iota(jnp.int32, (chunk, chunk), 1)
    cxor = ci ^ cj
    n_lev = chunk.bit_length() - 1
    diag = ci == cj
    lev_mask = [(ci > cj) & ((cxor >> lev) == 1) for lev in range(n_lev)]
    n_chunks = tile // chunk
    o_hg_chunks = [[None] * n_chunks for _ in range(heads)]

    def hg_issue(c, hh):
        rs = slice(c * chunk, (c + 1) * chunk)
        sl = slice(hh * dh, (hh + 1) * dh)
        f, gk = _forget_gate(p(5, hh, rs), lbv[:, sl])
        gq = _silu(p(4, hh, rs))
        gv = p(6, hh, rs).astype(_BF16)
        a = jnp.where(diag, jnp.sum(gq * gk, axis=-1, keepdims=True), 0.0)
        *levels, (q_top, k_top, f_tot) = _decay_levels(gq, gk, f)
        for lev, (q_lv, k_lv) in enumerate(levels):
            a = jnp.where(lev_mask[lev], _dot(q_lv, k_lv), a)
        s_old = shg_ref[0, hh]
        o_inter = _dot(q_top, s_old.astype(_BF16))
        f_col = jnp.broadcast_to(f_tot, (dh, dh)).T
        shg_ref[0, hh] = s_old * f_col + _dot_tn(k_top, gv)
        return c, hh, a.astype(_BF16), o_inter, gv

    def hg_finish(c, hh, a, o_inter, gv):
        o_hg_chunks[hh][c] = o_inter + _dot(a, gv)

    fill(1)
    pend = None
    for hh in range(heads):
        ctx = ret_issue(hh)
        fill(1)
        if pend is not None:
            ret_finish(*pend)
        pend = ctx
    fill(1)
    ret_finish(*pend)
    pend = None
    for c in range(n_chunks):
        for hh in range(heads):
            ctx = hg_issue(c, hh)
            fill(1)
            if pend is not None:
                hg_finish(*pend)
            pend = ctx
    fill(2)
    hg_finish(*pend)
    fill(1)
    for hh in range(heads):
        sl = slice(hh * dh, (hh + 1) * dh)
        o = jnp.concatenate(o_hg_chunks[hh], axis=0)
        o_all[heads + hh] = (_rms(o, hgg_ref[:, sl]) * _silu(p(7, hh))).astype(_BF16)
    fill(None)
    o_cat = jnp.concatenate(o_all, axis=-1)
    y = x_ref[0, rows, :] + _dot(o_cat, _weight(wout_ref))
    y_ref[0, rows, :] = _rms(y, fg_ref[...])


def _decode_project(x_ref, cos_ref, sin_ref, win_ref, ng_ref, lb_ref, dec, *, heads, dh):
    rqt_s, rkt_s, rv_s, rgate_s, hqt_s, hkt_s, hi_s, hgate_s, hft_s = dec
    gw = heads * dh
    h = _rms(x_ref[...], ng_ref[...]).astype(_BF16)

    def proj(sec):
        return _dot(h, _weight(win_ref, slice(sec * gw, (sec + 1) * gw)))

    cos, sin = cos_ref[...], sin_ref[...]

    def rot(pj):
        return jnp.concatenate(
            [_rotate(pj[:, hh * dh:(hh + 1) * dh], cos, sin, dh // 2) for hh in range(heads)], axis=-1)

    rqt_s[...] = rot(proj(0)).T
    rkt_s[...] = (rot(proj(1)) * (dh ** -0.5)).T
    rv_s[...] = proj(2)
    rgate_s[...] = _silu(proj(3))
    f, kin = _forget_gate(proj(5), _lower_bound(lb_ref))
    hqt_s[...] = _silu(proj(4)).T
    hkt_s[...] = kin.T
    hft_s[...] = f.T
    hi_s[...] = proj(6)
    hgate_s[...] = _silu(proj(7))


def _decode_stages(step, x_ref, wout_ref, rg_ref, hgg_ref, fg_ref, sret_in, shg_in, y_ref, sret_out, shg_out, dec,
                   *, heads, dh, per_step):
    rqt_s, rkt_s, rv_s, rgate_s, hqt_s, hkt_s, hi_s, hgate_s, hft_s = dec
    nb = x_ref.shape[0]
    wrows = max(per_step, 8)
    share = wrows // per_step
    assert share in (1, 2) and wrows % 8 == 0
    win = pl.ds(pl.multiple_of((step // share) * wrows, 8), wrows)
    base = (step % share) * per_step
    shift = (nb - step * per_step) % nb
    rolled = {}

    def col(ref_s, hh, bl):
        if id(ref_s) not in rolled:
            rolled[id(ref_s)] = pltpu.roll(ref_s[...], shift, 1)
        return jnp.broadcast_to(rolled[id(ref_s)][hh * dh:(hh + 1) * dh, bl:bl + 1], (dh, dh))

    def row_of(ref_s, sl, mine):
        return jnp.sum(jnp.where(mine, ref_s[win, sl], 0.0), axis=0, keepdims=True)

    rowi = lax.broadcasted_iota(jnp.int32, (wrows, dh), 0)
    o_ret = [jnp.zeros((wrows, dh), _F32)] * heads
    o_hg = [jnp.zeros((wrows, dh), _F32)] * heads
    for bl in range(per_step):
        mine = rowi == base + bl
        for hh in range(heads):
            sl = slice(hh * dh, (hh + 1) * dh)
            gamma = 1.0 - 2.0 ** (-5 - hh)
            s_new = gamma * sret_in[bl, hh] + col(rkt_s, hh, bl) * row_of(rv_s, sl, mine)
            sret_out[bl, hh] = s_new
            o = jnp.sum(s_new * col(rqt_s, hh, bl), axis=0, keepdims=True)
            o_ret[hh] = jnp.where(mine, o, o_ret[hh])
            g_new = col(hft_s, hh, bl) * shg_in[bl, hh] + col(hkt_s, hh, bl) * row_of(hi_s, sl, mine)
            shg_out[bl, hh] = g_new
            o = jnp.sum(g_new * col(hqt_s, hh, bl), axis=0, keepdims=True)
            o_hg[hh] = jnp.where(mine, o, o_hg[hh])
            yield

    def finish(o_rows, gain_ref, gate):
        return [_rms(o, gain_ref[:, hh * dh:(hh + 1) * dh]) * gate[:, hh * dh:(hh + 1) * dh]
                for hh, o in enumerate(o_rows)]

    o_cat = jnp.concatenate(
        finish(o_ret, rg_ref, rgate_s[win, :]) + finish(o_hg, hgg_ref, hgate_s[win, :]), axis=-1).astype(_BF16)
    y = _rms(x_ref[win, :] + _dot(o_cat, _weight(wout_ref)), fg_ref[...])
    if share == 1:
        y_ref[win, :] = y
    else:
        @pl.when(base == 0)
        def _():
            y_ref[win, :] = y

        @pl.when(base != 0)
        def _():
            keep = lax.broadcasted_iota(jnp.int32, y.shape, 0) >= base
            y_ref[win, :] = jnp.where(keep, y, y_ref[win, :])
    yield


def _mixer_kernel(xp_ref, xn_ref, cos_ref, sin_ref, win_ref, wout_ref, ng_ref, rg_ref, hgg_ref, lb_ref, fg_ref,
                  xs_ref, cos_s_ref, sin_s_ref, dret_in, dhg_in,
                  y_ref, sret_ref, shg_ref, ys_ref, dret_out, dhg_out,
                  pa_ref, pb_ref, rq_ref, rk_ref, rdec_ref, *dec,
                  heads, dh, tile, chunk, pairs, per_step):
    step = pl.program_id(0)
    pair = step % pairs
    rows_a, rows_b = slice(0, tile), slice(tile, 2 * tile)

    @pl.when(pair == 0)
    def _():
        sret_ref[...] = jnp.zeros_like(sret_ref)
        shg_ref[...] = jnp.zeros_like(shg_ref)

    @pl.when(step == 0)
    def _():
        _retention_tables(rq_ref, rk_ref, rdec_ref, heads, tile)
        _decode_project(xs_ref, cos_s_ref, sin_s_ref, win_ref, ng_ref, lb_ref, dec, heads=heads, dh=dh)
        for _ in _proj_stages(xp_ref, rows_a, ng_ref, win_ref, pa_ref):
            pass

    mixer = functools.partial(
        _mixer, cos_ref=cos_ref, sin_ref=sin_ref, wout_ref=wout_ref, rg_ref=rg_ref, hgg_ref=hgg_ref,
        lb_ref=lb_ref, fg_ref=fg_ref, y_ref=y_ref, sret_ref=sret_ref, shg_ref=shg_ref,
        rq_ref=rq_ref, rk_ref=rk_ref, rdec_ref=rdec_ref, heads=heads, dh=dh, tile=tile, chunk=chunk)
    decode = _decode_stages(step, xs_ref, wout_ref, rg_ref, hgg_ref, fg_ref, dret_in, dhg_in, ys_ref,
                            dret_out, dhg_out, dec, heads=heads, dh=dh, per_step=per_step)
    every = max(1, 2 * (1 + PROJ_GROUPS) // (per_step * heads + 1))
    mixer(xp_ref, rows_a, pa_ref,
          _filler(_zip_stages(_proj_stages(xp_ref, rows_b, ng_ref, win_ref, pb_ref), decode, every)))
    mixer(xp_ref, rows_b, pb_ref,
          _filler(_zip_stages(_proj_stages(xn_ref, rows_a, ng_ref, win_ref, pa_ref), decode, every)))
    for _ in decode:
        pass


def _mixer_call(x, cos, sin, win, wout, ng, rg, hgg, lb, fg, xs, cos_s, sin_s, dret, dhg):
    bsz, seq, d = x.shape
    nb, heads, dh, _ = dret.shape
    tile = min(PROMPT_TILE, seq // 2)
    chunk = min(HGRN_CHUNK, tile)
    assert seq % (2 * tile) == 0 and tile % chunk == 0 and chunk & (chunk - 1) == 0 and chunk >= 16
    gw = heads * dh
    pairs = seq // (2 * tile)
    steps = bsz * pairs
    per_step = nb // steps
    assert nb == per_step * steps and nb % 128 == 0

    def nxt(s):
        s1 = jnp.minimum(s + 1, steps - 1)
        return (s1 // pairs, 2 * (s1 % pairs), 0)

    const = lambda *shape: pl.BlockSpec(shape, lambda s: (0,) * len(shape))
    state_spec = pl.BlockSpec((1, heads, dh, dh), lambda s: (s // pairs, 0, 0, 0))
    dstate_spec = pl.BlockSpec((per_step, heads, dh, dh), lambda s: (s, 0, 0, 0))
    pair_spec = pl.BlockSpec((1, 2 * tile, d), lambda s: (s // pairs, s % pairs, 0))
    rope_spec = pl.BlockSpec((2 * tile, dh), lambda s: (s % pairs, 0))
    body = functools.partial(_mixer_kernel, heads=heads, dh=dh, tile=tile, chunk=chunk, pairs=pairs,
                             per_step=per_step)
    row_scratch = pltpu.VMEM((nb, gw), _F32)
    col_scratch = pltpu.VMEM((gw, nb), _F32)
    return pl.pallas_call(
        body,
        grid=(steps,),
        in_specs=[
            pair_spec, pl.BlockSpec((1, tile, d), nxt), rope_spec, rope_spec,
            const(d // 2, 8 * gw), const(gw, d), const(1, d), const(1, gw), const(1, gw),
            const(lb.shape[0], gw), const(1, d),
            const(nb, d), const(1, dh), const(1, dh), dstate_spec, dstate_spec,
        ],
        out_specs=[pair_spec, state_spec, state_spec, const(nb, d), dstate_spec, dstate_spec],
        out_shape=[
            jax.ShapeDtypeStruct((bsz, seq, d), _F32),
            jax.ShapeDtypeStruct((bsz, heads, dh, dh), _F32),
            jax.ShapeDtypeStruct((bsz, heads, dh, dh), _F32),
            jax.ShapeDtypeStruct((nb, d), _F32),
            jax.ShapeDtypeStruct(dret.shape, _F32),
            jax.ShapeDtypeStruct(dhg.shape, _F32),
        ],
        scratch_shapes=[pltpu.VMEM((tile, 8 * gw), _F32), pltpu.VMEM((tile, 8 * gw), _F32),
                        pltpu.VMEM((heads, tile, dh), _F32), pltpu.VMEM((heads, tile, dh), _F32),
                        pltpu.VMEM((heads, tile, tile), _F32)]
                       + [col_scratch, col_scratch, row_scratch, row_scratch, col_scratch, col_scratch,
                          row_scratch, row_scratch, col_scratch],
        compiler_params=pltpu.CompilerParams(
            dimension_semantics=("arbitrary",), vmem_limit_bytes=VMEM_LIMIT_BYTES),
        name="mixer_step",
    )(x, x, cos, sin, win, wout, ng, rg, hgg, lb, fg, xs, cos_s, sin_s, dret, dhg)


def _rope_tables(pos, dh):
    half = dh // 2
    lane = jnp.arange(dh)
    freqs = 1.0 / (ROPE_BASE ** ((lane % half).astype(_F32) / half))
    ang = pos[:, None] * freqs[None, :]
    return jnp.cos(ang), jnp.sin(ang) * jnp.where(lane < half, -1.0, 1.0)[None, :]


def kernel(x_prompt, x_sample, state_ret, state_hgrn, norm_g, w_in, ret_norm_g, hg_norm_g, hg_lb, w_out,
           final_norm_g):
    depth, _, heads, dh, _ = state_ret.shape
    assert depth == 1 and x_sample.shape[1] == 1
    lp = x_prompt.shape[1]
    win, wout = _pack_weights(w_in[0], w_out[0])
    ng, fg = norm_g[0][None, :], final_norm_g[None, :]
    rg, hgg = ret_norm_g[0][None, :], hg_norm_g[0][None, :]

    cos_p, sin_p = _rope_tables(jnp.arange(lp, dtype=_F32), dh)
    cos_s, sin_s = _rope_tables(PAST_LEN + jnp.arange(1, dtype=_F32), dh)
    y_p, ret_p, hg_p, y_s, ret_s, hg_s = _mixer_call(
        x_prompt, cos_p, sin_p, win, wout, ng, rg, hgg, hg_lb, fg,
        x_sample[:, 0, :], cos_s, sin_s, state_ret[0], state_hgrn[0])
    return (y_p, y_s[:, None, :], ret_p[None], hg_p[None], ret_s[None], hg_s[None])
```

```python
import functools
import math

import jax
import jax.numpy as jnp
from jax import lax
from jax.experimental import pallas as pl
from jax.experimental.pallas import tpu as pltpu

PAST_LEN = 16384
ROPE_BASE = 10000.0
NORM_EPS = 1e-6

PROMPT_TILE = 256
HGRN_CHUNK = 128
PROJ_GROUPS = 16
PACK_ROWS = 256
VMEM_LIMIT_BYTES = 56 * 1024 * 1024

_F32 = jnp.float32
_BF16 = jnp.bfloat16


def _rms(x, g):
    return x * lax.rsqrt(jnp.mean(x * x, axis=-1, keepdims=True) + NORM_EPS) * g


def _sigmoid(x):
    return 1.0 / (1.0 + jnp.exp(-x))


def _silu(x):
    return x * _sigmoid(x)


def _dot(a, b):
    return jnp.dot(a, b, preferred_element_type=_F32)


def _dot_nt(a, b):
    return lax.dot_general(a, b, (((1,), (1,)), ((), ())), preferred_element_type=_F32)


def _dot_tn(a, b):
    return lax.dot_general(a, b, (((0,), (0,)), ((), ())), preferred_element_type=_F32)


def _weight(w_ref, cols=slice(None)):
    return pltpu.bitcast(w_ref[:, cols], _BF16)


def _pack_kernel(*refs):
    n = len(refs) // 2
    for w_ref, o_ref in zip(refs[:n], refs[n:]):
        o_ref[...] = pltpu.bitcast(w_ref[...].astype(_BF16), jnp.int32)


def _pack_weights(*ws):
    steps = max(w.shape[0] for w in ws) // PACK_ROWS
    rows = [w.shape[0] // steps for w in ws]
    assert all(w.shape[0] == r * steps and r % 16 == 0 for w, r in zip(ws, rows))
    return pl.pallas_call(
        _pack_kernel,
        grid=(steps,),
        in_specs=[pl.BlockSpec((r, w.shape[1]), lambda i: (i, 0)) for w, r in zip(ws, rows)],
        out_specs=[pl.BlockSpec((r // 2, w.shape[1]), lambda i: (i, 0)) for w, r in zip(ws, rows)],
        out_shape=[jax.ShapeDtypeStruct((w.shape[0] // 2, w.shape[1]), jnp.int32) for w in ws],
        compiler_params=pltpu.CompilerParams(dimension_semantics=("arbitrary",)),
        name="pack_weights",
    )(*ws)


def _rotate(x, cos, sin_signed, half):
    return x * cos + pltpu.roll(x, half, 1) * sin_signed


def _lower_bound(lb_ref):
    a = lb_ref[...]
    m = jnp.max(a, axis=0, keepdims=True)
    e = jnp.exp(a - m)
    return e[0:1, :] / jnp.sum(e, axis=0, keepdims=True)


def _forget_gate(hf, lbv):
    z = jnp.exp(-jnp.abs(hf))
    r = 1.0 / (1.0 + z)
    zr = z * r
    pos = hf >= 0.0
    sig_p = jnp.where(pos, r, zr)
    sig_n = jnp.where(pos, zr, r)
    one_m = 1.0 - lbv
    return lbv + one_m * sig_p, one_m * sig_n


def _decay_levels(qa, kin, f):
    c, w = f.shape
    n_lev = c.bit_length() - 1
    rowi = lax.broadcasted_iota(jnp.int32, (c, w), 0)
    tot, q, k = f, qa * f, kin
    for lev in range(3):
        half = 1 << lev
        up = (rowi & half) != 0
        yield jnp.where(up, q, 0.0).astype(_BF16), jnp.where(up, 0.0, k).T.astype(_BF16)
        sib = jnp.where(up, pltpu.roll(tot, half, 0), pltpu.roll(tot, c - half, 0))
        q = jnp.where(up, q * sib, q)
        k = jnp.where(up, k, k * sib)
        tot = tot * sib
    q_b = [q[i * 8:(i + 1) * 8] for i in range(c // 8)]
    k_b = [k[i * 8:(i + 1) * 8] for i in range(c // 8)]
    tot_b = [tot[i * 8:i * 8 + 1] for i in range(c // 8)]
    for lev in range(3, n_lev):
        half = 1 << lev
        zero = jnp.zeros((half, w), _F32)
        yield (jnp.concatenate([q_b[i] if i % 2 else zero for i in range(len(q_b))], axis=0).astype(_BF16),
               jnp.concatenate([zero if i % 2 else k_b[i] for i in range(len(k_b))], axis=0).T.astype(_BF16))
        q_b = [jnp.concatenate([q_b[i], q_b[i + 1] * tot_b[i]], axis=0) for i in range(0, len(q_b), 2)]
        k_b = [jnp.concatenate([k_b[i] * tot_b[i + 1], k_b[i + 1]], axis=0) for i in range(0, len(k_b), 2)]
        tot_b = [tot_b[i] * tot_b[i + 1] for i in range(0, len(tot_b), 2)]
    yield q_b[0].astype(_BF16), k_b[0].astype(_BF16), tot_b[0]


def _proj_stages(x_ref, rows, ng_ref, win_ref, p_ref):
    h = _rms(x_ref[0, rows, :], ng_ref[...]).astype(_BF16)
    yield
    width = p_ref.shape[1] // PROJ_GROUPS
    for sec in range(PROJ_GROUPS):
        cols = slice(sec * width, (sec + 1) * width)
        p_ref[:, cols] = _dot(h, _weight(win_ref, cols))
        yield


def _zip_stages(main, side, every):
    for n, _ in enumerate(main, 1):
        if n % every == 0:
            next(side, None)
        yield


def _filler(stages):
    def fill(n):
        if n is None:
            for _ in stages:
                pass
        else:
            for _ in range(n):
                next(stages, None)
    return fill


def _log_gamma(hh):
    return math.log1p(-(2.0 ** (-5 - hh)))


def _retention_tables(rq_ref, rk_ref, rdec_ref, heads, tile):
    row = lax.broadcasted_iota(jnp.int32, rq_ref.shape[1:], 0).astype(_F32)
    ti = lax.broadcasted_iota(jnp.int32, (tile, tile), 0)
    si = lax.broadcasted_iota(jnp.int32, (tile, tile), 1)
    lag = jnp.maximum(ti - si, 0).astype(_F32)
    for hh in range(heads):
        lg = _log_gamma(hh)
        rq_ref[hh] = jnp.exp(lg * (row + 1.0))
        rk_ref[hh] = jnp.exp(lg * (tile - 1.0 - row))
        rdec_ref[hh] = jnp.where(ti >= si, jnp.exp(lg * lag), 0.0)


def _mixer(x_ref, rows, p_ref, fill, *, cos_ref, sin_ref, wout_ref, rg_ref, hgg_ref, lb_ref, fg_ref,
           y_ref, sret_ref, shg_ref, rq_ref, rk_ref, rdec_ref, heads, dh, tile, chunk):
    gw = heads * dh

    def p(sec, hh=None, rs=slice(None)):
        if hh is None:
            return p_ref[rs, sec * gw:(sec + 1) * gw]
        return p_ref[rs, sec * gw + hh * dh:sec * gw + (hh + 1) * dh]

    o_all = [None] * (2 * heads)

    cos, sin = cos_ref[rows, :], sin_ref[rows, :]

    def ret_issue(hh):
        q = _rotate(p(0, hh), cos, sin, dh // 2).astype(_BF16)
        kt = (_rotate(p(1, hh), cos, sin, dh // 2) * (dh ** -0.5)).astype(_BF16).T
        v = p(2, hh)
        s_old = sret_ref[0, hh]
        qk_qs = _dot(q, jnp.concatenate([kt, s_old.astype(_BF16)], axis=1))
        sret_ref[0, hh] = math.exp(tile * _log_gamma(hh)) * s_old + _dot(kt, (v * rk_ref[hh]).astype(_BF16))
        return hh, qk_qs, v

    def ret_finish(hh, qk_qs, v):
        sl = slice(hh * dh, (hh + 1) * dh)
        a = qk_qs[:, :tile] * rdec_ref[hh]
        o = qk_qs[:, tile:] * rq_ref[hh] + _dot(a.astype(_BF16), v.astype(_BF16))
        o_all[hh] = (_rms(o, rg_ref[:, sl]) * _silu(p(3, hh))).astype(_BF16)

    lbv = _lower_bound(lb_ref)
    ci = lax.broadcasted_iota(jnp.int32, (chunk, chunk), 0)
    cj = lax.broadcasted_iota(jnp.int32, (chunk, chunk), 1)
    cxor = ci ^ cj
    n_lev = chunk.bit_length() - 1
    diag = ci == cj
    lev_mask = [(ci > cj) & ((cxor >> lev) == 1) for lev in range(n_lev)]
    n_chunks = tile // chunk
    o_hg_chunks = [[None] * n_chunks for _ in range(heads)]

    def hg_issue(c, hh):
        rs = slice(c * chunk, (c + 1) * chunk)
        sl = slice(hh * dh, (hh + 1) * dh)
        f, gk = _forget_gate(p(5, hh, rs), lbv[:, sl])
        gq = _silu(p(4, hh, rs))
        gv = p(6, hh, rs).astype(_BF16)
        a = jnp.where(diag, jnp.sum(gq * gk, axis=-1, keepdims=True), 0.0).astype(_BF16)
        *levels, (q_top, k_top, f_tot) = _decay_levels(gq, gk, f)
        for lev, (q_lv, k_lv) in enumerate(levels):
            a = jnp.where(lev_mask[lev], _dot(q_lv, k_lv).astype(_BF16), a)
        s_old = shg_ref[0, hh]
        o_inter = _dot(q_top, s_old.astype(_BF16))
        f_col = jnp.broadcast_to(f_tot, (dh, dh)).T
        shg_ref[0, hh] = s_old * f_col + _dot_tn(k_top, gv)
        return c, hh, a, o_inter, gv

    def hg_finish(c, hh, a, o_inter, gv):
        o_hg_chunks[hh][c] = o_inter + _dot(a, gv)

    pend = None
    for hh in range(heads):
        ctx = ret_issue(hh)
        fill(1)
        if pend is not None:
            ret_finish(*pend)
        pend = ctx
    fill(2)
    ret_finish(*pend)
    pend = None
    for c in range(n_chunks):
        for hh in range(heads):
            ctx = hg_issue(c, hh)
            fill(1)
            if pend is not None:
                hg_finish(*pend)
            pend = ctx
    fill(2)
    hg_finish(*pend)
    fill(1)
    for hh in range(heads):
        sl = slice(hh * dh, (hh + 1) * dh)
        o = jnp.concatenate(o_hg_chunks[hh], axis=0)
        o_all[heads + hh] = (_rms(o, hgg_ref[:, sl]) * _silu(p(7, hh))).astype(_BF16)
    fill(None)
    o_cat = jnp.concatenate(o_all, axis=-1)
    y = x_ref[0, rows, :] + _dot(o_cat, _weight(wout_ref))
    y_ref[0, rows, :] = _rms(y, fg_ref[...])


def _decode_project(x_ref, cos_ref, sin_ref, win_ref, ng_ref, lb_ref, dec, *, heads, dh):
    rqt_s, rkt_s, rv_s, rgate_s, hqt_s, hkt_s, hi_s, hgate_s, hft_s = dec
    gw = heads * dh
    h = _rms(x_ref[...], ng_ref[...]).astype(_BF16)

    def proj(sec):
        return _dot(h, _weight(win_ref, slice(sec * gw, (sec + 1) * gw)))

    cos, sin = cos_ref[...], sin_ref[...]

    def rot(pj):
        return jnp.concatenate(
            [_rotate(pj[:, hh * dh:(hh + 1) * dh], cos, sin, dh // 2) for hh in range(heads)], axis=-1)

    rqt_s[...] = rot(proj(0)).T
    rkt_s[...] = (rot(proj(1)) * (dh ** -0.5)).T
    rv_s[...] = proj(2)
    rgate_s[...] = _silu(proj(3))
    f, kin = _forget_gate(proj(5), _lower_bound(lb_ref))
    hqt_s[...] = _silu(proj(4)).T
    hkt_s[...] = kin.T
    hft_s[...] = f.T
    hi_s[...] = proj(6)
    hgate_s[...] = _silu(proj(7))


def _decode_stages(step, x_ref, wout_ref, rg_ref, hgg_ref, fg_ref, sret_in, shg_in, y_ref, sret_out, shg_out, dec,
                   *, heads, dh, per_step):
    rqt_s, rkt_s, rv_s, rgate_s, hqt_s, hkt_s, hi_s, hgate_s, hft_s = dec
    nb = x_ref.shape[0]
    wrows = max(per_step, 8)
    share = wrows // per_step
    assert share in (1, 2) and wrows % 8 == 0
    win = pl.ds(pl.multiple_of((step // share) * wrows, 8), wrows)
    base = (step % share) * per_step
    shift = (nb - step * per_step) % nb
    rolled = {}

    def col(ref_s, hh, bl):
        if id(ref_s) not in rolled:
            rolled[id(ref_s)] = pltpu.roll(ref_s[...], shift, 1)
        return jnp.broadcast_to(rolled[id(ref_s)][hh * dh:(hh + 1) * dh, bl:bl + 1], (dh, dh))

    def row_of(ref_s, sl, mine):
        return jnp.sum(jnp.where(mine, ref_s[win, sl], 0.0), axis=0, keepdims=True)

    rowi = lax.broadcasted_iota(jnp.int32, (wrows, dh), 0)
    o_ret = [jnp.zeros((wrows, dh), _F32)] * heads
    o_hg = [jnp.zeros((wrows, dh), _F32)] * heads
    for bl in range(per_step):
        mine = rowi == base + bl
        for hh in range(heads):
            sl = slice(hh * dh, (hh + 1) * dh)
            gamma = 1.0 - 2.0 ** (-5 - hh)
            s_new = gamma * sret_in[bl, hh] + col(rkt_s, hh, bl) * row_of(rv_s, sl, mine)
            sret_out[bl, hh] = s_new
            o = jnp.sum(s_new * col(rqt_s, hh, bl), axis=0, keepdims=True)
            o_ret[hh] = jnp.where(mine, o, o_ret[hh])
            g_new = col(hft_s, hh, bl) * shg_in[bl, hh] + col(hkt_s, hh, bl) * row_of(hi_s, sl, mine)
            shg_out[bl, hh] = g_new
            o = jnp.sum(g_new * col(hqt_s, hh, bl), axis=0, keepdims=True)
            o_hg[hh] = jnp.where(mine, o, o_hg[hh])
            yield

    def finish(o_rows, gain_ref, gate):
        return [_rms(o, gain_ref[:, hh * dh:(hh + 1) * dh]) * gate[:, hh * dh:(hh + 1) * dh]
                for hh, o in enumerate(o_rows)]

    o_cat = jnp.concatenate(
        finish(o_ret, rg_ref, rgate_s[win, :]) + finish(o_hg, hgg_ref, hgate_s[win, :]), axis=-1).astype(_BF16)
    y = _rms(x_ref[win, :] + _dot(o_cat, _weight(wout_ref)), fg_ref[...])
    if share == 1:
        y_ref[win, :] = y
    else:
        @pl.when(base == 0)
        def _():
            y_ref[win, :] = y

        @pl.when(base != 0)
        def _():
            keep = lax.broadcasted_iota(jnp.int32, y.shape, 0) >= base
            y_ref[win, :] = jnp.where(keep, y, y_ref[win, :])
    yield


def _mixer_kernel(xp_ref, xn_ref, cos_ref, sin_ref, win_ref, wout_ref, ng_ref, rg_ref, hgg_ref, lb_ref, fg_ref,
                  xs_ref, cos_s_ref, sin_s_ref, dret_in, dhg_in,
                  y_ref, sret_ref, shg_ref, ys_ref, dret_out, dhg_out,
                  pa_ref, pb_ref, rq_ref, rk_ref, rdec_ref, *dec,
                  heads, dh, tile, chunk, pairs, per_step):
    step = pl.program_id(0)
    pair = step % pairs
    rows_a, rows_b = slice(0, tile), slice(tile, 2 * tile)

    @pl.when(pair == 0)
    def _():
        sret_ref[...] = jnp.zeros_like(sret_ref)
        shg_ref[...] = jnp.zeros_like(shg_ref)

    @pl.when(step == 0)
    def _():
        _retention_tables(rq_ref, rk_ref, rdec_ref, heads, tile)
        _decode_project(xs_ref, cos_s_ref, sin_s_ref, win_ref, ng_ref, lb_ref, dec, heads=heads, dh=dh)
        for _ in _proj_stages(xp_ref, rows_a, ng_ref, win_ref, pa_ref):
            pass

    mixer = functools.partial(
        _mixer, cos_ref=cos_ref, sin_ref=sin_ref, wout_ref=wout_ref, rg_ref=rg_ref, hgg_ref=hgg_ref,
        lb_ref=lb_ref, fg_ref=fg_ref, y_ref=y_ref, sret_ref=sret_ref, shg_ref=shg_ref,
        rq_ref=rq_ref, rk_ref=rk_ref, rdec_ref=rdec_ref, heads=heads, dh=dh, tile=tile, chunk=chunk)
    decode = _decode_stages(step, xs_ref, wout_ref, rg_ref, hgg_ref, fg_ref, dret_in, dhg_in, ys_ref,
                            dret_out, dhg_out, dec, heads=heads, dh=dh, per_step=per_step)
    every = max(1, 2 * (1 + PROJ_GROUPS) // (per_step * heads + 1))
    mixer(xp_ref, rows_a, pa_ref,
          _filler(_zip_stages(_proj_stages(xp_ref, rows_b, ng_ref, win_ref, pb_ref), decode, every)))
    mixer(xp_ref, rows_b, pb_ref,
          _filler(_zip_stages(_proj_stages(xn_ref, rows_a, ng_ref, win_ref, pa_ref), decode, every)))
    for _ in decode:
        pass


def _mixer_call(x, cos, sin, win, wout, ng, rg, hgg, lb, fg, xs, cos_s, sin_s, dret, dhg):
    bsz, seq, d = x.shape
    nb, heads, dh, _ = dret.shape
    tile = min(PROMPT_TILE, seq // 2)
    chunk = min(HGRN_CHUNK, tile)
    assert seq % (2 * tile) == 0 and tile % chunk == 0 and chunk & (chunk - 1) == 0 and chunk >= 16
    gw = heads * dh
    pairs = seq // (2 * tile)
    steps = bsz * pairs
    per_step = nb // steps
    assert nb == per_step * steps and nb % 128 == 0

    def nxt(s):
        s1 = jnp.minimum(s + 1, steps - 1)
        return (s1 // pairs, 2 * (s1 % pairs), 0)

    const = lambda *shape: pl.BlockSpec(shape, lambda s: (0,) * len(shape))
    state_spec = pl.BlockSpec((1, heads, dh, dh), lambda s: (s // pairs, 0, 0, 0))
    dstate_spec = pl.BlockSpec((per_step, heads, dh, dh), lambda s: (s, 0, 0, 0))
    pair_spec = pl.BlockSpec((1, 2 * tile, d), lambda s: (s // pairs, s % pairs, 0))
    rope_spec = pl.BlockSpec((2 * tile, dh), lambda s: (s % pairs, 0))
    body = functools.partial(_mixer_kernel, heads=heads, dh=dh, tile=tile, chunk=chunk, pairs=pairs,
                             per_step=per_step)
    row_scratch = pltpu.VMEM((nb, gw), _F32)
    col_scratch = pltpu.VMEM((gw, nb), _F32)
    return pl.pallas_call(
        body,
        grid=(steps,),
        in_specs=[
            pair_spec, pl.BlockSpec((1, tile, d), nxt), rope_spec, rope_spec,
            const(d // 2, 8 * gw), const(gw, d), const(1, d), const(1, gw), const(1, gw),
            const(lb.shape[0], gw), const(1, d),
            const(nb, d), const(1, dh), const(1, dh), dstate_spec, dstate_spec,
        ],
        out_specs=[pair_spec, state_spec, state_spec, const(nb, d), dstate_spec, dstate_spec],
        out_shape=[
            jax.ShapeDtypeStruct((bsz, seq, d), _F32),
            jax.ShapeDtypeStruct((bsz, heads, dh, dh), _F32),
            jax.ShapeDtypeStruct((bsz, heads, dh, dh), _F32),
            jax.ShapeDtypeStruct((nb, d), _F32),
            jax.ShapeDtypeStruct(dret.shape, _F32),
            jax.ShapeDtypeStruct(dhg.shape, _F32),
        ],
        scratch_shapes=[pltpu.VMEM((tile, 8 * gw), _F32), pltpu.VMEM((tile, 8 * gw), _F32),
                        pltpu.VMEM((heads, tile, dh), _F32), pltpu.VMEM((heads, tile, dh), _F32),
                        pltpu.VMEM((heads, tile, tile), _F32)]
                       + [col_scratch, col_scratch, row_scratch, row_scratch, col_scratch, col_scratch,
                          row_scratch, row_scratch, col_scratch],
        compiler_params=pltpu.CompilerParams(
            dimension_semantics=("arbitrary",), vmem_limit_bytes=VMEM_LIMIT_BYTES),
        name="mixer_step",
    )(x, x, cos, sin, win, wout, ng, rg, hgg, lb, fg, xs, cos_s, sin_s, dret, dhg)


def _rope_tables(pos, dh):
    half = dh // 2
    lane = jnp.arange(dh)
    freqs = 1.0 / (ROPE_BASE ** ((lane % half).astype(_F32) / half))
    ang = pos[:, None] * freqs[None, :]
    return jnp.cos(ang), jnp.sin(ang) * jnp.where(lane < half, -1.0, 1.0)[None, :]


def kernel(x_prompt, x_sample, state_ret, state_hgrn, norm_g, w_in, ret_norm_g, hg_norm_g, hg_lb, w_out,
           final_norm_g):
    depth, _, heads, dh, _ = state_ret.shape
    assert depth == 1 and x_sample.shape[1] == 1
    lp = x_prompt.shape[1]
    win, wout = _pack_weights(w_in[0], w_out[0])
    ng, fg = norm_g[0][None, :], final_norm_g[None, :]
    rg, hgg = ret_norm_g[0][None, :], hg_norm_g[0][None, :]

    cos_p, sin_p = _rope_tables(jnp.arange(lp, dtype=_F32), dh)
    cos_s, sin_s = _rope_tables(PAST_LEN + jnp.arange(1, dtype=_F32), dh)
    y_p, ret_p, hg_p, y_s, ret_s, hg_s = _mixer_call(
        x_prompt, cos_p, sin_p, win, wout, ng, rg, hgg, hg_lb, fg,
        x_sample[:, 0, :], cos_s, sin_s, state_ret[0], state_hgrn[0])
    return (y_p, y_s[:, None, :], ret_p[None], hg_p[None], ret_s[None], hg_s[None])
```

```python
import functools
import math

import jax
import jax.numpy as jnp
from jax import lax
from jax.experimental import pallas as pl
from jax.experimental.pallas import tpu as pltpu

PAST_LEN = 16384
ROPE_BASE = 10000.0
NORM_EPS = 1e-6

PROMPT_TILE = 256
HGRN_CHUNK = 128
PROJ_GROUPS = 16
PACK_ROWS = 256
MXU_ROW_GROUP = 32
VMEM_LIMIT_BYTES = 56 * 1024 * 1024

_F32 = jnp.float32
_BF16 = jnp.bfloat16


def _rms(x, g):
    return x * lax.rsqrt(jnp.mean(x * x, axis=-1, keepdims=True) + NORM_EPS) * g


def _sigmoid(x):
    return 1.0 / (1.0 + jnp.exp(-x))


def _silu(x):
    return x * _sigmoid(x)


def _dot(a, b):
    return jnp.dot(a, b, preferred_element_type=_F32)


def _dot_nt(a, b):
    return lax.dot_general(a, b, (((1,), (1,)), ((), ())), preferred_element_type=_F32)


def _dot_tn(a, b):
    return lax.dot_general(a, b, (((0,), (0,)), ((), ())), preferred_element_type=_F32)


def _weight(w_ref, cols=slice(None)):
    return pltpu.bitcast(w_ref[:, cols], _BF16)


def _pack_kernel(*refs):
    n = len(refs) // 2
    for w_ref, o_ref in zip(refs[:n], refs[n:]):
        o_ref[...] = pltpu.bitcast(w_ref[...].astype(_BF16), jnp.int32)


def _pack_weights(*ws):
    steps = max(w.shape[0] for w in ws) // PACK_ROWS
    rows = [w.shape[0] // steps for w in ws]
    assert all(w.shape[0] == r * steps and r % 16 == 0 for w, r in zip(ws, rows))
    return pl.pallas_call(
        _pack_kernel,
        grid=(steps,),
        in_specs=[pl.BlockSpec((r, w.shape[1]), lambda i: (i, 0)) for w, r in zip(ws, rows)],
        out_specs=[pl.BlockSpec((r // 2, w.shape[1]), lambda i: (i, 0)) for w, r in zip(ws, rows)],
        out_shape=[jax.ShapeDtypeStruct((w.shape[0] // 2, w.shape[1]), jnp.int32) for w in ws],
        compiler_params=pltpu.CompilerParams(dimension_semantics=("arbitrary",)),
        name="pack_weights",
    )(*ws)


def _rotate(x, cos, sin_signed, half):
    return x * cos + pltpu.roll(x, half, 1) * sin_signed


def _lower_bound(lb_ref):
    a = lb_ref[...]
    m = jnp.max(a, axis=0, keepdims=True)
    e = jnp.exp(a - m)
    return e[0:1, :] / jnp.sum(e, axis=0, keepdims=True)


def _forget_gate(hf, lbv):
    z = jnp.exp(-jnp.abs(hf))
    r = 1.0 / (1.0 + z)
    zr = z * r
    pos = hf >= 0.0
    sig_p = jnp.where(pos, r, zr)
    sig_n = jnp.where(pos, zr, r)
    one_m = 1.0 - lbv
    return lbv + one_m * sig_p, one_m * sig_n


def _decay_levels(qa, kin, f):
    c, w = f.shape
    n_lev = c.bit_length() - 1
    rowi = lax.broadcasted_iota(jnp.int32, (c, w), 0)
    tot, q, k = f, qa * f, kin
    for lev in range(3):
        half = 1 << lev
        up = (rowi & half) != 0
        yield jnp.where(up, q, 0.0).astype(_BF16), jnp.where(up, 0.0, k).T.astype(_BF16)
        sib = jnp.where(up, pltpu.roll(tot, half, 0), pltpu.roll(tot, c - half, 0))
        q = jnp.where(up, q * sib, q)
        k = jnp.where(up, k, k * sib)
        tot = tot * sib
    q_b = [q[i * 8:(i + 1) * 8] for i in range(c // 8)]
    k_b = [k[i * 8:(i + 1) * 8] for i in range(c // 8)]
    tot_b = [tot[i * 8:i * 8 + 1] for i in range(c // 8)]
    for lev in range(3, n_lev):
        half = 1 << lev
        zero = jnp.zeros((half, w), _F32)
        yield (jnp.concatenate([q_b[i] if i % 2 else zero for i in range(len(q_b))], axis=0).astype(_BF16),
               jnp.concatenate([zero if i % 2 else k_b[i] for i in range(len(k_b))], axis=0).T.astype(_BF16))
        q_b = [jnp.concatenate([q_b[i], q_b[i + 1] * tot_b[i]], axis=0) for i in range(0, len(q_b), 2)]
        k_b = [jnp.concatenate([k_b[i] * tot_b[i + 1], k_b[i + 1]], axis=0) for i in range(0, len(k_b), 2)]
        tot_b = [tot_b[i] * tot_b[i + 1] for i in range(0, len(tot_b), 2)]
    yield q_b[0].astype(_BF16), k_b[0].astype(_BF16), tot_b[0]


def _proj_stages(x_ref, rows, ng_ref, win_ref, p_ref):
    h = _rms(x_ref[0, rows, :], ng_ref[...]).astype(_BF16)
    yield
    width = p_ref.shape[1] // PROJ_GROUPS
    for sec in range(PROJ_GROUPS):
        cols = slice(sec * width, (sec + 1) * width)
        p_ref[:, cols] = _dot(h, _weight(win_ref, cols))
        yield


def _zip_stages(main, side, every):
    for n, _ in enumerate(main, 1):
        if n % every == 0:
            next(side, None)
        yield


def _filler(stages):
    def fill(n):
        if n is None:
            for _ in stages:
                pass
        else:
            for _ in range(n):
                next(stages, None)
    return fill


def _log_gamma(hh):
    return math.log1p(-(2.0 ** (-5 - hh)))


def _retention_tables(rq_ref, rk_ref, rdec_ref, heads, tile):
    row = lax.broadcasted_iota(jnp.int32, rq_ref.shape[1:], 0).astype(_F32)
    ti = lax.broadcasted_iota(jnp.int32, (tile, tile), 0)
    si = lax.broadcasted_iota(jnp.int32, (tile, tile), 1)
    lag = jnp.maximum(ti - si, 0).astype(_F32)
    for hh in range(heads):
        lg = _log_gamma(hh)
        rq_ref[hh] = jnp.exp(lg * (row + 1.0))
        rk_ref[hh] = jnp.exp(lg * (tile - 1.0 - row))
        rdec_ref[hh] = jnp.where(ti >= si, jnp.exp(lg * lag), 0.0)


def _mixer(x_ref, rows, p_ref, fill, *, cos_ref, sin_ref, wout_ref, rg_ref, hgg_ref, lb_ref, fg_ref,
           y_ref, sret_ref, shg_ref, rq_ref, rk_ref, rdec_ref, heads, dh, tile, chunk):
    gw = heads * dh

    def p(sec, hh=None, rs=slice(None)):
        if hh is None:
            return p_ref[rs, sec * gw:(sec + 1) * gw]
        return p_ref[rs, sec * gw + hh * dh:sec * gw + (hh + 1) * dh]

    o_all = [None] * (2 * heads)

    cos, sin = cos_ref[rows, :], sin_ref[rows, :]

    def ret_issue(hh):
        q = _rotate(p(0, hh), cos, sin, dh // 2).astype(_BF16)
        kt = (_rotate(p(1, hh), cos, sin, dh // 2) * (dh ** -0.5)).astype(_BF16).T
        v = p(2, hh)
        s_old = sret_ref[0, hh]
        qk_qs = _dot(q, jnp.concatenate([kt, s_old.astype(_BF16)], axis=1))
        sret_ref[0, hh] = math.exp(tile * _log_gamma(hh)) * s_old + _dot(kt, (v * rk_ref[hh]).astype(_BF16))
        return hh, qk_qs, v

    def ret_finish(hh, qk_qs, v):
        sl = slice(hh * dh, (hh + 1) * dh)
        a = qk_qs[:, :tile] * rdec_ref[hh]
        o = qk_qs[:, tile:] * rq_ref[hh] + _dot(a.astype(_BF16), v.astype(_BF16))
        o_all[hh] = (_rms(o, rg_ref[:, sl]) * _silu(p(3, hh))).astype(_BF16)

    lbv = _lower_bound(lb_ref)
    ci = lax.broadcasted_iota(jnp.int32, (chunk, chunk), 0)
    cj = lax.broadcasted_iota(jnp.int32, (chunk, chunk), 1)
    cxor = ci ^ cj
    n_lev = chunk.bit_length() - 1
    diag = ci == cj
    lev_mask = [(ci > cj) & ((cxor >> lev) == 1) for lev in range(n_lev)]
    n_chunks = tile // chunk
    o_hg_chunks = [[None] * n_chunks for _ in range(heads)]

    def hg_issue(c, hh):
        rs = slice(c * chunk, (c + 1) * chunk)
        sl = slice(hh * dh, (hh + 1) * dh)
        f, gk = _forget_gate(p(5, hh, rs), lbv[:, sl])
        gq = _silu(p(4, hh, rs))
        gv = p(6, hh, rs).astype(_BF16)
        a = jnp.where(diag, jnp.sum(gq * gk, axis=-1, keepdims=True), 0.0).astype(_BF16)
        *levels, (q_top, k_top, f_tot) = _decay_levels(gq, gk, f)
        for lev, (q_lv, k_lv) in enumerate(levels):
            half = 1 << lev
            if half % MXU_ROW_GROUP:
                a = jnp.where(lev_mask[lev], _dot(q_lv, k_lv).astype(_BF16), a)
            else:
                ups = [slice(i, i + half) for i in range(half, chunk, 2 * half)]
                prod = _dot(jnp.concatenate([q_lv[u] for u in ups], axis=0), k_lv).astype(_BF16)
                parts, at = [], 0
                for j, u in enumerate(ups):
                    parts.append(a[at:u.start])
                    parts.append(jnp.where(lev_mask[lev][u], prod[j * half:(j + 1) * half], a[u]))
                    at = u.stop
                a = jnp.concatenate(parts, axis=0)
        s_old = shg_ref[0, hh]
        o_inter = _dot(q_top, s_old.astype(_BF16))
        f_col = jnp.broadcast_to(f_tot, (dh, dh)).T
        shg_ref[0, hh] = s_old * f_col + _dot_tn(k_top, gv)
        return c, hh, a, o_inter, gv

    def hg_finish(c, hh, a, o_inter, gv):
        o_hg_chunks[hh][c] = o_inter + _dot(a, gv)

    pend = None
    for hh in range(heads):
        ctx = ret_issue(hh)
        fill(1)
        if pend is not None:
            ret_finish(*pend)
        pend = ctx
    fill(2)
    ret_finish(*pend)
    pend = None
    for c in range(n_chunks):
        for hh in range(heads):
            ctx = hg_issue(c, hh)
            fill(1)
            if pend is not None:
                hg_finish(*pend)
            pend = ctx
    fill(2)
    hg_finish(*pend)
    fill(1)
    for hh in range(heads):
        sl = slice(hh * dh, (hh + 1) * dh)
        o = jnp.concatenate(o_hg_chunks[hh], axis=0)
        o_all[heads + hh] = (_rms(o, hgg_ref[:, sl]) * _silu(p(7, hh))).astype(_BF16)
    fill(None)
    o_cat = jnp.concatenate(o_all, axis=-1)
    y = x_ref[0, rows, :] + _dot(o_cat, _weight(wout_ref))
    y_ref[0, rows, :] = _rms(y, fg_ref[...])


def _decode_project(x_ref, cos_ref, sin_ref, win_ref, ng_ref, lb_ref, dec, *, heads, dh):
    rqt_s, rkt_s, rv_s, rgate_s, hqt_s, hkt_s, hi_s, hgate_s, hft_s = dec
    gw = heads * dh
    h = _rms(x_ref[...], ng_ref[...]).astype(_BF16)

    def proj(sec):
        return _dot(h, _weight(win_ref, slice(sec * gw, (sec + 1) * gw)))

    cos, sin = cos_ref[...], sin_ref[...]

    def rot(pj):
        return jnp.concatenate(
            [_rotate(pj[:, hh * dh:(hh + 1) * dh], cos, sin, dh // 2) for hh in range(heads)], axis=-1)

    rqt_s[...] = rot(proj(0)).T
    rkt_s[...] = (rot(proj(1)) * (dh ** -0.5)).T
    rv_s[...] = proj(2)
    rgate_s[...] = _silu(proj(3))
    f, kin = _forget_gate(proj(5), _lower_bound(lb_ref))
    hqt_s[...] = _silu(proj(4)).T
    hkt_s[...] = kin.T
    hft_s[...] = f.T
    hi_s[...] = proj(6)
    hgate_s[...] = _silu(proj(7))


def _decode_stages(step, x_ref, wout_ref, rg_ref, hgg_ref, fg_ref, sret_in, shg_in, y_ref, sret_out, shg_out, dec,
                   *, heads, dh, per_step):
    rqt_s, rkt_s, rv_s, rgate_s, hqt_s, hkt_s, hi_s, hgate_s, hft_s = dec
    nb = x_ref.shape[0]
    wrows = max(per_step, 8)
    share = wrows // per_step
    assert share in (1, 2) and wrows % 8 == 0
    win = pl.ds(pl.multiple_of((step // share) * wrows, 8), wrows)
    base = (step % share) * per_step
    shift = (nb - step * per_step) % nb
    rolled = {}

    def col(ref_s, hh, bl):
        if id(ref_s) not in rolled:
            rolled[id(ref_s)] = pltpu.roll(ref_s[...], shift, 1)
        return jnp.broadcast_to(rolled[id(ref_s)][hh * dh:(hh + 1) * dh, bl:bl + 1], (dh, dh))

    def row_of(ref_s, sl, mine):
        return jnp.sum(jnp.where(mine, ref_s[win, sl], 0.0), axis=0, keepdims=True)

    rowi = lax.broadcasted_iota(jnp.int32, (wrows, dh), 0)
    o_ret = [jnp.zeros((wrows, dh), _F32)] * heads
    o_hg = [jnp.zeros((wrows, dh), _F32)] * heads
    for bl in range(per_step):
        mine = rowi == base + bl
        for hh in range(heads):
            sl = slice(hh * dh, (hh + 1) * dh)
            gamma = 1.0 - 2.0 ** (-5 - hh)
            s_new = gamma * sret_in[bl, hh] + col(rkt_s, hh, bl) * row_of(rv_s, sl, mine)
            sret_out[bl, hh] = s_new
            o = jnp.sum(s_new * col(rqt_s, hh, bl), axis=0, keepdims=True)
            o_ret[hh] = jnp.where(mine, o, o_ret[hh])
            g_new = col(hft_s, hh, bl) * shg_in[bl, hh] + col(hkt_s, hh, bl) * row_of(hi_s, sl, mine)
            shg_out[bl, hh] = g_new
            o = jnp.sum(g_new * col(hqt_s, hh, bl), axis=0, keepdims=True)
            o_hg[hh] = jnp.where(mine, o, o_hg[hh])
            yield

    def finish(o_rows, gain_ref, gate):
        return [_rms(o, gain_ref[:, hh * dh:(hh + 1) * dh]) * gate[:, hh * dh:(hh + 1) * dh]
                for hh, o in enumerate(o_rows)]

    o_cat = jnp.concatenate(
        finish(o_ret, rg_ref, rgate_s[win, :]) + finish(o_hg, hgg_ref, hgate_s[win, :]), axis=-1).astype(_BF16)
    y = _rms(x_ref[win, :] + _dot(o_cat, _weight(wout_ref)), fg_ref[...])
    if share == 1:
        y_ref[win, :] = y
    else:
        @pl.when(base == 0)
        def _():
            y_ref[win, :] = y

        @pl.when(base != 0)
        def _():
            keep = lax.broadcasted_iota(jnp.int32, y.shape, 0) >= base
            y_ref[win, :] = jnp.where(keep, y, y_ref[win, :])
    yield


def _mixer_kernel(xp_ref, xn_ref, cos_ref, sin_ref, win_ref, wout_ref, ng_ref, rg_ref, hgg_ref, lb_ref, fg_ref,
                  xs_ref, cos_s_ref, sin_s_ref, dret_in, dhg_in,
                  y_ref, sret_ref, shg_ref, ys_ref, dret_out, dhg_out,
                  pa_ref, pb_ref, rq_ref, rk_ref, rdec_ref, *dec,
                  heads, dh, tile, chunk, pairs, per_step):
    step = pl.program_id(0)
    pair = step % pairs
    rows_a, rows_b = slice(0, tile), slice(tile, 2 * tile)

    @pl.when(pair == 0)
    def _():
        sret_ref[...] = jnp.zeros_like(sret_ref)
        shg_ref[...] = jnp.zeros_like(shg_ref)

    @pl.when(step == 0)
    def _():
        _retention_tables(rq_ref, rk_ref, rdec_ref, heads, tile)
        _decode_project(xs_ref, cos_s_ref, sin_s_ref, win_ref, ng_ref, lb_ref, dec, heads=heads, dh=dh)
        for _ in _proj_stages(xp_ref, rows_a, ng_ref, win_ref, pa_ref):
            pass

    mixer = functools.partial(
        _mixer, cos_ref=cos_ref, sin_ref=sin_ref, wout_ref=wout_ref, rg_ref=rg_ref, hgg_ref=hgg_ref,
        lb_ref=lb_ref, fg_ref=fg_ref, y_ref=y_ref, sret_ref=sret_ref, shg_ref=shg_ref,
        rq_ref=rq_ref, rk_ref=rk_ref, rdec_ref=rdec_ref, heads=heads, dh=dh, tile=tile, chunk=chunk)
    decode = _decode_stages(step, xs_ref, wout_ref, rg_ref, hgg_ref, fg_ref, dret_in, dhg_in, ys_ref,
                            dret_out, dhg_out, dec, heads=heads, dh=dh, per_step=per_step)
    every = max(1, 2 * (1 + PROJ_GROUPS) // (per_step * heads + 1))
    mixer(xp_ref, rows_a, pa_ref,
          _filler(_zip_stages(_proj_stages(xp_ref, rows_b, ng_ref, win_ref, pb_ref), decode, every)))
    mixer(xp_ref, rows_b, pb_ref,
          _filler(_zip_stages(_proj_stages(xn_ref, rows_a, ng_ref, win_ref, pa_ref), decode, every)))
    for _ in decode:
        pass


def _mixer_call(x, cos, sin, win, wout, ng, rg, hgg, lb, fg, xs, cos_s, sin_s, dret, dhg):
    bsz, seq, d = x.shape
    nb, heads, dh, _ = dret.shape
    tile = min(PROMPT_TILE, seq // 2)
    chunk = min(HGRN_CHUNK, tile)
    assert seq % (2 * tile) == 0 and tile % chunk == 0 and chunk & (chunk - 1) == 0 and chunk >= 16
    gw = heads * dh
    pairs = seq // (2 * tile)
    steps = bsz * pairs
    per_step = nb // steps
    assert nb == per_step * steps and nb % 128 == 0

    def nxt(s):
        s1 = jnp.minimum(s + 1, steps - 1)
        return (s1 // pairs, 2 * (s1 % pairs), 0)

    const = lambda *shape: pl.BlockSpec(shape, lambda s: (0,) * len(shape))
    state_spec = pl.BlockSpec((1, heads, dh, dh), lambda s: (s // pairs, 0, 0, 0))
    dstate_spec = pl.BlockSpec((per_step, heads, dh, dh), lambda s: (s, 0, 0, 0))
    pair_spec = pl.BlockSpec((1, 2 * tile, d), lambda s: (s // pairs, s % pairs, 0))
    rope_spec = pl.BlockSpec((2 * tile, dh), lambda s: (s % pairs, 0))
    body = functools.partial(_mixer_kernel, heads=heads, dh=dh, tile=tile, chunk=chunk, pairs=pairs,
                             per_step=per_step)
    row_scratch = pltpu.VMEM((nb, gw), _F32)
    col_scratch = pltpu.VMEM((gw, nb), _F32)
    return pl.pallas_call(
        body,
        grid=(steps,),
        in_specs=[
            pair_spec, pl.BlockSpec((1, tile, d), nxt), rope_spec, rope_spec,
            const(d // 2, 8 * gw), const(gw, d), const(1, d), const(1, gw), const(1, gw),
            const(lb.shape[0], gw), const(1, d),
            const(nb, d), const(1, dh), const(1, dh), dstate_spec, dstate_spec,
        ],
        out_specs=[pair_spec, state_spec, state_spec, const(nb, d), dstate_spec, dstate_spec],
        out_shape=[
            jax.ShapeDtypeStruct((bsz, seq, d), _F32),
            jax.ShapeDtypeStruct((bsz, heads, dh, dh), _F32),
            jax.ShapeDtypeStruct((bsz, heads, dh, dh), _F32),
            jax.ShapeDtypeStruct((nb, d), _F32),
            jax.ShapeDtypeStruct(dret.shape, _F32),
            jax.ShapeDtypeStruct(dhg.shape, _F32),
        ],
        scratch_shapes=[pltpu.VMEM((tile, 8 * gw), _F32), pltpu.VMEM((tile, 8 * gw), _F32),
                        pltpu.VMEM((heads, tile, dh), _F32), pltpu.VMEM((heads, tile, dh), _F32),
                        pltpu.VMEM((heads, tile, tile), _F32)]
                       + [col_scratch, col_scratch, row_scratch, row_scratch, col_scratch, col_scratch,
                          row_scratch, row_scratch, col_scratch],
        compiler_params=pltpu.CompilerParams(
            dimension_semantics=("arbitrary",), vmem_limit_bytes=VMEM_LIMIT_BYTES),
        name="mixer_step",
    )(x, x, cos, sin, win, wout, ng, rg, hgg, lb, fg, xs, cos_s, sin_s, dret, dhg)


def _rope_tables(pos, dh):
    half = dh // 2
    lane = jnp.arange(dh)
    freqs = 1.0 / (ROPE_BASE ** ((lane % half).astype(_F32) / half))
    ang = pos[:, None] * freqs[None, :]
    return jnp.cos(ang), jnp.sin(ang) * jnp.where(lane < half, -1.0, 1.0)[None, :]


def kernel(x_prompt, x_sample, state_ret, state_hgrn, norm_g, w_in, ret_norm_g, hg_norm_g, hg_lb, w_out,
           final_norm_g):
    depth, _, heads, dh, _ = state_ret.shape
    assert depth == 1 and x_sample.shape[1] == 1
    lp = x_prompt.shape[1]
    win, wout = _pack_weights(w_in[0], w_out[0])
    ng, fg = norm_g[0][None, :], final_norm_g[None, :]
    rg, hgg = ret_norm_g[0][None, :], hg_norm_g[0][None, :]

    cos_p, sin_p = _rope_tables(jnp.arange(lp, dtype=_F32), dh)
    cos_s, sin_s = _rope_tables(PAST_LEN + jnp.arange(1, dtype=_F32), dh)
    y_p, ret_p, hg_p, y_s, ret_s, hg_s = _mixer_call(
        x_prompt, cos_p, sin_p, win, wout, ng, rg, hgg, hg_lb, fg,
        x_sample[:, 0, :], cos_s, sin_s, state_ret[0], state_hgrn[0])
    return (y_p, y_s[:, None, :], ret_p[None], hg_p[None], ret_s[None], hg_s[None])
```

```python
import functools
import math

import jax
import jax.numpy as jnp
from jax import lax
from jax.experimental import pallas as pl
from jax.experimental.pallas import tpu as pltpu

PAST_LEN = 16384
ROPE_BASE = 10000.0
NORM_EPS = 1e-6

PROMPT_TILE = 256
HGRN_CHUNK = 128
PROJ_GROUPS = 16
PACK_ROWS = 256
MXU_ROW_GROUP = 32
SUBLANES = 8
VMEM_LIMIT_BYTES = 56 * 1024 * 1024

_F32 = jnp.float32
_BF16 = jnp.bfloat16


def _rms(x, g):
    return x * lax.rsqrt(jnp.mean(x * x, axis=-1, keepdims=True) + NORM_EPS) * g


def _sigmoid(x):
    return 1.0 / (1.0 + jnp.exp(-x))


def _silu(x):
    return x * _sigmoid(x)


def _dot(a, b):
    return jnp.dot(a, b, preferred_element_type=_F32)


def _dot_nt(a, b):
    return lax.dot_general(a, b, (((1,), (1,)), ((), ())), preferred_element_type=_F32)


def _dot_tn(a, b):
    return lax.dot_general(a, b, (((0,), (0,)), ((), ())), preferred_element_type=_F32)


def _weight(w_ref, cols=slice(None)):
    return pltpu.bitcast(w_ref[:, cols], _BF16)


def _pack_kernel(*refs):
    n = len(refs) // 2
    for w_ref, o_ref in zip(refs[:n], refs[n:]):
        o_ref[...] = pltpu.bitcast(w_ref[...].astype(_BF16), jnp.int32)


def _pack_weights(*ws):
    steps = max(w.shape[0] for w in ws) // PACK_ROWS
    rows = [w.shape[0] // steps for w in ws]
    assert all(w.shape[0] == r * steps and r % 16 == 0 for w, r in zip(ws, rows))
    return pl.pallas_call(
        _pack_kernel,
        grid=(steps,),
        in_specs=[pl.BlockSpec((r, w.shape[1]), lambda i: (i, 0)) for w, r in zip(ws, rows)],
        out_specs=[pl.BlockSpec((r // 2, w.shape[1]), lambda i: (i, 0)) for w, r in zip(ws, rows)],
        out_shape=[jax.ShapeDtypeStruct((w.shape[0] // 2, w.shape[1]), jnp.int32) for w in ws],
        compiler_params=pltpu.CompilerParams(dimension_semantics=("arbitrary",)),
        name="pack_weights",
    )(*ws)


def _rotate(x, cos, sin_signed, half):
    return x * cos + pltpu.roll(x, half, 1) * sin_signed


def _lower_bound(lb_ref):
    a = lb_ref[...]
    m = jnp.max(a, axis=0, keepdims=True)
    e = jnp.exp(a - m)
    return e[0:1, :] / jnp.sum(e, axis=0, keepdims=True)


def _forget_gate(hf, lbv):
    z = jnp.exp(-jnp.abs(hf))
    r = 1.0 / (1.0 + z)
    zr = z * r
    pos = hf >= 0.0
    sig_p = jnp.where(pos, r, zr)
    sig_n = jnp.where(pos, zr, r)
    one_m = 1.0 - lbv
    return lbv + one_m * sig_p, one_m * sig_n


def _decay_levels(qa, kin, f):
    c, w = f.shape
    n_lev = c.bit_length() - 1
    rowi = lax.broadcasted_iota(jnp.int32, (c, w), 0)
    tot, q, k = f, qa * f, kin
    sub_levels = SUBLANES.bit_length() - 1
    for lev in range(sub_levels):
        half = 1 << lev
        up = (rowi & half) != 0
        yield q.astype(_BF16), k.T.astype(_BF16)
        sib = jnp.where(up, pltpu.roll(tot, half, 0), pltpu.roll(tot, c - half, 0))
        q = jnp.where(up, q * sib, q)
        k = jnp.where(up, k, k * sib)
        tot = tot * sib
    q_b = [q[i:i + SUBLANES] for i in range(0, c, SUBLANES)]
    k_b = [k[i:i + SUBLANES] for i in range(0, c, SUBLANES)]
    tot_b = [tot[i:i + 1] for i in range(0, c, SUBLANES)]
    for lev in range(sub_levels, n_lev):
        half = 1 << lev
        zero = jnp.zeros((half, w), _F32)
        yield (jnp.concatenate([q_b[i] if i % 2 else zero for i in range(len(q_b))], axis=0).astype(_BF16),
               jnp.concatenate([zero if i % 2 else k_b[i] for i in range(len(k_b))], axis=0).T.astype(_BF16))
        q_b = [jnp.concatenate([q_b[i], q_b[i + 1] * tot_b[i]], axis=0) for i in range(0, len(q_b), 2)]
        k_b = [jnp.concatenate([k_b[i] * tot_b[i + 1], k_b[i + 1]], axis=0) for i in range(0, len(k_b), 2)]
        tot_b = [tot_b[i] * tot_b[i + 1] for i in range(0, len(tot_b), 2)]
    yield q_b[0].astype(_BF16), k_b[0].astype(_BF16), tot_b[0]


def _proj_stages(x_ref, rows, ng_ref, win_ref, p_ref):
    h = _rms(x_ref[0, rows, :], ng_ref[...]).astype(_BF16)
    yield
    width = p_ref.shape[1] // PROJ_GROUPS
    for sec in range(PROJ_GROUPS):
        cols = slice(sec * width, (sec + 1) * width)
        p_ref[:, cols] = _dot(h, _weight(win_ref, cols))
        yield


def _zip_stages(main, side, every):
    for n, _ in enumerate(main, 1):
        if n % every == 0:
            next(side, None)
        yield


def _filler(stages):
    def fill(n):
        if n is None:
            for _ in stages:
                pass
        else:
            for _ in range(n):
                next(stages, None)
    return fill


def _log_gamma(hh):
    return math.log1p(-(2.0 ** (-5 - hh)))


def _retention_tables(rq_ref, rk_ref, rdec_ref, heads, tile):
    row = lax.broadcasted_iota(jnp.int32, rq_ref.shape[1:], 0).astype(_F32)
    ti = lax.broadcasted_iota(jnp.int32, (tile, tile), 0)
    si = lax.broadcasted_iota(jnp.int32, (tile, tile), 1)
    lag = jnp.maximum(ti - si, 0).astype(_F32)
    for hh in range(heads):
        lg = _log_gamma(hh)
        rq_ref[hh] = jnp.exp(lg * (row + 1.0))
        rk_ref[hh] = jnp.exp(lg * (tile - 1.0 - row))
        rdec_ref[hh] = jnp.where(ti >= si, jnp.exp(lg * lag), 0.0)


def _mixer(x_ref, rows, p_ref, fill, *, cos_ref, sin_ref, wout_ref, rg_ref, hgg_ref, lb_ref, fg_ref,
           y_ref, sret_ref, shg_ref, rq_ref, rk_ref, rdec_ref, heads, dh, tile, chunk):
    gw = heads * dh

    def p(sec, hh=None, rs=slice(None)):
        if hh is None:
            return p_ref[rs, sec * gw:(sec + 1) * gw]
        return p_ref[rs, sec * gw + hh * dh:sec * gw + (hh + 1) * dh]

    o_all = [None] * (2 * heads)

    cos, sin = cos_ref[rows, :], sin_ref[rows, :]

    def ret_issue(hh):
        q = _rotate(p(0, hh), cos, sin, dh // 2).astype(_BF16)
        kt = (_rotate(p(1, hh), cos, sin, dh // 2) * (dh ** -0.5)).astype(_BF16).T
        v = p(2, hh)
        s_old = sret_ref[0, hh]
        qk_qs = _dot(q, jnp.concatenate([kt, s_old.astype(_BF16)], axis=1))
        sret_ref[0, hh] = math.exp(tile * _log_gamma(hh)) * s_old + _dot(kt, (v * rk_ref[hh]).astype(_BF16))
        return hh, qk_qs, v

    def ret_finish(hh, qk_qs, v):
        sl = slice(hh * dh, (hh + 1) * dh)
        a = qk_qs[:, :tile] * rdec_ref[hh]
        o = qk_qs[:, tile:] * rq_ref[hh] + _dot(a.astype(_BF16), v.astype(_BF16))
        o_all[hh] = (_rms(o, rg_ref[:, sl]) * _silu(p(3, hh))).astype(_BF16)

    lbv = _lower_bound(lb_ref)
    ci = lax.broadcasted_iota(jnp.int32, (chunk, chunk), 0)
    cj = lax.broadcasted_iota(jnp.int32, (chunk, chunk), 1)
    cxor = ci ^ cj
    n_lev = chunk.bit_length() - 1
    diag = ci == cj
    lev_mask = [(ci > cj) & ((cxor >> lev) == 1) for lev in range(n_lev)]
    n_chunks = tile // chunk
    o_hg_chunks = [[None] * n_chunks for _ in range(heads)]

    def hg_issue(c, hh):
        rs = slice(c * chunk, (c + 1) * chunk)
        sl = slice(hh * dh, (hh + 1) * dh)
        f, gk = _forget_gate(p(5, hh, rs), lbv[:, sl])
        gq = _silu(p(4, hh, rs))
        gv = p(6, hh, rs).astype(_BF16)
        a = jnp.where(diag, jnp.sum(gq * gk, axis=-1, keepdims=True), 0.0).astype(_BF16)
        *levels, (q_top, k_top, f_tot) = _decay_levels(gq, gk, f)
        for lev, (q_lv, k_lv) in enumerate(levels):
            half = 1 << lev
            if half % MXU_ROW_GROUP:
                a = jnp.where(lev_mask[lev], _dot(q_lv, k_lv).astype(_BF16), a)
            else:
                ups = [slice(i, i + half) for i in range(half, chunk, 2 * half)]
                prod = _dot(jnp.concatenate([q_lv[u] for u in ups], axis=0), k_lv).astype(_BF16)
                parts, at = [], 0
                for j, u in enumerate(ups):
                    parts.append(a[at:u.start])
                    parts.append(jnp.where(lev_mask[lev][u], prod[j * half:(j + 1) * half], a[u]))
                    at = u.stop
                a = jnp.concatenate(parts, axis=0)
        s_old = shg_ref[0, hh]
        o_inter = _dot(q_top, s_old.astype(_BF16))
        f_col = jnp.broadcast_to(f_tot, (dh, dh)).T
        shg_ref[0, hh] = s_old * f_col + _dot_tn(k_top, gv)
        return c, hh, a, o_inter, gv

    def hg_finish(c, hh, a, o_inter, gv):
        o_hg_chunks[hh][c] = o_inter + _dot(a, gv)

    pend = None
    for hh in range(heads):
        ctx = ret_issue(hh)
        fill(1)
        if pend is not None:
            ret_finish(*pend)
        pend = ctx
    fill(2)
    ret_finish(*pend)
    pend = None
    for c in range(n_chunks):
        for hh in range(heads):
            ctx = hg_issue(c, hh)
            fill(1)
            if pend is not None:
                hg_finish(*pend)
            pend = ctx
    fill(2)
    hg_finish(*pend)
    fill(1)
    for hh in range(heads):
        sl = slice(hh * dh, (hh + 1) * dh)
        o = jnp.concatenate(o_hg_chunks[hh], axis=0)
        o_all[heads + hh] = (_rms(o, hgg_ref[:, sl]) * _silu(p(7, hh))).astype(_BF16)
    fill(None)
    o_cat = jnp.concatenate(o_all, axis=-1)
    y = x_ref[0, rows, :] + _dot(o_cat, _weight(wout_ref))
    y_ref[0, rows, :] = _rms(y, fg_ref[...])


def _decode_project(x_ref, cos_ref, sin_ref, win_ref, ng_ref, lb_ref, dec, *, heads, dh):
    rqt_s, rkt_s, rv_s, rgate_s, hqt_s, hkt_s, hi_s, hgate_s, hft_s = dec
    gw = heads * dh
    h = _rms(x_ref[...], ng_ref[...]).astype(_BF16)

    def proj(sec):
        return _dot(h, _weight(win_ref, slice(sec * gw, (sec + 1) * gw)))

    cos, sin = cos_ref[...], sin_ref[...]

    def rot(pj):
        return jnp.concatenate(
            [_rotate(pj[:, hh * dh:(hh + 1) * dh], cos, sin, dh // 2) for hh in range(heads)], axis=-1)

    rqt_s[...] = rot(proj(0)).T
    rkt_s[...] = (rot(proj(1)) * (dh ** -0.5)).T
    rv_s[...] = proj(2)
    rgate_s[...] = _silu(proj(3))
    f, kin = _forget_gate(proj(5), _lower_bound(lb_ref))
    hqt_s[...] = _silu(proj(4)).T
    hkt_s[...] = kin.T
    hft_s[...] = f.T
    hi_s[...] = proj(6)
    hgate_s[...] = _silu(proj(7))


def _decode_stages(step, x_ref, wout_ref, rg_ref, hgg_ref, fg_ref, sret_in, shg_in, y_ref, sret_out, shg_out, dec,
                   *, heads, dh, per_step):
    rqt_s, rkt_s, rv_s, rgate_s, hqt_s, hkt_s, hi_s, hgate_s, hft_s = dec
    nb = x_ref.shape[0]
    wrows = max(per_step, SUBLANES)
    share = wrows // per_step
    assert share in (1, 2) and wrows % SUBLANES == 0
    win = pl.ds(pl.multiple_of((step // share) * wrows, SUBLANES), wrows)
    base = (step % share) * per_step
    shift = (nb - step * per_step) % nb
    rolled = {}

    def col(ref_s, hh, bl):
        if id(ref_s) not in rolled:
            rolled[id(ref_s)] = pltpu.roll(ref_s[...], shift, 1)
        return jnp.broadcast_to(rolled[id(ref_s)][hh * dh:(hh + 1) * dh, bl:bl + 1], (dh, dh))

    def row_of(ref_s, sl, mine):
        return jnp.sum(jnp.where(mine, ref_s[win, sl], 0.0), axis=0, keepdims=True)

    rowi = lax.broadcasted_iota(jnp.int32, (wrows, dh), 0)
    o_ret = [jnp.zeros((wrows, dh), _F32)] * heads
    o_hg = [jnp.zeros((wrows, dh), _F32)] * heads
    for bl in range(per_step):
        mine = rowi == base + bl
        for hh in range(heads):
            sl = slice(hh * dh, (hh + 1) * dh)
            gamma = 1.0 - 2.0 ** (-5 - hh)
            s_new = gamma * sret_in[bl, hh] + col(rkt_s, hh, bl) * row_of(rv_s, sl, mine)
            sret_out[bl, hh] = s_new
            o = jnp.sum(s_new * col(rqt_s, hh, bl), axis=0, keepdims=True)
            o_ret[hh] = jnp.where(mine, o, o_ret[hh])
            g_new = col(hft_s, hh, bl) * shg_in[bl, hh] + col(hkt_s, hh, bl) * row_of(hi_s, sl, mine)
            shg_out[bl, hh] = g_new
            o = jnp.sum(g_new * col(hqt_s, hh, bl), axis=0, keepdims=True)
            o_hg[hh] = jnp.where(mine, o, o_hg[hh])
            yield

    def finish(o_rows, gain_ref, gate):
        return [_rms(o, gain_ref[:, hh * dh:(hh + 1) * dh]) * gate[:, hh * dh:(hh + 1) * dh]
                for hh, o in enumerate(o_rows)]

    o_cat = jnp.concatenate(
        finish(o_ret, rg_ref, rgate_s[win, :]) + finish(o_hg, hgg_ref, hgate_s[win, :]), axis=-1).astype(_BF16)
    y = _rms(x_ref[win, :] + _dot(o_cat, _weight(wout_ref)), fg_ref[...])
    if share == 1:
        y_ref[win, :] = y
    else:
        @pl.when(base == 0)
        def _():
            y_ref[win, :] = y

        @pl.when(base != 0)
        def _():
            keep = lax.broadcasted_iota(jnp.int32, y.shape, 0) >= base
            y_ref[win, :] = jnp.where(keep, y, y_ref[win, :])
    yield


def _mixer_kernel(xp_ref, xn_ref, cos_ref, sin_ref, win_ref, wout_ref, ng_ref, rg_ref, hgg_ref, lb_ref, fg_ref,
                  xs_ref, cos_s_ref, sin_s_ref, dret_in, dhg_in,
                  y_ref, sret_ref, shg_ref, ys_ref, dret_out, dhg_out,
                  pa_ref, pb_ref, rq_ref, rk_ref, rdec_ref, *dec,
                  heads, dh, tile, chunk, pairs, per_step):
    step = pl.program_id(0)
    pair = step % pairs
    rows_a, rows_b = slice(0, tile), slice(tile, 2 * tile)

    @pl.when(pair == 0)
    def _():
        sret_ref[...] = jnp.zeros_like(sret_ref)
        shg_ref[...] = jnp.zeros_like(shg_ref)

    @pl.when(step == 0)
    def _():
        _retention_tables(rq_ref, rk_ref, rdec_ref, heads, tile)
        _decode_project(xs_ref, cos_s_ref, sin_s_ref, win_ref, ng_ref, lb_ref, dec, heads=heads, dh=dh)
        for _ in _proj_stages(xp_ref, rows_a, ng_ref, win_ref, pa_ref):
            pass

    mixer = functools.partial(
        _mixer, cos_ref=cos_ref, sin_ref=sin_ref, wout_ref=wout_ref, rg_ref=rg_ref, hgg_ref=hgg_ref,
        lb_ref=lb_ref, fg_ref=fg_ref, y_ref=y_ref, sret_ref=sret_ref, shg_ref=shg_ref,
        rq_ref=rq_ref, rk_ref=rk_ref, rdec_ref=rdec_ref, heads=heads, dh=dh, tile=tile, chunk=chunk)
    decode = _decode_stages(step, xs_ref, wout_ref, rg_ref, hgg_ref, fg_ref, dret_in, dhg_in, ys_ref,
                            dret_out, dhg_out, dec, heads=heads, dh=dh, per_step=per_step)
    every = max(1, 2 * (1 + PROJ_GROUPS) // (per_step * heads + 1))
    mixer(xp_ref, rows_a, pa_ref,
          _filler(_zip_stages(_proj_stages(xp_ref, rows_b, ng_ref, win_ref, pb_ref), decode, every)))
    mixer(xp_ref, rows_b, pb_ref,
          _filler(_zip_stages(_proj_stages(xn_ref, rows_a, ng_ref, win_ref, pa_ref), decode, every)))
    for _ in decode:
        pass


def _mixer_call(x, cos, sin, win, wout, ng, rg, hgg, lb, fg, xs, cos_s, sin_s, dret, dhg):
    bsz, seq, d = x.shape
    nb, heads, dh, _ = dret.shape
    tile = min(PROMPT_TILE, seq // 2)
    chunk = min(HGRN_CHUNK, tile)
    assert seq % (2 * tile) == 0 and tile % chunk == 0 and chunk & (chunk - 1) == 0 and chunk >= 16
    gw = heads * dh
    pairs = seq // (2 * tile)
    steps = bsz * pairs
    per_step = nb // steps
    assert nb == per_step * steps and nb % 128 == 0

    def nxt(s):
        s1 = jnp.minimum(s + 1, steps - 1)
        return (s1 // pairs, 2 * (s1 % pairs), 0)

    const = lambda *shape: pl.BlockSpec(shape, lambda s: (0,) * len(shape))
    state_spec = pl.BlockSpec((1, heads, dh, dh), lambda s: (s // pairs, 0, 0, 0))
    dstate_spec = pl.BlockSpec((per_step, heads, dh, dh), lambda s: (s, 0, 0, 0))
    pair_spec = pl.BlockSpec((1, 2 * tile, d), lambda s: (s // pairs, s % pairs, 0))
    rope_spec = pl.BlockSpec((2 * tile, dh), lambda s: (s % pairs, 0))
    body = functools.partial(_mixer_kernel, heads=heads, dh=dh, tile=tile, chunk=chunk, pairs=pairs,
                             per_step=per_step)
    row_scratch = pltpu.VMEM((nb, gw), _F32)
    col_scratch = pltpu.VMEM((gw, nb), _F32)
    return pl.pallas_call(
        body,
        grid=(steps,),
        in_specs=[
            pair_spec, pl.BlockSpec((1, tile, d), nxt), rope_spec, rope_spec,
            const(d // 2, 8 * gw), const(gw, d), const(1, d), const(1, gw), const(1, gw),
            const(lb.shape[0], gw), const(1, d),
            const(nb, d), const(1, dh), const(1, dh), dstate_spec, dstate_spec,
        ],
        out_specs=[pair_spec, state_spec, state_spec, const(nb, d), dstate_spec, dstate_spec],
        out_shape=[
            jax.ShapeDtypeStruct((bsz, seq, d), _F32),
            jax.ShapeDtypeStruct((bsz, heads, dh, dh), _F32),
            jax.ShapeDtypeStruct((bsz, heads, dh, dh), _F32),
            jax.ShapeDtypeStruct((nb, d), _F32),
            jax.ShapeDtypeStruct(dret.shape, _F32),
            jax.ShapeDtypeStruct(dhg.shape, _F32),
        ],
        scratch_shapes=[pltpu.VMEM((tile, 8 * gw), _F32), pltpu.VMEM((tile, 8 * gw), _F32),
                        pltpu.VMEM((heads, tile, dh), _F32), pltpu.VMEM((heads, tile, dh), _F32),
                        pltpu.VMEM((heads, tile, tile), _F32)]
                       + [col_scratch, col_scratch, row_scratch, row_scratch, col_scratch, col_scratch,
                          row_scratch, row_scratch, col_scratch],
        compiler_params=pltpu.CompilerParams(
            dimension_semantics=("arbitrary",), vmem_limit_bytes=VMEM_LIMIT_BYTES),
        name="mixer_step",
    )(x, x, cos, sin, win, wout, ng, rg, hgg, lb, fg, xs, cos_s, sin_s, dret, dhg)


def _rope_tables(pos, dh):
    half = dh // 2
    lane = jnp.arange(dh)
    freqs = 1.0 / (ROPE_BASE ** ((lane % half).astype(_F32) / half))
    ang = pos[:, None] * freqs[None, :]
    return jnp.cos(ang), jnp.sin(ang) * jnp.where(lane < half, -1.0, 1.0)[None, :]


def kernel(x_prompt, x_sample, state_ret, state_hgrn, norm_g, w_in, ret_norm_g, hg_norm_g, hg_lb, w_out,
           final_norm_g):
    depth, _, heads, dh, _ = state_ret.shape
    assert depth == 1 and x_sample.shape[1] == 1
    lp = x_prompt.shape[1]
    win, wout = _pack_weights(w_in[0], w_out[0])
    ng, fg = norm_g[0][None, :], final_norm_g[None, :]
    rg, hgg = ret_norm_g[0][None, :], hg_norm_g[0][None, :]

    cos_p, sin_p = _rope_tables(jnp.arange(lp, dtype=_F32), dh)
    cos_s, sin_s = _rope_tables(PAST_LEN + jnp.arange(1, dtype=_F32), dh)
    y_p, ret_p, hg_p, y_s, ret_s, hg_s = _mixer_call(
        x_prompt, cos_p, sin_p, win, wout, ng, rg, hgg, hg_lb, fg,
        x_sample[:, 0, :], cos_s, sin_s, state_ret[0], state_hgrn[0])
    return (y_p, y_s[:, None, :], ret_p[None], hg_p[None], ret_s[None], hg_s[None])
```

```python
import functools
import math

import jax
import jax.numpy as jnp
from jax import lax
from jax.experimental import pallas as pl
from jax.experimental.pallas import tpu as pltpu

PAST_LEN = 16384
ROPE_BASE = 10000.0
NORM_EPS = 1e-6

PROMPT_TILE = 256
HGRN_CHUNK = 128
PROJ_GROUPS = 16
PACK_ROWS = 256
MXU_ROW_GROUP = 32
SUBLANES = 8
VMEM_LIMIT_BYTES = 56 * 1024 * 1024

_F32 = jnp.float32
_BF16 = jnp.bfloat16


def _rms(x, g):
    return x * lax.rsqrt(jnp.mean(x * x, axis=-1, keepdims=True) + NORM_EPS) * g


def _sigmoid(x):
    return 1.0 / (1.0 + jnp.exp(-x))


def _silu(x):
    return x * _sigmoid(x)


def _dot(a, b):
    return jnp.dot(a, b, preferred_element_type=_F32)


def _dot_nt(a, b):
    return lax.dot_general(a, b, (((1,), (1,)), ((), ())), preferred_element_type=_F32)


def _dot_tn(a, b):
    return lax.dot_general(a, b, (((0,), (0,)), ((), ())), preferred_element_type=_F32)


def _weight(w_ref, cols=slice(None)):
    return pltpu.bitcast(w_ref[:, cols], _BF16)


def _pack_kernel(*refs):
    n = len(refs) // 2
    for w_ref, o_ref in zip(refs[:n], refs[n:]):
        o_ref[...] = pltpu.bitcast(w_ref[...].astype(_BF16), jnp.int32)


def _pack_weights(*ws):
    steps = max(w.shape[0] for w in ws) // PACK_ROWS
    rows = [w.shape[0] // steps for w in ws]
    assert all(w.shape[0] == r * steps and r % 16 == 0 for w, r in zip(ws, rows))
    return pl.pallas_call(
        _pack_kernel,
        grid=(steps,),
        in_specs=[pl.BlockSpec((r, w.shape[1]), lambda i: (i, 0)) for w, r in zip(ws, rows)],
        out_specs=[pl.BlockSpec((r // 2, w.shape[1]), lambda i: (i, 0)) for w, r in zip(ws, rows)],
        out_shape=[jax.ShapeDtypeStruct((w.shape[0] // 2, w.shape[1]), jnp.int32) for w in ws],
        compiler_params=pltpu.CompilerParams(dimension_semantics=("arbitrary",)),
        name="pack_weights",
    )(*ws)


def _rotate(x, cos, sin_signed, half):
    return x * cos + pltpu.roll(x, half, 1) * sin_signed


def _lower_bound(lb_ref):
    a = lb_ref[...]
    m = jnp.max(a, axis=0, keepdims=True)
    e = jnp.exp(a - m)
    return e[0:1, :] / jnp.sum(e, axis=0, keepdims=True)


def _forget_gate(hf, lbv):
    z = jnp.exp(-jnp.abs(hf))
    r = 1.0 / (1.0 + z)
    zr = z * r
    pos = hf >= 0.0
    sig_p = jnp.where(pos, r, zr)
    sig_n = jnp.where(pos, zr, r)
    one_m = 1.0 - lbv
    return lbv + one_m * sig_p, one_m * sig_n


def _decay_levels(qa, kin, f):
    c, w = f.shape
    n_lev = c.bit_length() - 1
    rowi = lax.broadcasted_iota(jnp.int32, (c, w), 0)
    tot, q, k = f, qa * f, kin
    sub_levels = SUBLANES.bit_length() - 1
    for lev in range(sub_levels):
        half = 1 << lev
        up = (rowi & half) != 0
        yield jnp.where(up, q, 0.0).astype(_BF16), jnp.where(up, 0.0, k).T.astype(_BF16)
        sib = jnp.where(up, pltpu.roll(tot, half, 0), pltpu.roll(tot, c - half, 0))
        q = jnp.where(up, q * sib, q)
        k = jnp.where(up, k, k * sib)
        tot = tot * sib
    q_b = [q[i:i + SUBLANES] for i in range(0, c, SUBLANES)]
    k_b = [k[i:i + SUBLANES] for i in range(0, c, SUBLANES)]
    tot_b = [tot[i:i + 1] for i in range(0, c, SUBLANES)]
    for lev in range(sub_levels, n_lev):
        half = 1 << lev
        zero = jnp.zeros((half, w), _F32)
        yield (jnp.concatenate([q_b[i] if i % 2 else zero for i in range(len(q_b))], axis=0).astype(_BF16),
               jnp.concatenate([zero if i % 2 else k_b[i] for i in range(len(k_b))], axis=0).T.astype(_BF16))
        q_b = [jnp.concatenate([q_b[i], q_b[i + 1] * tot_b[i]], axis=0) for i in range(0, len(q_b), 2)]
        k_b = [jnp.concatenate([k_b[i] * tot_b[i + 1], k_b[i + 1]], axis=0) for i in range(0, len(k_b), 2)]
        tot_b = [tot_b[i] * tot_b[i + 1] for i in range(0, len(tot_b), 2)]
    yield q_b[0].astype(_BF16), k_b[0].astype(_BF16), tot_b[0]


def _proj_stages(x_ref, rows, ng_ref, win_ref, p_ref):
    h = _rms(x_ref[0, rows, :], ng_ref[...]).astype(_BF16)
    yield
    width = p_ref.shape[1] // PROJ_GROUPS
    for sec in range(PROJ_GROUPS):
        cols = slice(sec * width, (sec + 1) * width)
        p_ref[:, cols] = _dot(h, _weight(win_ref, cols))
        yield


def _zip_stages(main, side, every):
    for n, _ in enumerate(main, 1):
        if n % every == 0:
            next(side, None)
        yield


def _filler(stages):
    def fill(n):
        if n is None:
            for _ in stages:
                pass
        else:
            for _ in range(n):
                next(stages, None)
    return fill


def _log_gamma(hh):
    return math.log1p(-(2.0 ** (-5 - hh)))


def _retention_tables(rq_ref, rk_ref, rdec_ref, heads, tile):
    row = lax.broadcasted_iota(jnp.int32, rq_ref.shape[1:], 0).astype(_F32)
    ti = lax.broadcasted_iota(jnp.int32, (tile, tile), 0)
    si = lax.broadcasted_iota(jnp.int32, (tile, tile), 1)
    lag = jnp.maximum(ti - si, 0).astype(_F32)
    for hh in range(heads):
        lg = _log_gamma(hh)
        rq_ref[hh] = jnp.exp(lg * (row + 1.0))
        rk_ref[hh] = jnp.exp(lg * (tile - 1.0 - row))
        rdec_ref[hh] = jnp.where(ti >= si, jnp.exp(lg * lag), 0.0)


def _mixer(x_ref, rows, p_ref, fill, *, cos_ref, sin_ref, wout_ref, rg_ref, hgg_ref, lb_ref, fg_ref,
           y_ref, sret_ref, shg_ref, rq_ref, rk_ref, rdec_ref, heads, dh, tile, chunk):
    gw = heads * dh

    def p(sec, hh=None, rs=slice(None)):
        if hh is None:
            return p_ref[rs, sec * gw:(sec + 1) * gw]
        return p_ref[rs, sec * gw + hh * dh:sec * gw + (hh + 1) * dh]

    o_all = [None] * (2 * heads)

    cos, sin = cos_ref[rows, :], sin_ref[rows, :]

    def ret_issue(hh):
        q = _rotate(p(0, hh), cos, sin, dh // 2).astype(_BF16)
        kt = (_rotate(p(1, hh), cos, sin, dh // 2) * (dh ** -0.5)).astype(_BF16).T
        v = p(2, hh)
        s_old = sret_ref[0, hh]
        qk_qs = _dot(q, jnp.concatenate([kt, s_old.astype(_BF16)], axis=1))
        sret_ref[0, hh] = math.exp(tile * _log_gamma(hh)) * s_old + _dot(kt, (v * rk_ref[hh]).astype(_BF16))
        return hh, qk_qs, v

    def ret_finish(hh, qk_qs, v):
        sl = slice(hh * dh, (hh + 1) * dh)
        a = qk_qs[:, :tile] * rdec_ref[hh]
        o = qk_qs[:, tile:] * rq_ref[hh] + _dot(a.astype(_BF16), v.astype(_BF16))
        o_all[hh] = (_rms(o, rg_ref[:, sl]) * _silu(p(3, hh))).astype(_BF16)

    lbv = _lower_bound(lb_ref)
    ci = lax.broadcasted_iota(jnp.int32, (chunk, chunk), 0)
    cj = lax.broadcasted_iota(jnp.int32, (chunk, chunk), 1)
    cxor = ci ^ cj
    n_lev = chunk.bit_length() - 1
    diag = ci == cj
    lev_mask = [(ci > cj) & ((cxor >> lev) == 1) for lev in range(n_lev)]
    n_chunks = tile // chunk
    o_hg_chunks = [[None] * n_chunks for _ in range(heads)]

    def hg_issue(c, hh):
        rs = slice(c * chunk, (c + 1) * chunk)
        sl = slice(hh * dh, (hh + 1) * dh)
        f, gk = _forget_gate(p(5, hh, rs), lbv[:, sl])
        gq = _silu(p(4, hh, rs))
        gv = p(6, hh, rs).astype(_BF16)
        a = jnp.where(diag, jnp.sum(gq * gk, axis=-1, keepdims=True), 0.0).astype(_BF16)
        *levels, (q_top, k_top, f_tot) = _decay_levels(gq, gk, f)
        for lev, (q_lv, k_lv) in enumerate(levels):
            half = 1 << lev
            if half % MXU_ROW_GROUP:
                a = jnp.where(lev_mask[lev], _dot(q_lv, k_lv).astype(_BF16), a)
            else:
                ups = [slice(i, i + half) for i in range(half, chunk, 2 * half)]
                prod = _dot(jnp.concatenate([q_lv[u] for u in ups], axis=0), k_lv).astype(_BF16)
                parts, at = [], 0
                for j, u in enumerate(ups):
                    parts.append(a[at:u.start])
                    parts.append(jnp.where(lev_mask[lev][u], prod[j * half:(j + 1) * half], a[u]))
                    at = u.stop
                a = jnp.concatenate(parts, axis=0)
        s_old = shg_ref[0, hh]
        o_inter = _dot(q_top, s_old.astype(_BF16))
        f_col = jnp.broadcast_to(f_tot, (dh, dh)).T
        shg_ref[0, hh] = s_old * f_col + _dot_tn(k_top, gv)
        return c, hh, a, o_inter, gv

    def hg_finish(c, hh, a, o_inter, gv):
        o_hg_chunks[hh][c] = o_inter + _dot(a, gv)

    pend = None
    for hh in range(heads):
        ctx = ret_issue(hh)
        fill(1)
        if pend is not None:
            ret_finish(*pend)
        pend = ctx
    fill(2)
    ret_finish(*pend)
    pend = None
    for c in range(n_chunks):
        for hh in range(heads):
            ctx = hg_issue(c, hh)
            fill(1)
            if pend is not None:
                hg_finish(*pend)
            pend = ctx
    fill(2)
    hg_finish(*pend)
    fill(1)
    for hh in range(heads):
        sl = slice(hh * dh, (hh + 1) * dh)
        o = jnp.concatenate(o_hg_chunks[hh], axis=0)
        o_all[heads + hh] = (_rms(o, hgg_ref[:, sl]) * _silu(p(7, hh))).astype(_BF16)
    fill(None)
    o_cat = jnp.concatenate(o_all, axis=-1)
    y = x_ref[0, rows, :] + _dot(o_cat, _weight(wout_ref))
    y_ref[0, rows, :] = _rms(y, fg_ref[...])


def _decode_project(x_ref, cos_ref, sin_ref, win_ref, ng_ref, lb_ref, dec, *, heads, dh):
    rqt_s, rkt_s, rv_s, rgate_s, hqt_s, hkt_s, hi_s, hgate_s, hft_s = dec
    gw = heads * dh
    h = _rms(x_ref[...], ng_ref[...]).astype(_BF16)

    def proj(sec):
        return _dot(h, _weight(win_ref, slice(sec * gw, (sec + 1) * gw)))

    cos, sin = cos_ref[...], sin_ref[...]

    def rot(pj):
        return jnp.concatenate(
            [_rotate(pj[:, hh * dh:(hh + 1) * dh], cos, sin, dh // 2) for hh in range(heads)], axis=-1)

    rqt_s[...] = rot(proj(0)).T
    rkt_s[...] = (rot(proj(1)) * (dh ** -0.5)).T
    rv_s[...] = proj(2)
    rgate_s[...] = _silu(proj(3))
    f, kin = _forget_gate(proj(5), _lower_bound(lb_ref))
    hqt_s[...] = _silu(proj(4)).T
    hkt_s[...] = kin.T
    hft_s[...] = f.T
    hi_s[...] = proj(6)
    hgate_s[...] = _silu(proj(7))


def _decode_stages(step, x_ref, wout_ref, rg_ref, hgg_ref, fg_ref, sret_in, shg_in, y_ref, sret_out, shg_out, dec,
                   *, heads, dh, per_step):
    rqt_s, rkt_s, rv_s, rgate_s, hqt_s, hkt_s, hi_s, hgate_s, hft_s = dec
    nb = x_ref.shape[0]
    wrows = max(per_step, SUBLANES)
    share = wrows // per_step
    assert share in (1, 2) and wrows % SUBLANES == 0
    win = pl.ds(pl.multiple_of((step // share) * wrows, SUBLANES), wrows)
    base = (step % share) * per_step
    shift = (nb - step * per_step) % nb
    rolled = {}

    def col(ref_s, hh, bl):
        if id(ref_s) not in rolled:
            rolled[id(ref_s)] = pltpu.roll(ref_s[...], shift, 1)
        return jnp.broadcast_to(rolled[id(ref_s)][hh * dh:(hh + 1) * dh, bl:bl + 1], (dh, dh))

    def row_of(ref_s, sl, mine):
        return jnp.sum(jnp.where(mine, ref_s[win, sl], 0.0), axis=0, keepdims=True)

    rowi = lax.broadcasted_iota(jnp.int32, (wrows, dh), 0)
    o_ret = [jnp.zeros((wrows, dh), _F32)] * heads
    o_hg = [jnp.zeros((wrows, dh), _F32)] * heads
    for bl in range(per_step):
        mine = rowi == base + bl
        for hh in range(heads):
            sl = slice(hh * dh, (hh + 1) * dh)
            gamma = 1.0 - 2.0 ** (-5 - hh)
            s_new = gamma * sret_in[bl, hh] + col(rkt_s, hh, bl) * row_of(rv_s, sl, mine)
            sret_out[bl, hh] = s_new
            o = jnp.sum(s_new * col(rqt_s, hh, bl), axis=0, keepdims=True)
            o_ret[hh] = jnp.where(mine, o, o_ret[hh])
            g_new = col(hft_s, hh, bl) * shg_in[bl, hh] + col(hkt_s, hh, bl) * row_of(hi_s, sl, mine)
            shg_out[bl, hh] = g_new
            o = jnp.sum(g_new * col(hqt_s, hh, bl), axis=0, keepdims=True)
            o_hg[hh] = jnp.where(mine, o, o_hg[hh])
            yield

    def finish(o_rows, gain_ref, gate):
        return [_rms(o, gain_ref[:, hh * dh:(hh + 1) * dh]) * gate[:, hh * dh:(hh + 1) * dh]
                for hh, o in enumerate(o_rows)]

    o_cat = jnp.concatenate(
        finish(o_ret, rg_ref, rgate_s[win, :]) + finish(o_hg, hgg_ref, hgate_s[win, :]), axis=-1).astype(_BF16)
    y = _rms(x_ref[win, :] + _dot(o_cat, _weight(wout_ref)), fg_ref[...])
    if share == 1:
        y_ref[win, :] = y
    else:
        @pl.when(base == 0)
        def _():
            y_ref[win, :] = y

        @pl.when(base != 0)
        def _():
            keep = lax.broadcasted_iota(jnp.int32, y.shape, 0) >= base
            y_ref[win, :] = jnp.where(keep, y, y_ref[win, :])
    yield


def _mixer_kernel(xp_ref, xn_ref, cos_ref, sin_ref, win_ref, wout_ref, ng_ref, rg_ref, hgg_ref, lb_ref, fg_ref,
                  xs_ref, cos_s_ref, sin_s_ref, dret_in, dhg_in,
                  y_ref, sret_ref, shg_ref, ys_ref, dret_out, dhg_out,
                  pa_ref, pb_ref, rq_ref, rk_ref, rdec_ref, *dec,
                  heads, dh, tile, chunk, pairs, per_step):
    step = pl.program_id(0)
    pair = step % pairs
    rows_a, rows_b = slice(0, tile), slice(tile, 2 * tile)

    @pl.when(pair == 0)
    def _():
        sret_ref[...] = jnp.zeros_like(sret_ref)
        shg_ref[...] = jnp.zeros_like(shg_ref)

    @pl.when(step == 0)
    def _():
        _retention_tables(rq_ref, rk_ref, rdec_ref, heads, tile)
        _decode_project(xs_ref, cos_s_ref, sin_s_ref, win_ref, ng_ref, lb_ref, dec, heads=heads, dh=dh)
        for _ in _proj_stages(xp_ref, rows_a, ng_ref, win_ref, pa_ref):
            pass

    mixer = functools.partial(
        _mixer, cos_ref=cos_ref, sin_ref=sin_ref, wout_ref=wout_ref, rg_ref=rg_ref, hgg_ref=hgg_ref,
        lb_ref=lb_ref, fg_ref=fg_ref, y_ref=y_ref, sret_ref=sret_ref, shg_ref=shg_ref,
        rq_ref=rq_ref, rk_ref=rk_ref, rdec_ref=rdec_ref, heads=heads, dh=dh, tile=tile, chunk=chunk)
    decode = _decode_stages(step, xs_ref, wout_ref, rg_ref, hgg_ref, fg_ref, dret_in, dhg_in, ys_ref,
                            dret_out, dhg_out, dec, heads=heads, dh=dh, per_step=per_step)
    every = max(1, 2 * (1 + PROJ_GROUPS) // (per_step * heads + 1))
    mixer(xp_ref, rows_a, pa_ref,
          _filler(_zip_stages(_proj_stages(xp_ref, rows_b, ng_ref, win_ref, pb_ref), decode, every)))
    mixer(xp_ref, rows_b, pb_ref,
          _filler(_zip_stages(_proj_stages(xn_ref, rows_a, ng_ref, win_ref, pa_ref), decode, every)))
    for _ in decode:
        pass


def _mixer_call(x, cos, sin, win, wout, ng, rg, hgg, lb, fg, xs, cos_s, sin_s, dret, dhg):
    bsz, seq, d = x.shape
    nb, heads, dh, _ = dret.shape
    tile = min(PROMPT_TILE, seq // 2)
    chunk = min(HGRN_CHUNK, tile)
    assert seq % (2 * tile) == 0 and tile % chunk == 0 and chunk & (chunk - 1) == 0 and chunk >= 16
    gw = heads * dh
    pairs = seq // (2 * tile)
    steps = bsz * pairs
    per_step = nb // steps
    assert nb == per_step * steps and nb % 128 == 0

    def nxt(s):
        s1 = jnp.minimum(s + 1, steps - 1)
        return (s1 // pairs, 2 * (s1 % pairs), 0)

    const = lambda *shape: pl.BlockSpec(shape, lambda s: (0,) * len(shape))
    state_spec = pl.BlockSpec((1, heads, dh, dh), lambda s: (s // pairs, 0, 0, 0))
    dstate_spec = pl.BlockSpec((per_step, heads, dh, dh), lambda s: (s, 0, 0, 0))
    pair_spec = pl.BlockSpec((1, 2 * tile, d), lambda s: (s // pairs, s % pairs, 0))
    rope_spec = pl.BlockSpec((2 * tile, dh), lambda s: (s % pairs, 0))
    body = functools.partial(_mixer_kernel, heads=heads, dh=dh, tile=tile, chunk=chunk, pairs=pairs,
                             per_step=per_step)
    row_scratch = pltpu.VMEM((nb, gw), _F32)
    col_scratch = pltpu.VMEM((gw, nb), _F32)
    return pl.pallas_call(
        body,
        grid=(steps,),
        in_specs=[
            pair_spec, pl.BlockSpec((1, tile, d), nxt), rope_spec, rope_spec,
            const(d // 2, 8 * gw), const(gw, d), const(1, d), const(1, gw), const(1, gw),
            const(lb.shape[0], gw), const(1, d),
            const(nb, d), const(1, dh), const(1, dh), dstate_spec, dstate_spec,
        ],
        out_specs=[pair_spec, state_spec, state_spec, const(nb, d), dstate_spec, dstate_spec],
        out_shape=[
            jax.ShapeDtypeStruct((bsz, seq, d), _F32),
            jax.ShapeDtypeStruct((bsz, heads, dh, dh), _F32),
            jax.ShapeDtypeStruct((bsz, heads, dh, dh), _F32),
            jax.ShapeDtypeStruct((nb, d), _F32),
            jax.ShapeDtypeStruct(dret.shape, _F32),
            jax.ShapeDtypeStruct(dhg.shape, _F32),
        ],
        scratch_shapes=[pltpu.VMEM((tile, 8 * gw), _F32), pltpu.VMEM((tile, 8 * gw), _F32),
                        pltpu.VMEM((heads, tile, dh), _F32), pltpu.VMEM((heads, tile, dh), _F32),
                        pltpu.VMEM((heads, tile, tile), _F32)]
                       + [col_scratch, col_scratch, row_scratch, row_scratch, col_scratch, col_scratch,
                          row_scratch, row_scratch, col_scratch],
        compiler_params=pltpu.CompilerParams(
            dimension_semantics=("arbitrary",), vmem_limit_bytes=VMEM_LIMIT_BYTES),
        name="mixer_step",
    )(x, x, cos, sin, win, wout, ng, rg, hgg, lb, fg, xs, cos_s, sin_s, dret, dhg)


def _rope_tables(pos, dh):
    half = dh // 2
    lane = jnp.arange(dh)
    freqs = 1.0 / (ROPE_BASE ** ((lane % half).astype(_F32) / half))
    ang = pos[:, None] * freqs[None, :]
    return jnp.cos(ang), jnp.sin(ang) * jnp.where(lane < half, -1.0, 1.0)[None, :]


def kernel(x_prompt, x_sample, state_ret, state_hgrn, norm_g, w_in, ret_norm_g, hg_norm_g, hg_lb, w_out,
           final_norm_g):
    depth, _, heads, dh, _ = state_ret.shape
    assert depth == 1 and x_sample.shape[1] == 1
    lp = x_prompt.shape[1]
    win, wout = _pack_weights(w_in[0], w_out[0])
    ng, fg = norm_g[0][None, :], final_norm_g[None, :]
    rg, hgg = ret_norm_g[0][None, :], hg_norm_g[0][None, :]

    cos_p, sin_p = _rope_tables(jnp.arange(lp, dtype=_F32), dh)
    cos_s, sin_s = _rope_tables(PAST_LEN + jnp.arange(1, dtype=_F32), dh)
    y_p, ret_p, hg_p, y_s, ret_s, hg_s = _mixer_call(
        x_prompt, cos_p, sin_p, win, wout, ng, rg, hgg, hg_lb, fg,
        x_sample[:, 0, :], cos_s, sin_s, state_ret[0], state_hgrn[0])
    return (y_p, y_s[:, None, :], ret_p[None], hg_p[None], ret_s[None], hg_s[None])
```

```python
import functools
import math

import jax
import jax.numpy as jnp
from jax import lax
from jax.experimental import pallas as pl
from jax.experimental.pallas import tpu as pltpu

PAST_LEN = 16384
ROPE_BASE = 10000.0
NORM_EPS = 1e-6

PROMPT_TILE = 256
HGRN_CHUNK = 128
PROJ_GROUPS = 16
PACK_ROWS = 256
MXU_ROW_GROUP = 32
SUBLANES = 8
VMEM_LIMIT_BYTES = 56 * 1024 * 1024

_F32 = jnp.float32
_BF16 = jnp.bfloat16


def _rms(x, g):
    return x * lax.rsqrt(jnp.mean(x * x, axis=-1, keepdims=True) + NORM_EPS) * g


def _sigmoid(x):
    return 1.0 / (1.0 + jnp.exp(-x))


def _silu(x):
    return x * _sigmoid(x)


def _dot(a, b):
    return jnp.dot(a, b, preferred_element_type=_F32)


def _dot_nt(a, b):
    return lax.dot_general(a, b, (((1,), (1,)), ((), ())), preferred_element_type=_F32)


def _dot_tn(a, b):
    return lax.dot_general(a, b, (((0,), (0,)), ((), ())), preferred_element_type=_F32)


def _weight(w_ref, cols=slice(None)):
    return pltpu.bitcast(w_ref[:, cols], _BF16)


def _pack_kernel(*refs):
    n = len(refs) // 2
    for w_ref, o_ref in zip(refs[:n], refs[n:]):
        o_ref[...] = pltpu.bitcast(w_ref[...].astype(_BF16), jnp.int32)


def _pack_weights(*ws):
    steps = max(w.shape[0] for w in ws) // PACK_ROWS
    rows = [w.shape[0] // steps for w in ws]
    assert all(w.shape[0] == r * steps and r % 16 == 0 for w, r in zip(ws, rows))
    return pl.pallas_call(
        _pack_kernel,
        grid=(steps,),
        in_specs=[pl.BlockSpec((r, w.shape[1]), lambda i: (i, 0)) for w, r in zip(ws, rows)],
        out_specs=[pl.BlockSpec((r // 2, w.shape[1]), lambda i: (i, 0)) for w, r in zip(ws, rows)],
        out_shape=[jax.ShapeDtypeStruct((w.shape[0] // 2, w.shape[1]), jnp.int32) for w in ws],
        compiler_params=pltpu.CompilerParams(dimension_semantics=("arbitrary",)),
        name="pack_weights",
    )(*ws)


def _rotate(x, cos, sin_signed, half):
    return x * cos + pltpu.roll(x, half, 1) * sin_signed


def _lower_bound(lb_ref):
    a = lb_ref[...]
    m = jnp.max(a, axis=0, keepdims=True)
    e = jnp.exp(a - m)
    return e[0:1, :] / jnp.sum(e, axis=0, keepdims=True)


def _forget_gate(hf, lbv):
    z = jnp.exp(-jnp.abs(hf))
    r = 1.0 / (1.0 + z)
    zr = z * r
    pos = hf >= 0.0
    sig_p = jnp.where(pos, r, zr)
    sig_n = jnp.where(pos, zr, r)
    one_m = 1.0 - lbv
    return lbv + one_m * sig_p, one_m * sig_n


def _decay_levels(qa, kin, f):
    c, w = f.shape
    n_lev = c.bit_length() - 1
    rowi = lax.broadcasted_iota(jnp.int32, (c, w), 0)
    tot, q, k = f, qa * f, kin
    sub_levels = SUBLANES.bit_length() - 1
    for lev in range(sub_levels):
        half = 1 << lev
        up = (rowi & half) != 0
        yield jnp.where(up, q, 0.0).astype(_BF16), jnp.where(up, 0.0, k).T.astype(_BF16)
        sib = jnp.where(up, pltpu.roll(tot, half, 0), pltpu.roll(tot, c - half, 0))
        q = jnp.where(up, q * sib, q)
        k = jnp.where(up, k, k * sib)
        tot = tot * sib
    q_b = [q[i:i + SUBLANES] for i in range(0, c, SUBLANES)]
    k_b = [k[i:i + SUBLANES] for i in range(0, c, SUBLANES)]
    tot_b = [tot[i:i + 1] for i in range(0, c, SUBLANES)]
    for lev in range(sub_levels, n_lev):
        half = 1 << lev
        zero = jnp.zeros((half, w), _F32)
        yield (jnp.concatenate([q_b[i] if i % 2 else zero for i in range(len(q_b))], axis=0).astype(_BF16),
               jnp.concatenate([zero if i % 2 else k_b[i] for i in range(len(k_b))], axis=0).T.astype(_BF16))
        q_b = [jnp.concatenate([q_b[i], q_b[i + 1] * tot_b[i]], axis=0) for i in range(0, len(q_b), 2)]
        k_b = [jnp.concatenate([k_b[i] * tot_b[i + 1], k_b[i + 1]], axis=0) for i in range(0, len(k_b), 2)]
        tot_b = [tot_b[i] * tot_b[i + 1] for i in range(0, len(tot_b), 2)]
    yield q_b[0].astype(_BF16), k_b[0].astype(_BF16), tot_b[0]


def _proj_stages(x_ref, rows, ng_ref, win_ref, p_ref):
    h = _rms(x_ref[0, rows, :], ng_ref[...]).astype(_BF16)
    yield
    width = p_ref.shape[1] // PROJ_GROUPS
    for sec in range(PROJ_GROUPS):
        cols = slice(sec * width, (sec + 1) * width)
        p_ref[:, cols] = _dot(h, _weight(win_ref, cols))
        yield


def _zip_stages(main, side, every):
    for n, _ in enumerate(main, 1):
        if n % every == 0:
            next(side, None)
        yield


def _filler(stages):
    def fill(n):
        if n is None:
            for _ in stages:
                pass
        else:
            for _ in range(n):
                next(stages, None)
    return fill


def _log_gamma(hh):
    return math.log1p(-(2.0 ** (-5 - hh)))


def _retention_tables(rq_ref, rk_ref, rdec_ref, heads, tile):
    row = lax.broadcasted_iota(jnp.int32, rq_ref.shape[1:], 0).astype(_F32)
    ti = lax.broadcasted_iota(jnp.int32, (tile, tile), 0)
    si = lax.broadcasted_iota(jnp.int32, (tile, tile), 1)
    lag = jnp.maximum(ti - si, 0).astype(_F32)
    for hh in range(heads):
        lg = _log_gamma(hh)
        rq_ref[hh] = jnp.exp(lg * (row + 1.0))
        rk_ref[hh] = jnp.exp(lg * (tile - 1.0 - row))
        rdec_ref[hh] = jnp.where(ti >= si, jnp.exp(lg * lag), 0.0)


def _mixer(x_ref, rows, rope_rows, p_ref, fill, *, cos_ref, sin_ref, wout_ref, rg_ref, hgg_ref, lb_ref, fg_ref,
           y_ref, sret_ref, shg_ref, rq_ref, rk_ref, rdec_ref, heads, dh, tile, chunk):
    gw = heads * dh

    def p(sec, hh=None, rs=slice(None)):
        if hh is None:
            return p_ref[rs, sec * gw:(sec + 1) * gw]
        return p_ref[rs, sec * gw + hh * dh:sec * gw + (hh + 1) * dh]

    o_all = [None] * (2 * heads)

    cos, sin = cos_ref[rope_rows, :], sin_ref[rope_rows, :]

    def ret_issue(hh):
        q = _rotate(p(0, hh), cos, sin, dh // 2).astype(_BF16)
        kt = (_rotate(p(1, hh), cos, sin, dh // 2) * (dh ** -0.5)).astype(_BF16).T
        v = p(2, hh)
        s_old = sret_ref[0, hh]
        qk_qs = _dot(q, jnp.concatenate([kt, s_old.astype(_BF16)], axis=1))
        sret_ref[0, hh] = math.exp(tile * _log_gamma(hh)) * s_old + _dot(kt, (v * rk_ref[hh]).astype(_BF16))
        return hh, qk_qs, v

    def ret_finish(hh, qk_qs, v):
        sl = slice(hh * dh, (hh + 1) * dh)
        a = qk_qs[:, :tile] * rdec_ref[hh]
        o = qk_qs[:, tile:] * rq_ref[hh] + _dot(a.astype(_BF16), v.astype(_BF16))
        o_all[hh] = (_rms(o, rg_ref[:, sl]) * _silu(p(3, hh))).astype(_BF16)

    lbv = _lower_bound(lb_ref)
    ci = lax.broadcasted_iota(jnp.int32, (chunk, chunk), 0)
    cj = lax.broadcasted_iota(jnp.int32, (chunk, chunk), 1)
    cxor = ci ^ cj
    n_lev = chunk.bit_length() - 1
    diag = ci == cj
    lev_mask = [(ci > cj) & ((cxor >> lev) == 1) for lev in range(n_lev)]
    n_chunks = tile // chunk
    o_hg_chunks = [[None] * n_chunks for _ in range(heads)]

    def hg_issue(c, hh):
        rs = slice(c * chunk, (c + 1) * chunk)
        sl = slice(hh * dh, (hh + 1) * dh)
        f, gk = _forget_gate(p(5, hh, rs), lbv[:, sl])
        gq = _silu(p(4, hh, rs))
        gv = p(6, hh, rs).astype(_BF16)
        a = jnp.where(diag, jnp.sum(gq * gk, axis=-1, keepdims=True), 0.0).astype(_BF16)
        *levels, (q_top, k_top, f_tot) = _decay_levels(gq, gk, f)
        for lev, (q_lv, k_lv) in enumerate(levels):
            half = 1 << lev
            if half % MXU_ROW_GROUP:
                a = jnp.where(lev_mask[lev], _dot(q_lv, k_lv).astype(_BF16), a)
            else:
                ups = [slice(i, i + half) for i in range(half, chunk, 2 * half)]
                prod = _dot(jnp.concatenate([q_lv[u] for u in ups], axis=0), k_lv).astype(_BF16)
                parts, at = [], 0
                for j, u in enumerate(ups):
                    parts.append(a[at:u.start])
                    parts.append(jnp.where(lev_mask[lev][u], prod[j * half:(j + 1) * half], a[u]))
                    at = u.stop
                a = jnp.concatenate(parts, axis=0)
        s_old = shg_ref[0, hh]
        o_inter = _dot(q_top, s_old.astype(_BF16))
        f_col = jnp.broadcast_to(f_tot, (dh, dh)).T
        shg_ref[0, hh] = s_old * f_col + _dot_tn(k_top, gv)
        return c, hh, a, o_inter, gv

    def hg_finish(c, hh, a, o_inter, gv):
        o_hg_chunks[hh][c] = o_inter + _dot(a, gv)

    pend = None
    for hh in range(heads):
        ctx = ret_issue(hh)
        fill(1)
        if pend is not None:
            ret_finish(*pend)
        pend = ctx
    fill(2)
    ret_finish(*pend)
    pend = None
    for c in range(n_chunks):
        for hh in range(heads):
            ctx = hg_issue(c, hh)
            fill(1)
            if pend is not None:
                hg_finish(*pend)
            pend = ctx
    fill(2)
    hg_finish(*pend)
    fill(1)
    for hh in range(heads):
        sl = slice(hh * dh, (hh + 1) * dh)
        o = jnp.concatenate(o_hg_chunks[hh], axis=0)
        o_all[heads + hh] = (_rms(o, hgg_ref[:, sl]) * _silu(p(7, hh))).astype(_BF16)
    fill(None)
    o_cat = jnp.concatenate(o_all, axis=-1)
    y = x_ref[0, rows, :] + _dot(o_cat, _weight(wout_ref))
    y_ref[0, rows, :] = _rms(y, fg_ref[...])


def _decode_project(x_ref, cos_ref, sin_ref, win_ref, ng_ref, lb_ref, dec, *, heads, dh):
    rqt_s, rkt_s, rv_s, rgate_s, hqt_s, hkt_s, hi_s, hgate_s, hft_s = dec
    gw = heads * dh
    h = _rms(x_ref[...], ng_ref[...]).astype(_BF16)

    def proj(sec):
        return _dot(h, _weight(win_ref, slice(sec * gw, (sec + 1) * gw)))

    cos, sin = cos_ref[...], sin_ref[...]

    def rot(pj):
        return jnp.concatenate(
            [_rotate(pj[:, hh * dh:(hh + 1) * dh], cos, sin, dh // 2) for hh in range(heads)], axis=-1)

    rqt_s[...] = rot(proj(0)).T
    rkt_s[...] = (rot(proj(1)) * (dh ** -0.5)).T
    rv_s[...] = proj(2)
    rgate_s[...] = _silu(proj(3))
    f, kin = _forget_gate(proj(5), _lower_bound(lb_ref))
    hqt_s[...] = _silu(proj(4)).T
    hkt_s[...] = kin.T
    hft_s[...] = f.T
    hi_s[...] = proj(6)
    hgate_s[...] = _silu(proj(7))


def _decode_stages(step, x_ref, wout_ref, rg_ref, hgg_ref, fg_ref, sret_in, shg_in, y_ref, sret_out, shg_out, dec,
                   *, heads, dh, per_step):
    rqt_s, rkt_s, rv_s, rgate_s, hqt_s, hkt_s, hi_s, hgate_s, hft_s = dec
    nb = x_ref.shape[0]
    wrows = max(per_step, SUBLANES)
    share = wrows // per_step
    assert share in (1, 2) and wrows % SUBLANES == 0
    win = pl.ds(pl.multiple_of((step // share) * wrows, SUBLANES), wrows)
    base = (step % share) * per_step
    shift = (nb - step * per_step) % nb
    rolled = {}

    def col(ref_s, hh, bl):
        if id(ref_s) not in rolled:
            rolled[id(ref_s)] = pltpu.roll(ref_s[...], shift, 1)
        return jnp.broadcast_to(rolled[id(ref_s)][hh * dh:(hh + 1) * dh, bl:bl + 1], (dh, dh))

    def row_of(ref_s, sl, mine):
        return jnp.sum(jnp.where(mine, ref_s[win, sl], 0.0), axis=0, keepdims=True)

    rowi = lax.broadcasted_iota(jnp.int32, (wrows, dh), 0)
    o_ret = [jnp.zeros((wrows, dh), _F32)] * heads
    o_hg = [jnp.zeros((wrows, dh), _F32)] * heads
    for bl in range(per_step):
        mine = rowi == base + bl
        for hh in range(heads):
            sl = slice(hh * dh, (hh + 1) * dh)
            gamma = 1.0 - 2.0 ** (-5 - hh)
            s_new = gamma * sret_in[bl, hh] + col(rkt_s, hh, bl) * row_of(rv_s, sl, mine)
            sret_out[bl, hh] = s_new
            o = jnp.sum(s_new * col(rqt_s, hh, bl), axis=0, keepdims=True)
            o_ret[hh] = jnp.where(mine, o, o_ret[hh])
            g_new = col(hft_s, hh, bl) * shg_in[bl, hh] + col(hkt_s, hh, bl) * row_of(hi_s, sl, mine)
            shg_out[bl, hh] = g_new
            o = jnp.sum(g_new * col(hqt_s, hh, bl), axis=0, keepdims=True)
            o_hg[hh] = jnp.where(mine, o, o_hg[hh])
            yield

    def finish(o_rows, gain_ref, gate):
        return [_rms(o, gain_ref[:, hh * dh:(hh + 1) * dh]) * gate[:, hh * dh:(hh + 1) * dh]
                for hh, o in enumerate(o_rows)]

    o_cat = jnp.concatenate(
        finish(o_ret, rg_ref, rgate_s[win, :]) + finish(o_hg, hgg_ref, hgate_s[win, :]), axis=-1).astype(_BF16)
    y = _rms(x_ref[win, :] + _dot(o_cat, _weight(wout_ref)), fg_ref[...])
    if share == 1:
        y_ref[win, :] = y
    else:
        @pl.when(base == 0)
        def _():
            y_ref[win, :] = y

        @pl.when(base != 0)
        def _():
            keep = lax.broadcasted_iota(jnp.int32, y.shape, 0) >= base
            y_ref[win, :] = jnp.where(keep, y, y_ref[win, :])
    yield


def _mixer_kernel(xp_ref, xn_ref, cos_ref, sin_ref, win_ref, wout_ref, ng_ref, rg_ref, hgg_ref, lb_ref, fg_ref,
                  xs_ref, cos_s_ref, sin_s_ref, dret_in, dhg_in,
                  y_ref, sret_ref, shg_ref, ys_ref, dret_out, dhg_out,
                  pa_ref, pb_ref, rq_ref, rk_ref, rdec_ref, *dec,
                  heads, dh, tile, chunk, pairs, per_step):
    step = pl.program_id(0)
    pair = step % pairs
    rows_a, rows_b = slice(0, tile), slice(tile, 2 * tile)

    @pl.when(pair == 0)
    def _():
        sret_ref[...] = jnp.zeros_like(sret_ref)
        shg_ref[...] = jnp.zeros_like(shg_ref)

    @pl.when(step == 0)
    def _():
        _retention_tables(rq_ref, rk_ref, rdec_ref, heads, tile)
        _decode_project(xs_ref, cos_s_ref, sin_s_ref, win_ref, ng_ref, lb_ref, dec, heads=heads, dh=dh)
        for _ in _proj_stages(xp_ref, rows_a, ng_ref, win_ref, pa_ref):
            pass

    mixer = functools.partial(
        _mixer, cos_ref=cos_ref, sin_ref=sin_ref, wout_ref=wout_ref, rg_ref=rg_ref, hgg_ref=hgg_ref,
        lb_ref=lb_ref, fg_ref=fg_ref, y_ref=y_ref, sret_ref=sret_ref, shg_ref=shg_ref,
        rq_ref=rq_ref, rk_ref=rk_ref, rdec_ref=rdec_ref, heads=heads, dh=dh, tile=tile, chunk=chunk)
    decode = _decode_stages(step, xs_ref, wout_ref, rg_ref, hgg_ref, fg_ref, dret_in, dhg_in, ys_ref,
                            dret_out, dhg_out, dec, heads=heads, dh=dh, per_step=per_step)
    every = max(1, 2 * (1 + PROJ_GROUPS) // (per_step * heads + 1))
    rope_a = pl.ds(pl.multiple_of(pair * 2 * tile, tile), tile)
    rope_b = pl.ds(pl.multiple_of(pair * 2 * tile + tile, tile), tile)
    mixer(xp_ref, rows_a, rope_a, pa_ref,
          _filler(_zip_stages(_proj_stages(xp_ref, rows_b, ng_ref, win_ref, pb_ref), decode, every)))
    mixer(xp_ref, rows_b, rope_b, pb_ref,
          _filler(_zip_stages(_proj_stages(xn_ref, rows_a, ng_ref, win_ref, pa_ref), decode, every)))
    for _ in decode:
        pass


def _mixer_call(x, cos, sin, win, wout, ng, rg, hgg, lb, fg, xs, cos_s, sin_s, dret, dhg):
    bsz, seq, d = x.shape
    nb, heads, dh, _ = dret.shape
    tile = min(PROMPT_TILE, seq // 2)
    chunk = min(HGRN_CHUNK, tile)
    assert seq % (2 * tile) == 0 and tile % chunk == 0 and chunk & (chunk - 1) == 0 and chunk >= 16
    gw = heads * dh
    pairs = seq // (2 * tile)
    steps = bsz * pairs
    per_step = nb // steps
    assert nb == per_step * steps and nb % 128 == 0

    def nxt(s):
        s1 = jnp.minimum(s + 1, steps - 1)
        return (s1 // pairs, 2 * (s1 % pairs), 0)

    const = lambda *shape: pl.BlockSpec(shape, lambda s: (0,) * len(shape))
    state_spec = pl.BlockSpec((1, heads, dh, dh), lambda s: (s // pairs, 0, 0, 0))
    dstate_spec = pl.BlockSpec((per_step, heads, dh, dh), lambda s: (s, 0, 0, 0))
    pair_spec = pl.BlockSpec((1, 2 * tile, d), lambda s: (s // pairs, s % pairs, 0))
    rope_spec = pl.BlockSpec((seq, dh), lambda s: (0, 0))
    body = functools.partial(_mixer_kernel, heads=heads, dh=dh, tile=tile, chunk=chunk, pairs=pairs,
                             per_step=per_step)
    row_scratch = pltpu.VMEM((nb, gw), _F32)
    col_scratch = pltpu.VMEM((gw, nb), _F32)
    return pl.pallas_call(
        body,
        grid=(steps,),
        in_specs=[
            pair_spec, pl.BlockSpec((1, tile, d), nxt), rope_spec, rope_spec,
            const(d // 2, 8 * gw), const(gw, d), const(1, d), const(1, gw), const(1, gw),
            const(lb.shape[0], gw), const(1, d),
            const(nb, d), const(1, dh), const(1, dh), dstate_spec, dstate_spec,
        ],
        out_specs=[pair_spec, state_spec, state_spec, const(nb, d), dstate_spec, dstate_spec],
        out_shape=[
            jax.ShapeDtypeStruct((bsz, seq, d), _F32),
            jax.ShapeDtypeStruct((bsz, heads, dh, dh), _F32),
            jax.ShapeDtypeStruct((bsz, heads, dh, dh), _F32),
            jax.ShapeDtypeStruct((nb, d), _F32),
            jax.ShapeDtypeStruct(dret.shape, _F32),
            jax.ShapeDtypeStruct(dhg.shape, _F32),
        ],
        scratch_shapes=[pltpu.VMEM((tile, 8 * gw), _F32), pltpu.VMEM((tile, 8 * gw), _F32),
                        pltpu.VMEM((heads, tile, dh), _F32), pltpu.VMEM((heads, tile, dh), _F32),
                        pltpu.VMEM((heads, tile, tile), _F32)]
                       + [col_scratch, col_scratch, row_scratch, row_scratch, col_scratch, col_scratch,
                          row_scratch, row_scratch, col_scratch],
        compiler_params=pltpu.CompilerParams(
            dimension_semantics=("arbitrary",), vmem_limit_bytes=VMEM_LIMIT_BYTES),
        name="mixer_step",
    )(x, x, cos, sin, win, wout, ng, rg, hgg, lb, fg, xs, cos_s, sin_s, dret, dhg)


def _rope_tables(pos, dh):
    half = dh // 2
    lane = jnp.arange(dh)
    freqs = 1.0 / (ROPE_BASE ** ((lane % half).astype(_F32) / half))
    ang = pos[:, None] * freqs[None, :]
    return jnp.cos(ang), jnp.sin(ang) * jnp.where(lane < half, -1.0, 1.0)[None, :]


def kernel(x_prompt, x_sample, state_ret, state_hgrn, norm_g, w_in, ret_norm_g, hg_norm_g, hg_lb, w_out,
           final_norm_g):
    depth, _, heads, dh, _ = state_ret.shape
    assert depth == 1 and x_sample.shape[1] == 1
    lp = x_prompt.shape[1]
    win, wout = _pack_weights(w_in[0], w_out[0])
    ng, fg = norm_g[0][None, :], final_norm_g[None, :]
    rg, hgg = ret_norm_g[0][None, :], hg_norm_g[0][None, :]

    cos_p, sin_p = _rope_tables(jnp.arange(lp, dtype=_F32), dh)
    cos_s, sin_s = _rope_tables(PAST_LEN + jnp.arange(1, dtype=_F32), dh)
    y_p, ret_p, hg_p, y_s, ret_s, hg_s = _mixer_call(
        x_prompt, cos_p, sin_p, win, wout, ng, rg, hgg, hg_lb, fg,
        x_sample[:, 0, :], cos_s, sin_s, state_ret[0], state_hgrn[0])
    return (y_p, y_s[:, None, :], ret_p[None], hg_p[None], ret_s[None], hg_s[None])
```

```python
import functools
import math

import jax
import jax.numpy as jnp
from jax import lax
from jax.experimental import pallas as pl
from jax.experimental.pallas import tpu as pltpu

PAST_LEN = 16384
ROPE_BASE = 10000.0
NORM_EPS = 1e-6

PROMPT_TILE = 256
HGRN_CHUNK = 128
PROJ_GROUPS = 16
PACK_ROWS = 256
MXU_ROW_GROUP = 32
SUBLANES = 8
VMEM_LIMIT_BYTES = 56 * 1024 * 1024

_F32 = jnp.float32
_BF16 = jnp.bfloat16


def _rms(x, g):
    return x * lax.rsqrt(jnp.mean(x * x, axis=-1, keepdims=True) + NORM_EPS) * g


def _sigmoid(x):
    return 1.0 / (1.0 + jnp.exp(-x))


def _silu(x):
    return x * _sigmoid(x)


def _dot(a, b):
    return jnp.dot(a, b, preferred_element_type=_F32)


def _dot_nt(a, b):
    return lax.dot_general(a, b, (((1,), (1,)), ((), ())), preferred_element_type=_F32)


def _dot_tn(a, b):
    return lax.dot_general(a, b, (((0,), (0,)), ((), ())), preferred_element_type=_F32)


def _weight(w_ref, cols=slice(None)):
    return pltpu.bitcast(w_ref[:, cols], _BF16)


def _pack_kernel(*refs):
    n = len(refs) // 2
    for w_ref, o_ref in zip(refs[:n], refs[n:]):
        o_ref[...] = pltpu.bitcast(w_ref[...].astype(_BF16), jnp.int32)


def _pack_weights(*ws):
    steps = max(w.shape[0] for w in ws) // PACK_ROWS
    rows = [w.shape[0] // steps for w in ws]
    assert all(w.shape[0] == r * steps and r % 16 == 0 for w, r in zip(ws, rows))
    return pl.pallas_call(
        _pack_kernel,
        grid=(steps,),
        in_specs=[pl.BlockSpec((r, w.shape[1]), lambda i: (i, 0)) for w, r in zip(ws, rows)],
        out_specs=[pl.BlockSpec((r // 2, w.shape[1]), lambda i: (i, 0)) for w, r in zip(ws, rows)],
        out_shape=[jax.ShapeDtypeStruct((w.shape[0] // 2, w.shape[1]), jnp.int32) for w in ws],
        compiler_params=pltpu.CompilerParams(dimension_semantics=("arbitrary",)),
        name="pack_weights",
    )(*ws)


def _rotate(x, cos, sin_signed, half):
    return x * cos + pltpu.roll(x, half, 1) * sin_signed


def _lower_bound(lb_ref):
    a = lb_ref[...]
    m = jnp.max(a, axis=0, keepdims=True)
    e = jnp.exp(a - m)
    return e[0:1, :] / jnp.sum(e, axis=0, keepdims=True)


def _forget_gate(hf, lbv):
    z = jnp.exp(-jnp.abs(hf))
    r = 1.0 / (1.0 + z)
    zr = z * r
    pos = hf >= 0.0
    sig_p = jnp.where(pos, r, zr)
    sig_n = jnp.where(pos, zr, r)
    one_m = 1.0 - lbv
    return lbv + one_m * sig_p, one_m * sig_n


def _decay_levels(qa, kin, f):
    c, w = f.shape
    n_lev = c.bit_length() - 1
    rowi = lax.broadcasted_iota(jnp.int32, (c, w), 0)
    tot, q, k = f, qa * f, kin
    sub_levels = SUBLANES.bit_length() - 1
    for lev in range(sub_levels):
        half = 1 << lev
        up = (rowi & half) != 0
        yield jnp.where(up, q, 0.0).astype(_BF16), jnp.where(up, 0.0, k).T.astype(_BF16)
        sib = jnp.where(up, pltpu.roll(tot, half, 0), pltpu.roll(tot, c - half, 0))
        q = jnp.where(up, q * sib, q)
        k = jnp.where(up, k, k * sib)
        tot = tot * sib
    q_b = [q[i:i + SUBLANES] for i in range(0, c, SUBLANES)]
    k_b = [k[i:i + SUBLANES] for i in range(0, c, SUBLANES)]
    tot_b = [tot[i:i + 1] for i in range(0, c, SUBLANES)]
    for lev in range(sub_levels, n_lev):
        half = 1 << lev
        zero = jnp.zeros((half, w), _F32)
        yield (jnp.concatenate([q_b[i] if i % 2 else zero for i in range(len(q_b))], axis=0).astype(_BF16),
               jnp.concatenate([zero if i % 2 else k_b[i] for i in range(len(k_b))], axis=0).T.astype(_BF16))
        q_b = [jnp.concatenate([q_b[i], q_b[i + 1] * tot_b[i]], axis=0) for i in range(0, len(q_b), 2)]
        k_b = [jnp.concatenate([k_b[i] * tot_b[i + 1], k_b[i + 1]], axis=0) for i in range(0, len(k_b), 2)]
        tot_b = [tot_b[i] * tot_b[i + 1] for i in range(0, len(tot_b), 2)]
    yield q_b[0].astype(_BF16), k_b[0].astype(_BF16), tot_b[0]


def _proj_stages(x_ref, rows, ng_ref, win_ref, p_ref):
    h = _rms(x_ref[0, rows, :], ng_ref[...]).astype(_BF16)
    yield
    width = p_ref.shape[1] // PROJ_GROUPS
    for sec in range(PROJ_GROUPS):
        cols = slice(sec * width, (sec + 1) * width)
        p_ref[:, cols] = _dot(h, _weight(win_ref, cols))
        yield


def _zip_stages(main, side, every):
    for n, _ in enumerate(main, 1):
        if n % every == 0:
            next(side, None)
        yield


def _filler(stages):
    def fill(n):
        if n is None:
            for _ in stages:
                pass
        else:
            for _ in range(n):
                next(stages, None)
    return fill


def _log_gamma(hh):
    return math.log1p(-(2.0 ** (-5 - hh)))


def _retention_tables(rq_ref, rk_ref, rdec_ref, heads, tile):
    row = lax.broadcasted_iota(jnp.int32, rq_ref.shape[1:], 0).astype(_F32)
    ti = lax.broadcasted_iota(jnp.int32, (tile, tile), 0)
    si = lax.broadcasted_iota(jnp.int32, (tile, tile), 1)
    lag = jnp.maximum(ti - si, 0).astype(_F32)
    for hh in range(heads):
        lg = _log_gamma(hh)
        rq_ref[hh] = jnp.exp(lg * (row + 1.0))
        rk_ref[hh] = jnp.exp(lg * (tile - 1.0 - row))
        rdec_ref[hh] = jnp.where(ti >= si, jnp.exp(lg * lag), 0.0)


def _mixer(x_ref, rows, p_ref, fill, *, cos_ref, sin_ref, wout_ref, rg_ref, hgg_ref, lb_ref, fg_ref,
           y_ref, sret_ref, shg_ref, rq_ref, rk_ref, rdec_ref, heads, dh, tile, chunk):
    gw = heads * dh

    def p(sec, hh=None, rs=slice(None)):
        if hh is None:
            return p_ref[rs, sec * gw:(sec + 1) * gw]
        return p_ref[rs, sec * gw + hh * dh:sec * gw + (hh + 1) * dh]

    o_all = [None] * (2 * heads)

    cos, sin = cos_ref[rows, :], sin_ref[rows, :]

    def ret_issue(hh):
        q = _rotate(p(0, hh), cos, sin, dh // 2).astype(_BF16)
        kt = (_rotate(p(1, hh), cos, sin, dh // 2) * (dh ** -0.5)).astype(_BF16).T
        v = p(2, hh)
        s_old = sret_ref[0, hh]
        qk_qs = _dot(q, jnp.concatenate([kt, s_old.astype(_BF16)], axis=1))
        sret_ref[0, hh] = math.exp(tile * _log_gamma(hh)) * s_old + _dot(kt, (v * rk_ref[hh]).astype(_BF16))
        return hh, qk_qs, v

    def ret_finish(hh, qk_qs, v):
        sl = slice(hh * dh, (hh + 1) * dh)
        a = qk_qs[:, :tile] * rdec_ref[hh]
        o = qk_qs[:, tile:] * rq_ref[hh] + _dot(a.astype(_BF16), v.astype(_BF16))
        o_all[hh] = (_rms(o, rg_ref[:, sl]) * _silu(p(3, hh))).astype(_BF16)

    lbv = _lower_bound(lb_ref)
    ci = lax.broadcasted_iota(jnp.int32, (chunk, chunk), 0)
    cj = lax.broadcasted_iota(jnp.int32, (chunk, chunk), 1)
    cxor = ci ^ cj
    n_lev = chunk.bit_length() - 1
    diag = ci == cj
    lev_mask = [(ci > cj) & ((cxor >> lev) == 1) for lev in range(n_lev)]
    n_chunks = tile // chunk
    o_hg_chunks = [[None] * n_chunks for _ in range(heads)]

    def hg_issue(c, hh):
        rs = slice(c * chunk, (c + 1) * chunk)
        sl = slice(hh * dh, (hh + 1) * dh)
        f, gk = _forget_gate(p(5, hh, rs), lbv[:, sl])
        gq = _silu(p(4, hh, rs))
        gv = p(6, hh, rs).astype(_BF16)
        a = jnp.where(diag, jnp.sum(gq * gk, axis=-1, keepdims=True), 0.0).astype(_BF16)
        *levels, (q_top, k_top, f_tot) = _decay_levels(gq, gk, f)
        for lev, (q_lv, k_lv) in enumerate(levels):
            half = 1 << lev
            if half % MXU_ROW_GROUP:
                a = jnp.where(lev_mask[lev], _dot(q_lv, k_lv).astype(_BF16), a)
            else:
                ups = [slice(i, i + half) for i in range(half, chunk, 2 * half)]
                prod = _dot(jnp.concatenate([q_lv[u] for u in ups], axis=0), k_lv).astype(_BF16)
                parts, at = [], 0
                for j, u in enumerate(ups):
                    parts.append(a[at:u.start])
                    parts.append(jnp.where(lev_mask[lev][u], prod[j * half:(j + 1) * half], a[u]))
                    at = u.stop
                a = jnp.concatenate(parts, axis=0)
        s_old = shg_ref[0, hh]
        o_inter = _dot(q_top, s_old.astype(_BF16))
        f_col = jnp.broadcast_to(f_tot, (dh, dh)).T
        shg_ref[0, hh] = s_old * f_col + _dot_tn(k_top, gv)
        return c, hh, a, o_inter, gv

    def hg_finish(c, hh, a, o_inter, gv):
        o_hg_chunks[hh][c] = o_inter + _dot(a, gv)

    pend = None
    for hh in range(heads):
        ctx = ret_issue(hh)
        fill(1)
        if pend is not None:
            ret_finish(*pend)
        pend = ctx
    fill(2)
    ret_finish(*pend)
    pend = None
    for c in range(n_chunks):
        for hh in range(heads):
            ctx = hg_issue(c, hh)
            fill(1)
            if pend is not None:
                hg_finish(*pend)
            pend = ctx
    fill(2)
    hg_finish(*pend)
    fill(1)
    for hh in range(heads):
        sl = slice(hh * dh, (hh + 1) * dh)
        o = jnp.concatenate(o_hg_chunks[hh], axis=0)
        o_all[heads + hh] = (_rms(o, hgg_ref[:, sl]) * _silu(p(7, hh))).astype(_BF16)
    fill(None)
    o_cat = jnp.concatenate(o_all, axis=-1)
    y = x_ref[0, rows, :] + _dot(o_cat, _weight(wout_ref))
    y_ref[0, rows, :] = _rms(y, fg_ref[...])


def _decode_project(x_ref, cos_ref, sin_ref, win_ref, ng_ref, lb_ref, dec, *, heads, dh):
    rqt_s, rkt_s, rv_s, rgate_s, hqt_s, hkt_s, hi_s, hgate_s, hft_s = dec
    gw = heads * dh
    h = _rms(x_ref[...], ng_ref[...]).astype(_BF16)

    def proj(sec):
        return _dot(h, _weight(win_ref, slice(sec * gw, (sec + 1) * gw)))

    cos, sin = cos_ref[...], sin_ref[...]

    def rot(pj):
        return jnp.concatenate(
            [_rotate(pj[:, hh * dh:(hh + 1) * dh], cos, sin, dh // 2) for hh in range(heads)], axis=-1)

    rqt_s[...] = rot(proj(0)).T
    rkt_s[...] = (rot(proj(1)) * (dh ** -0.5)).T
    rv_s[...] = proj(2)
    rgate_s[...] = _silu(proj(3))
    f, kin = _forget_gate(proj(5), _lower_bound(lb_ref))
    hqt_s[...] = _silu(proj(4)).T
    hkt_s[...] = kin.T
    hft_s[...] = f.T
    hi_s[...] = proj(6)
    hgate_s[...] = _silu(proj(7))


def _decode_stages(step, x_ref, wout_ref, rg_ref, hgg_ref, fg_ref, sret_in, shg_in, y_ref, sret_out, shg_out, dec,
                   *, heads, dh, per_step):
    rqt_s, rkt_s, rv_s, rgate_s, hqt_s, hkt_s, hi_s, hgate_s, hft_s = dec
    nb = x_ref.shape[0]
    wrows = max(per_step, SUBLANES)
    share = wrows // per_step
    assert share in (1, 2) and wrows % SUBLANES == 0
    win = pl.ds(pl.multiple_of((step // share) * wrows, SUBLANES), wrows)
    base = (step % share) * per_step
    shift = (nb - step * per_step) % nb
    rolled = {}

    def col(ref_s, hh, bl):
        if id(ref_s) not in rolled:
            rolled[id(ref_s)] = pltpu.roll(ref_s[...], shift, 1)
        return jnp.broadcast_to(rolled[id(ref_s)][hh * dh:(hh + 1) * dh, bl:bl + 1], (dh, dh))

    def row_of(ref_s, sl, mine):
        return jnp.sum(jnp.where(mine, ref_s[win, sl], 0.0), axis=0, keepdims=True)

    rowi = lax.broadcasted_iota(jnp.int32, (wrows, dh), 0)
    o_ret = [jnp.zeros((wrows, dh), _F32)] * heads
    o_hg = [jnp.zeros((wrows, dh), _F32)] * heads
    for bl in range(per_step):
        mine = rowi == base + bl
        for hh in range(heads):
            sl = slice(hh * dh, (hh + 1) * dh)
            gamma = 1.0 - 2.0 ** (-5 - hh)
            s_new = gamma * sret_in[bl, hh] + col(rkt_s, hh, bl) * row_of(rv_s, sl, mine)
            sret_out[bl, hh] = s_new
            o = jnp.sum(s_new * col(rqt_s, hh, bl), axis=0, keepdims=True)
            o_ret[hh] = jnp.where(mine, o, o_ret[hh])
            g_new = col(hft_s, hh, bl) * shg_in[bl, hh] + col(hkt_s, hh, bl) * row_of(hi_s, sl, mine)
            shg_out[bl, hh] = g_new
            o = jnp.sum(g_new * col(hqt_s, hh, bl), axis=0, keepdims=True)
            o_hg[hh] = jnp.where(mine, o, o_hg[hh])
            yield

    def finish(o_rows, gain_ref, gate):
        return [_rms(o, gain_ref[:, hh * dh:(hh + 1) * dh]) * gate[:, hh * dh:(hh + 1) * dh]
                for hh, o in enumerate(o_rows)]

    o_cat = jnp.concatenate(
        finish(o_ret, rg_ref, rgate_s[win, :]) + finish(o_hg, hgg_ref, hgate_s[win, :]), axis=-1).astype(_BF16)
    y = _rms(x_ref[win, :] + _dot(o_cat, _weight(wout_ref)), fg_ref[...])
    if share == 1:
        y_ref[win, :] = y
    else:
        @pl.when(base == 0)
        def _():
            y_ref[win, :] = y

        @pl.when(base != 0)
        def _():
            keep = lax.broadcasted_iota(jnp.int32, y.shape, 0) >= base
            y_ref[win, :] = jnp.where(keep, y, y_ref[win, :])
    yield


def _mixer_kernel(xp_ref, xn_ref, cos_ref, sin_ref, win_ref, wout_ref, ng_ref, rg_ref, hgg_ref, lb_ref, fg_ref,
                  xs_ref, cos_s_ref, sin_s_ref, dret_in, dhg_in,
                  y_ref, sret_ref, shg_ref, ys_ref, dret_out, dhg_out,
                  rq_ref, rk_ref, rdec_ref, pa_ref, pb_ref, *dec,
                  heads, dh, tile, chunk, pairs, per_step):
    step = pl.program_id(0)
    pair = step % pairs
    rows_a, rows_b = slice(0, tile), slice(tile, 2 * tile)

    @pl.when(pair == 0)
    def _():
        sret_ref[...] = jnp.zeros_like(sret_ref)
        shg_ref[...] = jnp.zeros_like(shg_ref)

    @pl.when(step == 0)
    def _():
        _retention_tables(rq_ref, rk_ref, rdec_ref, heads, tile)
        _decode_project(xs_ref, cos_s_ref, sin_s_ref, win_ref, ng_ref, lb_ref, dec, heads=heads, dh=dh)
        for _ in _proj_stages(xp_ref, rows_a, ng_ref, win_ref, pa_ref):
            pass

    mixer = functools.partial(
        _mixer, cos_ref=cos_ref, sin_ref=sin_ref, wout_ref=wout_ref, rg_ref=rg_ref, hgg_ref=hgg_ref,
        lb_ref=lb_ref, fg_ref=fg_ref, y_ref=y_ref, sret_ref=sret_ref, shg_ref=shg_ref,
        rq_ref=rq_ref, rk_ref=rk_ref, rdec_ref=rdec_ref, heads=heads, dh=dh, tile=tile, chunk=chunk)
    decode = _decode_stages(step, xs_ref, wout_ref, rg_ref, hgg_ref, fg_ref, dret_in, dhg_in, ys_ref,
                            dret_out, dhg_out, dec, heads=heads, dh=dh, per_step=per_step)
    every = max(1, 2 * (1 + PROJ_GROUPS) // (per_step * heads + 1))
    mixer(xp_ref, rows_a, pa_ref,
          _filler(_zip_stages(_proj_stages(xp_ref, rows_b, ng_ref, win_ref, pb_ref), decode, every)))
    mixer(xp_ref, rows_b, pb_ref,
          _filler(_zip_stages(_proj_stages(xn_ref, rows_a, ng_ref, win_ref, pa_ref), decode, every)))
    for _ in decode:
        pass


def _mixer_call(x, cos, sin, win, wout, ng, rg, hgg, lb, fg, xs, cos_s, sin_s, dret, dhg):
    bsz, seq, d = x.shape
    nb, heads, dh, _ = dret.shape
    tile = min(PROMPT_TILE, seq // 2)
    chunk = min(HGRN_CHUNK, tile)
    assert seq % (2 * tile) == 0 and tile % chunk == 0 and chunk & (chunk - 1) == 0 and chunk >= 16
    gw = heads * dh
    pairs = seq // (2 * tile)
    steps = bsz * pairs
    per_step = nb // steps
    assert nb == per_step * steps and nb % 128 == 0

    def nxt(s):
        s1 = jnp.minimum(s + 1, steps - 1)
        return (s1 // pairs, 2 * (s1 % pairs), 0)

    const = lambda *shape: pl.BlockSpec(shape, lambda s: (0,) * len(shape))
    state_spec = pl.BlockSpec((1, heads, dh, dh), lambda s: (s // pairs, 0, 0, 0))
    dstate_spec = pl.BlockSpec((per_step, heads, dh, dh), lambda s: (s, 0, 0, 0))
    pair_spec = pl.BlockSpec((1, 2 * tile, d), lambda s: (s // pairs, s % pairs, 0))
    rope_spec = pl.BlockSpec((2 * tile, dh), lambda s: (s % pairs, 0))
    body = functools.partial(_mixer_kernel, heads=heads, dh=dh, tile=tile, chunk=chunk, pairs=pairs,
                             per_step=per_step)
    row_scratch = pltpu.VMEM((nb, gw), _F32)
    col_scratch = pltpu.VMEM((gw, nb), _F32)
    return pl.pallas_call(
        body,
        grid=(steps,),
        in_specs=[
            pair_spec, pl.BlockSpec((1, tile, d), nxt), rope_spec, rope_spec,
            const(d // 2, 8 * gw), const(gw, d), const(1, d), const(1, gw), const(1, gw),
            const(lb.shape[0], gw), const(1, d),
            const(nb, d), const(1, dh), const(1, dh), dstate_spec, dstate_spec,
        ],
        out_specs=[pair_spec, state_spec, state_spec, const(nb, d), dstate_spec, dstate_spec],
        out_shape=[
            jax.ShapeDtypeStruct((bsz, seq, d), _F32),
            jax.ShapeDtypeStruct((bsz, heads, dh, dh), _F32),
            jax.ShapeDtypeStruct((bsz, heads, dh, dh), _F32),
            jax.ShapeDtypeStruct((nb, d), _F32),
            jax.ShapeDtypeStruct(dret.shape, _F32),
            jax.ShapeDtypeStruct(dhg.shape, _F32),
        ],
        scratch_shapes=[pltpu.VMEM((heads, tile, dh), _F32), pltpu.VMEM((heads, tile, dh), _F32),
                        pltpu.VMEM((heads, tile, tile), _F32),
                        pltpu.VMEM((tile, 8 * gw), _F32), pltpu.VMEM((tile, 8 * gw), _F32)]
                       + [col_scratch, col_scratch, row_scratch, row_scratch, col_scratch, col_scratch,
                          row_scratch, row_scratch, col_scratch],
        compiler_params=pltpu.CompilerParams(
            dimension_semantics=("arbitrary",), vmem_limit_bytes=VMEM_LIMIT_BYTES),
        name="mixer_step",
    )(x, x, cos, sin, win, wout, ng, rg, hgg, lb, fg, xs, cos_s, sin_s, dret, dhg)


def _rope_tables(pos, dh):
    half = dh // 2
    lane = jnp.arange(dh)
    freqs = 1.0 / (ROPE_BASE ** ((lane % half).astype(_F32) / half))
    ang = pos[:, None] * freqs[None, :]
    return jnp.cos(ang), jnp.sin(ang) * jnp.where(lane < half, -1.0, 1.0)[None, :]


def kernel(x_prompt, x_sample, state_ret, state_hgrn, norm_g, w_in, ret_norm_g, hg_norm_g, hg_lb, w_out,
           final_norm_g):
    depth, _, heads, dh, _ = state_ret.shape
    assert depth == 1 and x_sample.shape[1] == 1
    lp = x_prompt.shape[1]
    win, wout = _pack_weights(w_in[0], w_out[0])
    ng, fg = norm_g[0][None, :], final_norm_g[None, :]
    rg, hgg = ret_norm_g[0][None, :], hg_norm_g[0][None, :]

    cos_p, sin_p = _rope_tables(jnp.arange(lp, dtype=_F32), dh)
    cos_s, sin_s = _rope_tables(PAST_LEN + jnp.arange(1, dtype=_F32), dh)
    y_p, ret_p, hg_p, y_s, ret_s, hg_s = _mixer_call(
        x_prompt, cos_p, sin_p, win, wout, ng, rg, hgg, hg_lb, fg,
        x_sample[:, 0, :], cos_s, sin_s, state_ret[0], state_hgrn[0])
    return (y_p, y_s[:, None, :], ret_p[None], hg_p[None], ret_s[None], hg_s[None])
```

```python
import functools
import math

import jax
import jax.numpy as jnp
from jax import lax
from jax.experimental import pallas as pl
from jax.experimental.pallas import tpu as pltpu

PAST_LEN = 16384
ROPE_BASE = 10000.0
NORM_EPS = 1e-6

PROMPT_TILE = 256
HGRN_CHUNK = 128
PROJ_GROUPS = 16
PACK_ROWS = 256
MXU_ROW_GROUP = 32
SUBLANES = 8
VMEM_LIMIT_BYTES = 56 * 1024 * 1024

_F32 = jnp.float32
_BF16 = jnp.bfloat16


def _rms(x, g):
    return x * lax.rsqrt(jnp.mean(x * x, axis=-1, keepdims=True) + NORM_EPS) * g


def _sigmoid(x):
    return 1.0 / (1.0 + jnp.exp(-x))


def _silu(x):
    return x * _sigmoid(x)


def _dot(a, b):
    return jnp.dot(a, b, preferred_element_type=_F32)


def _dot_nt(a, b):
    return lax.dot_general(a, b, (((1,), (1,)), ((), ())), preferred_element_type=_F32)


def _dot_tn(a, b):
    return lax.dot_general(a, b, (((0,), (0,)), ((), ())), preferred_element_type=_F32)


def _weight(w_ref, cols=slice(None)):
    return pltpu.bitcast(w_ref[:, cols], _BF16)


def _pack_kernel(*refs, half):
    *refs, cos_ref, sin_ref = refs
    n = len(refs) // 2
    for w_ref, o_ref in zip(refs[:n], refs[n:]):
        o_ref[...] = pltpu.bitcast(w_ref[...].astype(_BF16), jnp.int32)
    rows, dh = cos_ref.shape
    lane = lax.broadcasted_iota(jnp.int32, (rows, dh), 1)
    pos = (lax.broadcasted_iota(jnp.int32, (rows, dh), 0) + pl.program_id(0) * rows).astype(_F32)
    freqs = 1.0 / (ROPE_BASE ** ((lane % half).astype(_F32) / half))
    ang = pos * freqs
    cos_ref[...] = jnp.cos(ang)
    sin_ref[...] = jnp.where(lane < half, -jnp.sin(ang), jnp.sin(ang))


def _pack_weights(ws, seq, dh):
    steps = max(w.shape[0] for w in ws) // PACK_ROWS
    rows = [w.shape[0] // steps for w in ws]
    assert all(w.shape[0] == r * steps and r % 16 == 0 for w, r in zip(ws, rows)) and seq % (8 * steps) == 0
    rope_spec = pl.BlockSpec((seq // steps, dh), lambda i: (i, 0))
    return pl.pallas_call(
        functools.partial(_pack_kernel, half=dh // 2),
        grid=(steps,),
        in_specs=[pl.BlockSpec((r, w.shape[1]), lambda i: (i, 0)) for w, r in zip(ws, rows)],
        out_specs=[pl.BlockSpec((r // 2, w.shape[1]), lambda i: (i, 0)) for w, r in zip(ws, rows)]
                  + [rope_spec, rope_spec],
        out_shape=[jax.ShapeDtypeStruct((w.shape[0] // 2, w.shape[1]), jnp.int32) for w in ws]
                  + [jax.ShapeDtypeStruct((seq, dh), _F32)] * 2,
        compiler_params=pltpu.CompilerParams(dimension_semantics=("arbitrary",)),
        name="pack_weights",
    )(*ws)


def _rotate(x, cos, sin_signed, half):
    return x * cos + pltpu.roll(x, half, 1) * sin_signed


def _lower_bound(lb_ref):
    a = lb_ref[...]
    m = jnp.max(a, axis=0, keepdims=True)
    e = jnp.exp(a - m)
    return e[0:1, :] / jnp.sum(e, axis=0, keepdims=True)


def _forget_gate(hf, lbv):
    z = jnp.exp(-jnp.abs(hf))
    r = 1.0 / (1.0 + z)
    zr = z * r
    pos = hf >= 0.0
    sig_p = jnp.where(pos, r, zr)
    sig_n = jnp.where(pos, zr, r)
    one_m = 1.0 - lbv
    return lbv + one_m * sig_p, one_m * sig_n


def _decay_levels(qa, kin, f):
    c, w = f.shape
    n_lev = c.bit_length() - 1
    rowi = lax.broadcasted_iota(jnp.int32, (c, w), 0)
    tot, q, k = f, qa * f, kin
    sub_levels = SUBLANES.bit_length() - 1
    for lev in range(sub_levels):
        half = 1 << lev
        up = (rowi & half) != 0
        yield jnp.where(up, q, 0.0).astype(_BF16), jnp.where(up, 0.0, k).T.astype(_BF16)
        sib = jnp.where(up, pltpu.roll(tot, half, 0), pltpu.roll(tot, c - half, 0))
        q = jnp.where(up, q * sib, q)
        k = jnp.where(up, k, k * sib)
        tot = tot * sib
    q_b = [q[i:i + SUBLANES] for i in range(0, c, SUBLANES)]
    k_b = [k[i:i + SUBLANES] for i in range(0, c, SUBLANES)]
    tot_b = [tot[i:i + 1] for i in range(0, c, SUBLANES)]
    for lev in range(sub_levels, n_lev):
        half = 1 << lev
        zero = jnp.zeros((half, w), _F32)
        yield (jnp.concatenate([q_b[i] if i % 2 else zero for i in range(len(q_b))], axis=0).astype(_BF16),
               jnp.concatenate([zero if i % 2 else k_b[i] for i in range(len(k_b))], axis=0).T.astype(_BF16))
        q_b = [jnp.concatenate([q_b[i], q_b[i + 1] * tot_b[i]], axis=0) for i in range(0, len(q_b), 2)]
        k_b = [jnp.concatenate([k_b[i] * tot_b[i + 1], k_b[i + 1]], axis=0) for i in range(0, len(k_b), 2)]
        tot_b = [tot_b[i] * tot_b[i + 1] for i in range(0, len(tot_b), 2)]
    yield q_b[0].astype(_BF16), k_b[0].astype(_BF16), tot_b[0]


def _proj_stages(x_ref, rows, ng_ref, win_ref, p_ref):
    h = _rms(x_ref[0, rows, :], ng_ref[...]).astype(_BF16)
    yield
    width = p_ref.shape[1] // PROJ_GROUPS
    for sec in range(PROJ_GROUPS):
        cols = slice(sec * width, (sec + 1) * width)
        p_ref[:, cols] = _dot(h, _weight(win_ref, cols))
        yield


def _zip_stages(main, side, every):
    for n, _ in enumerate(main, 1):
        if n % every == 0:
            next(side, None)
        yield


def _filler(stages):
    def fill(n):
        if n is None:
            for _ in stages:
                pass
        else:
            for _ in range(n):
                next(stages, None)
    return fill


def _log_gamma(hh):
    return math.log1p(-(2.0 ** (-5 - hh)))


def _retention_tables(rq_ref, rk_ref, rdec_ref, heads, tile):
    row = lax.broadcasted_iota(jnp.int32, rq_ref.shape[1:], 0).astype(_F32)
    ti = lax.broadcasted_iota(jnp.int32, (tile, tile), 0)
    si = lax.broadcasted_iota(jnp.int32, (tile, tile), 1)
    lag = jnp.maximum(ti - si, 0).astype(_F32)
    for hh in range(heads):
        lg = _log_gamma(hh)
        rq_ref[hh] = jnp.exp(lg * (row + 1.0))
        rk_ref[hh] = jnp.exp(lg * (tile - 1.0 - row))
        rdec_ref[hh] = jnp.where(ti >= si, jnp.exp(lg * lag), 0.0)


def _mixer(x_ref, rows, p_ref, fill, *, cos_ref, sin_ref, wout_ref, rg_ref, hgg_ref, lb_ref, fg_ref,
           y_ref, sret_ref, shg_ref, rq_ref, rk_ref, rdec_ref, heads, dh, tile, chunk):
    gw = heads * dh

    def p(sec, hh=None, rs=slice(None)):
        if hh is None:
            return p_ref[rs, sec * gw:(sec + 1) * gw]
        return p_ref[rs, sec * gw + hh * dh:sec * gw + (hh + 1) * dh]

    o_all = [None] * (2 * heads)

    cos, sin = cos_ref[rows, :], sin_ref[rows, :]

    def ret_issue(hh):
        q = _rotate(p(0, hh), cos, sin, dh // 2).astype(_BF16)
        kt = (_rotate(p(1, hh), cos, sin, dh // 2) * (dh ** -0.5)).astype(_BF16).T
        v = p(2, hh)
        s_old = sret_ref[0, hh]
        qk_qs = _dot(q, jnp.concatenate([kt, s_old.astype(_BF16)], axis=1))
        sret_ref[0, hh] = math.exp(tile * _log_gamma(hh)) * s_old + _dot(kt, (v * rk_ref[hh]).astype(_BF16))
        return hh, qk_qs, v

    def ret_finish(hh, qk_qs, v):
        sl = slice(hh * dh, (hh + 1) * dh)
        a = qk_qs[:, :tile] * rdec_ref[hh]
        o = qk_qs[:, tile:] * rq_ref[hh] + _dot(a.astype(_BF16), v.astype(_BF16))
        o_all[hh] = (_rms(o, rg_ref[:, sl]) * _silu(p(3, hh))).astype(_BF16)

    lbv = _lower_bound(lb_ref)
    ci = lax.broadcasted_iota(jnp.int32, (chunk, chunk), 0)
    cj = lax.broadcasted_iota(jnp.int32, (chunk, chunk), 1)
    cxor = ci ^ cj
    n_lev = chunk.bit_length() - 1
    diag = ci == cj
    lev_mask = [(ci > cj) & ((cxor >> lev) == 1) for lev in range(n_lev)]
    n_chunks = tile // chunk
    o_hg_chunks = [[None] * n_chunks for _ in range(heads)]

    def hg_issue(c, hh):
        rs = slice(c * chunk, (c + 1) * chunk)
        sl = slice(hh * dh, (hh + 1) * dh)
        f, gk = _forget_gate(p(5, hh, rs), lbv[:, sl])
        gq = _silu(p(4, hh, rs))
        gv = p(6, hh, rs).astype(_BF16)
        a = jnp.where(diag, jnp.sum(gq * gk, axis=-1, keepdims=True), 0.0).astype(_BF16)
        *levels, (q_top, k_top, f_tot) = _decay_levels(gq, gk, f)
        for lev, (q_lv, k_lv) in enumerate(levels):
            half = 1 << lev
            if half % MXU_ROW_GROUP:
                a = jnp.where(lev_mask[lev], _dot(q_lv, k_lv).astype(_BF16), a)
            else:
                ups = [slice(i, i + half) for i in range(half, chunk, 2 * half)]
                prod = _dot(jnp.concatenate([q_lv[u] for u in ups], axis=0), k_lv).astype(_BF16)
                parts, at = [], 0
                for j, u in enumerate(ups):
                    parts.append(a[at:u.start])
                    parts.append(jnp.where(lev_mask[lev][u], prod[j * half:(j + 1) * half], a[u]))
                    at = u.stop
                a = jnp.concatenate(parts, axis=0)
        s_old = shg_ref[0, hh]
        o_inter = _dot(q_top, s_old.astype(_BF16))
        f_col = jnp.broadcast_to(f_tot, (dh, dh)).T
        shg_ref[0, hh] = s_old * f_col + _dot_tn(k_top, gv)
        return c, hh, a, o_inter, gv

    def hg_finish(c, hh, a, o_inter, gv):
        o_hg_chunks[hh][c] = o_inter + _dot(a, gv)

    pend = None
    for hh in range(heads):
        ctx = ret_issue(hh)
        fill(1)
        if pend is not None:
            ret_finish(*pend)
        pend = ctx
    fill(2)
    ret_finish(*pend)
    pend = None
    for c in range(n_chunks):
        for hh in range(heads):
            ctx = hg_issue(c, hh)
            fill(1)
            if pend is not None:
                hg_finish(*pend)
            pend = ctx
    fill(2)
    hg_finish(*pend)
    fill(1)
    for hh in range(heads):
        sl = slice(hh * dh, (hh + 1) * dh)
        o = jnp.concatenate(o_hg_chunks[hh], axis=0)
        o_all[heads + hh] = (_rms(o, hgg_ref[:, sl]) * _silu(p(7, hh))).astype(_BF16)
    fill(None)
    o_cat = jnp.concatenate(o_all, axis=-1)
    y = x_ref[0, rows, :] + _dot(o_cat, _weight(wout_ref))
    y_ref[0, rows, :] = _rms(y, fg_ref[...])


def _decode_project(x_ref, cos_ref, sin_ref, win_ref, ng_ref, lb_ref, dec, *, heads, dh):
    rqt_s, rkt_s, rv_s, rgate_s, hqt_s, hkt_s, hi_s, hgate_s, hft_s = dec
    gw = heads * dh
    h = _rms(x_ref[...], ng_ref[...]).astype(_BF16)

    def proj(sec):
        return _dot(h, _weight(win_ref, slice(sec * gw, (sec + 1) * gw)))

    cos, sin = cos_ref[...], sin_ref[...]

    def rot(pj):
        return jnp.concatenate(
            [_rotate(pj[:, hh * dh:(hh + 1) * dh], cos, sin, dh // 2) for hh in range(heads)], axis=-1)

    rqt_s[...] = rot(proj(0)).T
    rkt_s[...] = (rot(proj(1)) * (dh ** -0.5)).T
    rv_s[...] = proj(2)
    rgate_s[...] = _silu(proj(3))
    f, kin = _forget_gate(proj(5), _lower_bound(lb_ref))
    hqt_s[...] = _silu(proj(4)).T
    hkt_s[...] = kin.T
    hft_s[...] = f.T
    hi_s[...] = proj(6)
    hgate_s[...] = _silu(proj(7))


def _decode_stages(step, x_ref, wout_ref, rg_ref, hgg_ref, fg_ref, sret_in, shg_in, y_ref, sret_out, shg_out, dec,
                   *, heads, dh, per_step):
    rqt_s, rkt_s, rv_s, rgate_s, hqt_s, hkt_s, hi_s, hgate_s, hft_s = dec
    nb = x_ref.shape[0]
    wrows = max(per_step, SUBLANES)
    share = wrows // per_step
    assert share in (1, 2) and wrows % SUBLANES == 0
    win = pl.ds(pl.multiple_of((step // share) * wrows, SUBLANES), wrows)
    base = (step % share) * per_step
    shift = (nb - step * per_step) % nb
    rolled = {}

    def col(ref_s, hh, bl):
        if id(ref_s) not in rolled:
            rolled[id(ref_s)] = pltpu.roll(ref_s[...], shift, 1)
        return jnp.broadcast_to(rolled[id(ref_s)][hh * dh:(hh + 1) * dh, bl:bl + 1], (dh, dh))

    def row_of(ref_s, sl, mine):
        return jnp.sum(jnp.where(mine, ref_s[win, sl], 0.0), axis=0, keepdims=True)

    rowi = lax.broadcasted_iota(jnp.int32, (wrows, dh), 0)
    o_ret = [jnp.zeros((wrows, dh), _F32)] * heads
    o_hg = [jnp.zeros((wrows, dh), _F32)] * heads
    for bl in range(per_step):
        mine = rowi == base + bl
        for hh in range(heads):
            sl = slice(hh * dh, (hh + 1) * dh)
            gamma = 1.0 - 2.0 ** (-5 - hh)
            s_new = gamma * sret_in[bl, hh] + col(rkt_s, hh, bl) * row_of(rv_s, sl, mine)
            sret_out[bl, hh] = s_new
            o = jnp.sum(s_new * col(rqt_s, hh, bl), axis=0, keepdims=True)
            o_ret[hh] = jnp.where(mine, o, o_ret[hh])
            g_new = col(hft_s, hh, bl) * shg_in[bl, hh] + col(hkt_s, hh, bl) * row_of(hi_s, sl, mine)
            shg_out[bl, hh] = g_new
            o = jnp.sum(g_new * col(hqt_s, hh, bl), axis=0, keepdims=True)
            o_hg[hh] = jnp.where(mine, o, o_hg[hh])
            yield

    def finish(o_rows, gain_ref, gate):
        return [_rms(o, gain_ref[:, hh * dh:(hh + 1) * dh]) * gate[:, hh * dh:(hh + 1) * dh]
                for hh, o in enumerate(o_rows)]

    o_cat = jnp.concatenate(
        finish(o_ret, rg_ref, rgate_s[win, :]) + finish(o_hg, hgg_ref, hgate_s[win, :]), axis=-1).astype(_BF16)
    y = _rms(x_ref[win, :] + _dot(o_cat, _weight(wout_ref)), fg_ref[...])
    if share == 1:
        y_ref[win, :] = y
    else:
        @pl.when(base == 0)
        def _():
            y_ref[win, :] = y

        @pl.when(base != 0)
        def _():
            keep = lax.broadcasted_iota(jnp.int32, y.shape, 0) >= base
            y_ref[win, :] = jnp.where(keep, y, y_ref[win, :])
    yield


def _mixer_kernel(xp_ref, xn_ref, cos_ref, sin_ref, win_ref, wout_ref, ng_ref, rg_ref, hgg_ref, lb_ref, fg_ref,
                  xs_ref, cos_s_ref, sin_s_ref, dret_in, dhg_in,
                  y_ref, sret_ref, shg_ref, ys_ref, dret_out, dhg_out,
                  pa_ref, pb_ref, rq_ref, rk_ref, rdec_ref, *dec,
                  heads, dh, tile, chunk, pairs, per_step):
    step = pl.program_id(0)
    pair = step % pairs
    rows_a, rows_b = slice(0, tile), slice(tile, 2 * tile)

    @pl.when(pair == 0)
    def _():
        sret_ref[...] = jnp.zeros_like(sret_ref)
        shg_ref[...] = jnp.zeros_like(shg_ref)

    @pl.when(step == 0)
    def _():
        _retention_tables(rq_ref, rk_ref, rdec_ref, heads, tile)
        _decode_project(xs_ref, cos_s_ref, sin_s_ref, win_ref, ng_ref, lb_ref, dec, heads=heads, dh=dh)
        for _ in _proj_stages(xp_ref, rows_a, ng_ref, win_ref, pa_ref):
            pass

    mixer = functools.partial(
        _mixer, cos_ref=cos_ref, sin_ref=sin_ref, wout_ref=wout_ref, rg_ref=rg_ref, hgg_ref=hgg_ref,
        lb_ref=lb_ref, fg_ref=fg_ref, y_ref=y_ref, sret_ref=sret_ref, shg_ref=shg_ref,
        rq_ref=rq_ref, rk_ref=rk_ref, rdec_ref=rdec_ref, heads=heads, dh=dh, tile=tile, chunk=chunk)
    decode = _decode_stages(step, xs_ref, wout_ref, rg_ref, hgg_ref, fg_ref, dret_in, dhg_in, ys_ref,
                            dret_out, dhg_out, dec, heads=heads, dh=dh, per_step=per_step)
    every = max(1, 2 * (1 + PROJ_GROUPS) // (per_step * heads + 1))
    mixer(xp_ref, rows_a, pa_ref,
          _filler(_zip_stages(_proj_stages(xp_ref, rows_b, ng_ref, win_ref, pb_ref), decode, every)))
    mixer(xp_ref, rows_b, pb_ref,
          _filler(_zip_stages(_proj_stages(xn_ref, rows_a, ng_ref, win_ref, pa_ref), decode, every)))
    for _ in decode:
        pass


def _mixer_call(x, cos, sin, win, wout, ng, rg, hgg, lb, fg, xs, cos_s, sin_s, dret, dhg):
    bsz, seq, d = x.shape
    nb, heads, dh, _ = dret.shape
    tile = min(PROMPT_TILE, seq // 2)
    chunk = min(HGRN_CHUNK, tile)
    assert seq % (2 * tile) == 0 and tile % chunk == 0 and chunk & (chunk - 1) == 0 and chunk >= 16
    gw = heads * dh
    pairs = seq // (2 * tile)
    steps = bsz * pairs
    per_step = nb // steps
    assert nb == per_step * steps and nb % 128 == 0

    def nxt(s):
        s1 = jnp.minimum(s + 1, steps - 1)
        return (s1 // pairs, 2 * (s1 % pairs), 0)

    const = lambda *shape: pl.BlockSpec(shape, lambda s: (0,) * len(shape))
    state_spec = pl.BlockSpec((1, heads, dh, dh), lambda s: (s // pairs, 0, 0, 0))
    dstate_spec = pl.BlockSpec((per_step, heads, dh, dh), lambda s: (s, 0, 0, 0))
    pair_spec = pl.BlockSpec((1, 2 * tile, d), lambda s: (s // pairs, s % pairs, 0))
    rope_spec = pl.BlockSpec((2 * tile, dh), lambda s: (s % pairs, 0))
    body = functools.partial(_mixer_kernel, heads=heads, dh=dh, tile=tile, chunk=chunk, pairs=pairs,
                             per_step=per_step)
    row_scratch = pltpu.VMEM((nb, gw), _F32)
    col_scratch = pltpu.VMEM((gw, nb), _F32)
    return pl.pallas_call(
        body,
        grid=(steps,),
        in_specs=[
            pair_spec, pl.BlockSpec((1, tile, d), nxt), rope_spec, rope_spec,
            const(d // 2, 8 * gw), const(gw, d), const(1, d), const(1, gw), const(1, gw),
            const(lb.shape[0], gw), const(1, d),
            const(nb, d), const(1, dh), const(1, dh), dstate_spec, dstate_spec,
        ],
        out_specs=[pair_spec, state_spec, state_spec, const(nb, d), dstate_spec, dstate_spec],
        out_shape=[
            jax.ShapeDtypeStruct((bsz, seq, d), _F32),
            jax.ShapeDtypeStruct((bsz, heads, dh, dh), _F32),
            jax.ShapeDtypeStruct((bsz, heads, dh, dh), _F32),
            jax.ShapeDtypeStruct((nb, d), _F32),
            jax.ShapeDtypeStruct(dret.shape, _F32),
            jax.ShapeDtypeStruct(dhg.shape, _F32),
        ],
        scratch_shapes=[pltpu.VMEM((tile, 8 * gw), _F32), pltpu.VMEM((tile, 8 * gw), _F32),
                        pltpu.VMEM((heads, tile, dh), _F32), pltpu.VMEM((heads, tile, dh), _F32),
                        pltpu.VMEM((heads, tile, tile), _F32)]
                       + [col_scratch, col_scratch, row_scratch, row_scratch, col_scratch, col_scratch,
                          row_scratch, row_scratch, col_scratch],
        compiler_params=pltpu.CompilerParams(
            dimension_semantics=("arbitrary",), vmem_limit_bytes=VMEM_LIMIT_BYTES),
        name="mixer_step",
    )(x, x, cos, sin, win, wout, ng, rg, hgg, lb, fg, xs, cos_s, sin_s, dret, dhg)


def _rope_tables(pos, dh):
    half = dh // 2
    lane = jnp.arange(dh)
    freqs = 1.0 / (ROPE_BASE ** ((lane % half).astype(_F32) / half))
    ang = pos[:, None] * freqs[None, :]
    return jnp.cos(ang), jnp.sin(ang) * jnp.where(lane < half, -1.0, 1.0)[None, :]


def kernel(x_prompt, x_sample, state_ret, state_hgrn, norm_g, w_in, ret_norm_g, hg_norm_g, hg_lb, w_out,
           final_norm_g):
    depth, _, heads, dh, _ = state_ret.shape
    assert depth == 1 and x_sample.shape[1] == 1
    lp = x_prompt.shape[1]
    win, wout, cos_p, sin_p = _pack_weights((w_in[0], w_out[0]), lp, dh)
    ng, fg = norm_g[0][None, :], final_norm_g[None, :]
    rg, hgg = ret_norm_g[0][None, :], hg_norm_g[0][None, :]

    cos_s, sin_s = _rope_tables(PAST_LEN + jnp.arange(1, dtype=_F32), dh)
    y_p, ret_p, hg_p, y_s, ret_s, hg_s = _mixer_call(
        x_prompt, cos_p, sin_p, win, wout, ng, rg, hgg, hg_lb, fg,
        x_sample[:, 0, :], cos_s, sin_s, state_ret[0], state_hgrn[0])
    return (y_p, y_s[:, None, :], ret_p[None], hg_p[None], ret_s[None], hg_s[None])
```

```python
import functools
import math

import jax
import jax.numpy as jnp
from jax import lax
from jax.experimental import pallas as pl
from jax.experimental.pallas import tpu as pltpu

PAST_LEN = 16384
ROPE_BASE = 10000.0
NORM_EPS = 1e-6

PROMPT_TILE = 256
HGRN_CHUNK = 128
PROJ_GROUPS = 16
PACK_ROWS = 256
MXU_ROW_GROUP = 32
SUBLANES = 8
VMEM_LIMIT_BYTES = 56 * 1024 * 1024

_F32 = jnp.float32
_BF16 = jnp.bfloat16


def _rms(x, g):
    return x * lax.rsqrt(jnp.mean(x * x, axis=-1, keepdims=True) + NORM_EPS) * g


def _sigmoid(x):
    return 1.0 / (1.0 + jnp.exp(-x))


def _silu(x):
    return x * _sigmoid(x)


def _dot(a, b):
    return jnp.dot(a, b, preferred_element_type=_F32)


def _dot_nt(a, b):
    return lax.dot_general(a, b, (((1,), (1,)), ((), ())), preferred_element_type=_F32)


def _dot_tn(a, b):
    return lax.dot_general(a, b, (((0,), (0,)), ((), ())), preferred_element_type=_F32)


def _weight(w_ref, cols=slice(None)):
    return pltpu.bitcast(w_ref[:, cols], _BF16)


def _pack_kernel(xs_ref, *refs, half):
    *refs, cos_ref, sin_ref, cos_d_ref, sin_d_ref, xs_out = refs
    xs_out[...] = xs_ref[:, 0, :]
    n = len(refs) // 2
    for w_ref, o_ref in zip(refs[:n], refs[n:]):
        o_ref[...] = pltpu.bitcast(w_ref[...].astype(_BF16), jnp.int32)
    rows, dh = cos_ref.shape
    lane = lax.broadcasted_iota(jnp.int32, (rows, dh), 1)
    pos = (lax.broadcasted_iota(jnp.int32, (rows, dh), 0) + pl.program_id(0) * rows).astype(_F32)
    freqs = 1.0 / (ROPE_BASE ** ((lane % half).astype(_F32) / half))
    ang = pos * freqs
    cos_ref[...] = jnp.cos(ang)
    sin_ref[...] = jnp.where(lane < half, -jnp.sin(ang), jnp.sin(ang))
    ang_d = float(PAST_LEN) * freqs[:SUBLANES]
    cos_d_ref[...] = jnp.cos(ang_d)
    sin_d_ref[...] = jnp.where(lane[:SUBLANES] < half, -jnp.sin(ang_d), jnp.sin(ang_d))


def _pack_weights(ws, seq, dh, xs):
    steps = max(w.shape[0] for w in ws) // PACK_ROWS
    rows = [w.shape[0] // steps for w in ws]
    assert all(w.shape[0] == r * steps and r % 16 == 0 for w, r in zip(ws, rows)) and seq % (8 * steps) == 0
    rope_spec = pl.BlockSpec((seq // steps, dh), lambda i: (i, 0))
    nb, _, d = xs.shape
    assert nb % (8 * steps) == 0
    return pl.pallas_call(
        functools.partial(_pack_kernel, half=dh // 2),
        grid=(steps,),
        in_specs=[pl.BlockSpec((nb // steps, 1, d), lambda i: (i, 0, 0))]
                 + [pl.BlockSpec((r, w.shape[1]), lambda i: (i, 0)) for w, r in zip(ws, rows)],
        out_specs=[pl.BlockSpec((r // 2, w.shape[1]), lambda i: (i, 0)) for w, r in zip(ws, rows)]
                  + [rope_spec, rope_spec] + [pl.BlockSpec((SUBLANES, dh), lambda i: (0, 0))] * 2
                  + [pl.BlockSpec((nb // steps, d), lambda i: (i, 0))],
        out_shape=[jax.ShapeDtypeStruct((w.shape[0] // 2, w.shape[1]), jnp.int32) for w in ws]
                  + [jax.ShapeDtypeStruct((seq, dh), _F32)] * 2 + [jax.ShapeDtypeStruct((SUBLANES, dh), _F32)] * 2
                  + [jax.ShapeDtypeStruct((nb, d), _F32)],
        compiler_params=pltpu.CompilerParams(dimension_semantics=("arbitrary",)),
        name="pack_weights",
    )(xs, *ws)


def _rotate(x, cos, sin_signed, half):
    return x * cos + pltpu.roll(x, half, 1) * sin_signed


def _lower_bound(lb_ref):
    a = lb_ref[...]
    m = jnp.max(a, axis=0, keepdims=True)
    e = jnp.exp(a - m)
    return e[0:1, :] / jnp.sum(e, axis=0, keepdims=True)


def _forget_gate(hf, lbv):
    z = jnp.exp(-jnp.abs(hf))
    r = 1.0 / (1.0 + z)
    zr = z * r
    pos = hf >= 0.0
    sig_p = jnp.where(pos, r, zr)
    sig_n = jnp.where(pos, zr, r)
    one_m = 1.0 - lbv
    return lbv + one_m * sig_p, one_m * sig_n


def _decay_levels(qa, kin, f):
    c, w = f.shape
    n_lev = c.bit_length() - 1
    rowi = lax.broadcasted_iota(jnp.int32, (c, w), 0)
    tot, q, k = f, qa * f, kin
    sub_levels = SUBLANES.bit_length() - 1
    for lev in range(sub_levels):
        half = 1 << lev
        up = (rowi & half) != 0
        yield jnp.where(up, q, 0.0).astype(_BF16), jnp.where(up, 0.0, k).T.astype(_BF16)
        sib = jnp.where(up, pltpu.roll(tot, half, 0), pltpu.roll(tot, c - half, 0))
        q = jnp.where(up, q * sib, q)
        k = jnp.where(up, k, k * sib)
        tot = tot * sib
    q_b = [q[i:i + SUBLANES] for i in range(0, c, SUBLANES)]
    k_b = [k[i:i + SUBLANES] for i in range(0, c, SUBLANES)]
    tot_b = [tot[i:i + 1] for i in range(0, c, SUBLANES)]
    for lev in range(sub_levels, n_lev):
        half = 1 << lev
        zero = jnp.zeros((half, w), _F32)
        yield (jnp.concatenate([q_b[i] if i % 2 else zero for i in range(len(q_b))], axis=0).astype(_BF16),
               jnp.concatenate([zero if i % 2 else k_b[i] for i in range(len(k_b))], axis=0).T.astype(_BF16))
        q_b = [jnp.concatenate([q_b[i], q_b[i + 1] * tot_b[i]], axis=0) for i in range(0, len(q_b), 2)]
        k_b = [jnp.concatenate([k_b[i] * tot_b[i + 1], k_b[i + 1]], axis=0) for i in range(0, len(k_b), 2)]
        tot_b = [tot_b[i] * tot_b[i + 1] for i in range(0, len(tot_b), 2)]
    yield q_b[0].astype(_BF16), k_b[0].astype(_BF16), tot_b[0]


def _proj_stages(x_ref, rows, ng_ref, win_ref, p_ref):
    h = _rms(x_ref[0, rows, :], ng_ref[...]).astype(_BF16)
    yield
    width = p_ref.shape[1] // PROJ_GROUPS
    for sec in range(PROJ_GROUPS):
        cols = slice(sec * width, (sec + 1) * width)
        p_ref[:, cols] = _dot(h, _weight(win_ref, cols))
        yield


def _zip_stages(main, side, every):
    for n, _ in enumerate(main, 1):
        if n % every == 0:
            next(side, None)
        yield


def _filler(stages):
    def fill(n):
        if n is None:
            for _ in stages:
                pass
        else:
            for _ in range(n):
                next(stages, None)
    return fill


def _log_gamma(hh):
    return math.log1p(-(2.0 ** (-5 - hh)))


def _retention_tables(rq_ref, rk_ref, rdec_ref, heads, tile):
    row = lax.broadcasted_iota(jnp.int32, rq_ref.shape[1:], 0).astype(_F32)
    ti = lax.broadcasted_iota(jnp.int32, (tile, tile), 0)
    si = lax.broadcasted_iota(jnp.int32, (tile, tile), 1)
    lag = jnp.maximum(ti - si, 0).astype(_F32)
    for hh in range(heads):
        lg = _log_gamma(hh)
        rq_ref[hh] = jnp.exp(lg * (row + 1.0))
        rk_ref[hh] = jnp.exp(lg * (tile - 1.0 - row))
        rdec_ref[hh] = jnp.where(ti >= si, jnp.exp(lg * lag), 0.0)


def _mixer(x_ref, rows, p_ref, fill, *, cos_ref, sin_ref, wout_ref, rg_ref, hgg_ref, lb_ref, fg_ref,
           y_ref, sret_ref, shg_ref, rq_ref, rk_ref, rdec_ref, heads, dh, tile, chunk):
    gw = heads * dh

    def p(sec, hh=None, rs=slice(None)):
        if hh is None:
            return p_ref[rs, sec * gw:(sec + 1) * gw]
        return p_ref[rs, sec * gw + hh * dh:sec * gw + (hh + 1) * dh]

    o_all = [None] * (2 * heads)

    cos, sin = cos_ref[rows, :], sin_ref[rows, :]

    def ret_issue(hh):
        q = _rotate(p(0, hh), cos, sin, dh // 2).astype(_BF16)
        kt = (_rotate(p(1, hh), cos, sin, dh // 2) * (dh ** -0.5)).astype(_BF16).T
        v = p(2, hh)
        s_old = sret_ref[0, hh]
        qk_qs = _dot(q, jnp.concatenate([kt, s_old.astype(_BF16)], axis=1))
        sret_ref[0, hh] = math.exp(tile * _log_gamma(hh)) * s_old + _dot(kt, (v * rk_ref[hh]).astype(_BF16))
        return hh, qk_qs, v

    def ret_finish(hh, qk_qs, v):
        sl = slice(hh * dh, (hh + 1) * dh)
        a = qk_qs[:, :tile] * rdec_ref[hh]
        o = qk_qs[:, tile:] * rq_ref[hh] + _dot(a.astype(_BF16), v.astype(_BF16))
        o_all[hh] = (_rms(o, rg_ref[:, sl]) * _silu(p(3, hh))).astype(_BF16)

    lbv = _lower_bound(lb_ref)
    ci = lax.broadcasted_iota(jnp.int32, (chunk, chunk), 0)
    cj = lax.broadcasted_iota(jnp.int32, (chunk, chunk), 1)
    cxor = ci ^ cj
    n_lev = chunk.bit_length() - 1
    diag = ci == cj
    lev_mask = [(ci > cj) & ((cxor >> lev) == 1) for lev in range(n_lev)]
    n_chunks = tile // chunk
    o_hg_chunks = [[None] * n_chunks for _ in range(heads)]

    def hg_issue(c, hh):
        rs = slice(c * chunk, (c + 1) * chunk)
        sl = slice(hh * dh, (hh + 1) * dh)
        f, gk = _forget_gate(p(5, hh, rs), lbv[:, sl])
        gq = _silu(p(4, hh, rs))
        gv = p(6, hh, rs).astype(_BF16)
        a = jnp.where(diag, jnp.sum(gq * gk, axis=-1, keepdims=True), 0.0).astype(_BF16)
        *levels, (q_top, k_top, f_tot) = _decay_levels(gq, gk, f)
        for lev, (q_lv, k_lv) in enumerate(levels):
            half = 1 << lev
            if half % MXU_ROW_GROUP:
                a = jnp.where(lev_mask[lev], _dot(q_lv, k_lv).astype(_BF16), a)
            else:
                ups = [slice(i, i + half) for i in range(half, chunk, 2 * half)]
                prod = _dot(jnp.concatenate([q_lv[u] for u in ups], axis=0), k_lv).astype(_BF16)
                parts, at = [], 0
                for j, u in enumerate(ups):
                    parts.append(a[at:u.start])
                    parts.append(jnp.where(lev_mask[lev][u], prod[j * half:(j + 1) * half], a[u]))
                    at = u.stop
                a = jnp.concatenate(parts, axis=0)
        s_old = shg_ref[0, hh]
        o_inter = _dot(q_top, s_old.astype(_BF16))
        f_col = jnp.broadcast_to(f_tot, (dh, dh)).T
        shg_ref[0, hh] = s_old * f_col + _dot_tn(k_top, gv)
        return c, hh, a, o_inter, gv

    def hg_finish(c, hh, a, o_inter, gv):
        o_hg_chunks[hh][c] = o_inter + _dot(a, gv)

    pend = None
    for hh in range(heads):
        ctx = ret_issue(hh)
        fill(1)
        if pend is not None:
            ret_finish(*pend)
        pend = ctx
    fill(2)
    ret_finish(*pend)
    pend = None
    for c in range(n_chunks):
        for hh in range(heads):
            ctx = hg_issue(c, hh)
            fill(1)
            if pend is not None:
                hg_finish(*pend)
            pend = ctx
    fill(2)
    hg_finish(*pend)
    fill(1)
    for hh in range(heads):
        sl = slice(hh * dh, (hh + 1) * dh)
        o = jnp.concatenate(o_hg_chunks[hh], axis=0)
        o_all[heads + hh] = (_rms(o, hgg_ref[:, sl]) * _silu(p(7, hh))).astype(_BF16)
    fill(None)
    o_cat = jnp.concatenate(o_all, axis=-1)
    y = x_ref[0, rows, :] + _dot(o_cat, _weight(wout_ref))
    y_ref[0, rows, :] = _rms(y, fg_ref[...])


def _decode_project(x_ref, cos_ref, sin_ref, win_ref, ng_ref, lb_ref, dec, *, heads, dh):
    rqt_s, rkt_s, rv_s, rgate_s, hqt_s, hkt_s, hi_s, hgate_s, hft_s = dec
    gw = heads * dh
    h = _rms(x_ref[...], ng_ref[...]).astype(_BF16)

    def proj(sec):
        return _dot(h, _weight(win_ref, slice(sec * gw, (sec + 1) * gw)))

    cos, sin = cos_ref[0:1, :], sin_ref[0:1, :]

    def rot(pj):
        return jnp.concatenate(
            [_rotate(pj[:, hh * dh:(hh + 1) * dh], cos, sin, dh // 2) for hh in range(heads)], axis=-1)

    rqt_s[...] = rot(proj(0)).T
    rkt_s[...] = (rot(proj(1)) * (dh ** -0.5)).T
    rv_s[...] = proj(2)
    rgate_s[...] = _silu(proj(3))
    f, kin = _forget_gate(proj(5), _lower_bound(lb_ref))
    hqt_s[...] = _silu(proj(4)).T
    hkt_s[...] = kin.T
    hft_s[...] = f.T
    hi_s[...] = proj(6)
    hgate_s[...] = _silu(proj(7))


def _decode_stages(step, x_ref, wout_ref, rg_ref, hgg_ref, fg_ref, sret_in, shg_in, y_ref, sret_out, shg_out, dec,
                   *, heads, dh, per_step):
    rqt_s, rkt_s, rv_s, rgate_s, hqt_s, hkt_s, hi_s, hgate_s, hft_s = dec
    nb = x_ref.shape[0]
    wrows = max(per_step, SUBLANES)
    share = wrows // per_step
    assert share in (1, 2) and wrows % SUBLANES == 0
    win = pl.ds(pl.multiple_of((step // share) * wrows, SUBLANES), wrows)
    base = (step % share) * per_step
    shift = (nb - step * per_step) % nb
    rolled = {}

    def col(ref_s, hh, bl):
        if id(ref_s) not in rolled:
            rolled[id(ref_s)] = pltpu.roll(ref_s[...], shift, 1)
        return jnp.broadcast_to(rolled[id(ref_s)][hh * dh:(hh + 1) * dh, bl:bl + 1], (dh, dh))

    def row_of(ref_s, sl, mine):
        return jnp.sum(jnp.where(mine, ref_s[win, sl], 0.0), axis=0, keepdims=True)

    rowi = lax.broadcasted_iota(jnp.int32, (wrows, dh), 0)
    o_ret = [jnp.zeros((wrows, dh), _F32)] * heads
    o_hg = [jnp.zeros((wrows, dh), _F32)] * heads
    for bl in range(per_step):
        mine = rowi == base + bl
        for hh in range(heads):
            sl = slice(hh * dh, (hh + 1) * dh)
            gamma = 1.0 - 2.0 ** (-5 - hh)
            s_new = gamma * sret_in[bl, hh] + col(rkt_s, hh, bl) * row_of(rv_s, sl, mine)
            sret_out[bl, hh] = s_new
            o = jnp.sum(s_new * col(rqt_s, hh, bl), axis=0, keepdims=True)
            o_ret[hh] = jnp.where(mine, o, o_ret[hh])
            g_new = col(hft_s, hh, bl) * shg_in[bl, hh] + col(hkt_s, hh, bl) * row_of(hi_s, sl, mine)
            shg_out[bl, hh] = g_new
            o = jnp.sum(g_new * col(hqt_s, hh, bl), axis=0, keepdims=True)
            o_hg[hh] = jnp.where(mine, o, o_hg[hh])
            yield

    def finish(o_rows, gain_ref, gate):
        return [_rms(o, gain_ref[:, hh * dh:(hh + 1) * dh]) * gate[:, hh * dh:(hh + 1) * dh]
                for hh, o in enumerate(o_rows)]

    o_cat = jnp.concatenate(
        finish(o_ret, rg_ref, rgate_s[win, :]) + finish(o_hg, hgg_ref, hgate_s[win, :]), axis=-1).astype(_BF16)
    y = _rms(x_ref[win, :] + _dot(o_cat, _weight(wout_ref)), fg_ref[...])
    if share == 1:
        y_ref[win, :] = y
    else:
        @pl.when(base == 0)
        def _():
            y_ref[win, :] = y

        @pl.when(base != 0)
        def _():
            keep = lax.broadcasted_iota(jnp.int32, y.shape, 0) >= base
            y_ref[win, :] = jnp.where(keep, y, y_ref[win, :])
    yield


def _mixer_kernel(xp_ref, xn_ref, cos_ref, sin_ref, win_ref, wout_ref, ng_ref, rg_ref, hgg_ref, lb_ref, fg_ref,
                  xs_ref, cos_s_ref, sin_s_ref, dret_in, dhg_in,
                  y_ref, sret_ref, shg_ref, ys_ref, dret_out, dhg_out,
                  pa_ref, pb_ref, rq_ref, rk_ref, rdec_ref, *dec,
                  heads, dh, tile, chunk, pairs, per_step):
    step = pl.program_id(0)
    pair = step % pairs
    rows_a, rows_b = slice(0, tile), slice(tile, 2 * tile)

    @pl.when(pair == 0)
    def _():
        sret_ref[...] = jnp.zeros_like(sret_ref)
        shg_ref[...] = jnp.zeros_like(shg_ref)

    @pl.when(step == 0)
    def _():
        _retention_tables(rq_ref, rk_ref, rdec_ref, heads, tile)
        _decode_project(xs_ref, cos_s_ref, sin_s_ref, win_ref, ng_ref, lb_ref, dec, heads=heads, dh=dh)
        for _ in _proj_stages(xp_ref, rows_a, ng_ref, win_ref, pa_ref):
            pass

    mixer = functools.partial(
        _mixer, cos_ref=cos_ref, sin_ref=sin_ref, wout_ref=wout_ref, rg_ref=rg_ref, hgg_ref=hgg_ref,
        lb_ref=lb_ref, fg_ref=fg_ref, y_ref=y_ref, sret_ref=sret_ref, shg_ref=shg_ref,
        rq_ref=rq_ref, rk_ref=rk_ref, rdec_ref=rdec_ref, heads=heads, dh=dh, tile=tile, chunk=chunk)
    decode = _decode_stages(step, xs_ref, wout_ref, rg_ref, hgg_ref, fg_ref, dret_in, dhg_in, ys_ref,
                            dret_out, dhg_out, dec, heads=heads, dh=dh, per_step=per_step)
    every = max(1, 2 * (1 + PROJ_GROUPS) // (per_step * heads + 1))
    mixer(xp_ref, rows_a, pa_ref,
          _filler(_zip_stages(_proj_stages(xp_ref, rows_b, ng_ref, win_ref, pb_ref), decode, every)))
    mixer(xp_ref, rows_b, pb_ref,
          _filler(_zip_stages(_proj_stages(xn_ref, rows_a, ng_ref, win_ref, pa_ref), decode, every)))
    for _ in decode:
        pass


def _mixer_call(x, cos, sin, win, wout, ng, rg, hgg, lb, fg, xs, cos_s, sin_s, dret, dhg):
    bsz, seq, d = x.shape
    nb, heads, dh, _ = dret.shape
    tile = min(PROMPT_TILE, seq // 2)
    chunk = min(HGRN_CHUNK, tile)
    assert seq % (2 * tile) == 0 and tile % chunk == 0 and chunk & (chunk - 1) == 0 and chunk >= 16
    gw = heads * dh
    pairs = seq // (2 * tile)
    steps = bsz * pairs
    per_step = nb // steps
    assert nb == per_step * steps and nb % 128 == 0

    def nxt(s):
        s1 = jnp.minimum(s + 1, steps - 1)
        return (s1 // pairs, 2 * (s1 % pairs), 0)

    const = lambda *shape: pl.BlockSpec(shape, lambda s: (0,) * len(shape))
    state_spec = pl.BlockSpec((1, heads, dh, dh), lambda s: (s // pairs, 0, 0, 0))
    dstate_spec = pl.BlockSpec((per_step, heads, dh, dh), lambda s: (s, 0, 0, 0))
    pair_spec = pl.BlockSpec((1, 2 * tile, d), lambda s: (s // pairs, s % pairs, 0))
    rope_spec = pl.BlockSpec((2 * tile, dh), lambda s: (s % pairs, 0))
    body = functools.partial(_mixer_kernel, heads=heads, dh=dh, tile=tile, chunk=chunk, pairs=pairs,
                             per_step=per_step)
    row_scratch = pltpu.VMEM((nb, gw), _F32)
    col_scratch = pltpu.VMEM((gw, nb), _F32)
    return pl.pallas_call(
        body,
        grid=(steps,),
        in_specs=[
            pair_spec, pl.BlockSpec((1, tile, d), nxt), rope_spec, rope_spec,
            const(d // 2, 8 * gw), const(gw, d), const(1, d), const(1, gw), const(1, gw),
            const(lb.shape[0], gw), const(1, d),
            const(nb, d), const(SUBLANES, dh), const(SUBLANES, dh), dstate_spec, dstate_spec,
        ],
        out_specs=[pair_spec, state_spec, state_spec, const(nb, d), dstate_spec, dstate_spec],
        out_shape=[
            jax.ShapeDtypeStruct((bsz, seq, d), _F32),
            jax.ShapeDtypeStruct((bsz, heads, dh, dh), _F32),
            jax.ShapeDtypeStruct((bsz, heads, dh, dh), _F32),
            jax.ShapeDtypeStruct((nb, d), _F32),
            jax.ShapeDtypeStruct(dret.shape, _F32),
            jax.ShapeDtypeStruct(dhg.shape, _F32),
        ],
        scratch_shapes=[pltpu.VMEM((tile, 8 * gw), _F32), pltpu.VMEM((tile, 8 * gw), _F32),
                        pltpu.VMEM((heads, tile, dh), _F32), pltpu.VMEM((heads, tile, dh), _F32),
                        pltpu.VMEM((heads, tile, tile), _F32)]
                       + [col_scratch, col_scratch, row_scratch, row_scratch, col_scratch, col_scratch,
                          row_scratch, row_scratch, col_scratch],
        compiler_params=pltpu.CompilerParams(
            dimension_semantics=("arbitrary",), vmem_limit_bytes=VMEM_LIMIT_BYTES),
        name="mixer_step",
    )(x, x, cos, sin, win, wout, ng, rg, hgg, lb, fg, xs, cos_s, sin_s, dret, dhg)


def kernel(x_prompt, x_sample, state_ret, state_hgrn, norm_g, w_in, ret_norm_g, hg_norm_g, hg_lb, w_out,
           final_norm_g):
    depth, _, heads, dh, _ = state_ret.shape
    assert depth == 1 and x_sample.shape[1] == 1
    lp = x_prompt.shape[1]
    win, wout, cos_p, sin_p, cos_s, sin_s, xs = _pack_weights((w_in[0], w_out[0]), lp, dh, x_sample)
    ng, fg = norm_g[0][None, :], final_norm_g[None, :]
    rg, hgg = ret_norm_g[0][None, :], hg_norm_g[0][None, :]

    y_p, ret_p, hg_p, y_s, ret_s, hg_s = _mixer_call(
        x_prompt, cos_p, sin_p, win, wout, ng, rg, hgg, hg_lb, fg,
        xs, cos_s, sin_s, state_ret[0], state_hgrn[0])
    return (y_p, y_s[:, None, :], ret_p[None], hg_p[None], ret_s[None], hg_s[None])
```

```python
import functools
import math

import jax
import jax.numpy as jnp
from jax import lax
from jax.experimental import pallas as pl
from jax.experimental.pallas import tpu as pltpu

PAST_LEN = 16384
ROPE_BASE = 10000.0
NORM_EPS = 1e-6

PROMPT_TILE = 256
HGRN_CHUNK = 128
PROJ_GROUPS = 16
PACK_ROWS = 512
MXU_ROW_GROUP = 32
SUBLANES = 8
VMEM_LIMIT_BYTES = 56 * 1024 * 1024

_F32 = jnp.float32
_BF16 = jnp.bfloat16


def _rms(x, g):
    return x * lax.rsqrt(jnp.mean(x * x, axis=-1, keepdims=True) + NORM_EPS) * g


def _sigmoid(x):
    return 1.0 / (1.0 + jnp.exp(-x))


def _silu(x):
    return x * _sigmoid(x)


def _dot(a, b):
    return jnp.dot(a, b, preferred_element_type=_F32)


def _dot_nt(a, b):
    return lax.dot_general(a, b, (((1,), (1,)), ((), ())), preferred_element_type=_F32)


def _dot_tn(a, b):
    return lax.dot_general(a, b, (((0,), (0,)), ((), ())), preferred_element_type=_F32)


def _weight(w_ref, cols=slice(None)):
    return pltpu.bitcast(w_ref[:, cols], _BF16)


def _pack_kernel(xs_ref, *refs, half):
    *refs, cos_ref, sin_ref, cos_d_ref, sin_d_ref, xs_out = refs
    xs_out[...] = xs_ref[:, 0, :]
    n = len(refs) // 2
    for w_ref, o_ref in zip(refs[:n], refs[n:]):
        o_ref[...] = pltpu.bitcast(w_ref[...].astype(_BF16), jnp.int32)
    rows, dh = cos_ref.shape
    lane = lax.broadcasted_iota(jnp.int32, (rows, dh), 1)
    pos = (lax.broadcasted_iota(jnp.int32, (rows, dh), 0) + pl.program_id(0) * rows).astype(_F32)
    freqs = 1.0 / (ROPE_BASE ** ((lane % half).astype(_F32) / half))
    ang = pos * freqs
    cos_ref[...] = jnp.cos(ang)
    sin_ref[...] = jnp.where(lane < half, -jnp.sin(ang), jnp.sin(ang))
    ang_d = float(PAST_LEN) * freqs[:SUBLANES]
    cos_d_ref[...] = jnp.cos(ang_d)
    sin_d_ref[...] = jnp.where(lane[:SUBLANES] < half, -jnp.sin(ang_d), jnp.sin(ang_d))


def _pack_weights(ws, seq, dh, xs):
    steps = max(w.shape[0] for w in ws) // PACK_ROWS
    rows = [w.shape[0] // steps for w in ws]
    assert all(w.shape[0] == r * steps and r % 16 == 0 for w, r in zip(ws, rows)) and seq % (8 * steps) == 0
    rope_spec = pl.BlockSpec((seq // steps, dh), lambda i: (i, 0))
    nb, _, d = xs.shape
    assert nb % (8 * steps) == 0
    return pl.pallas_call(
        functools.partial(_pack_kernel, half=dh // 2),
        grid=(steps,),
        in_specs=[pl.BlockSpec((nb // steps, 1, d), lambda i: (i, 0, 0))]
                 + [pl.BlockSpec((r, w.shape[1]), lambda i: (i, 0)) for w, r in zip(ws, rows)],
        out_specs=[pl.BlockSpec((r // 2, w.shape[1]), lambda i: (i, 0)) for w, r in zip(ws, rows)]
                  + [rope_spec, rope_spec] + [pl.BlockSpec((SUBLANES, dh), lambda i: (0, 0))] * 2
                  + [pl.BlockSpec((nb // steps, d), lambda i: (i, 0))],
        out_shape=[jax.ShapeDtypeStruct((w.shape[0] // 2, w.shape[1]), jnp.int32) for w in ws]
                  + [jax.ShapeDtypeStruct((seq, dh), _F32)] * 2 + [jax.ShapeDtypeStruct((SUBLANES, dh), _F32)] * 2
                  + [jax.ShapeDtypeStruct((nb, d), _F32)],
        compiler_params=pltpu.CompilerParams(
            dimension_semantics=("arbitrary",), vmem_limit_bytes=VMEM_LIMIT_BYTES),
        name="pack_weights",
    )(xs, *ws)


def _rotate(x, cos, sin_signed, half):
    return x * cos + pltpu.roll(x, half, 1) * sin_signed


def _lower_bound(lb_ref):
    a = lb_ref[...]
    m = jnp.max(a, axis=0, keepdims=True)
    e = jnp.exp(a - m)
    return e[0:1, :] / jnp.sum(e, axis=0, keepdims=True)


def _forget_gate(hf, lbv):
    z = jnp.exp(-jnp.abs(hf))
    r = 1.0 / (1.0 + z)
    zr = z * r
    pos = hf >= 0.0
    sig_p = jnp.where(pos, r, zr)
    sig_n = jnp.where(pos, zr, r)
    one_m = 1.0 - lbv
    return lbv + one_m * sig_p, one_m * sig_n


def _decay_levels(qa, kin, f):
    c, w = f.shape
    n_lev = c.bit_length() - 1
    rowi = lax.broadcasted_iota(jnp.int32, (c, w), 0)
    tot, q, k = f, qa * f, kin
    sub_levels = SUBLANES.bit_length() - 1
    for lev in range(sub_levels):
        half = 1 << lev
        up = (rowi & half) != 0
        yield jnp.where(up, q, 0.0).astype(_BF16), jnp.where(up, 0.0, k).T.astype(_BF16)
        sib = jnp.where(up, pltpu.roll(tot, half, 0), pltpu.roll(tot, c - half, 0))
        q = jnp.where(up, q * sib, q)
        k = jnp.where(up, k, k * sib)
        tot = tot * sib
    q_b = [q[i:i + SUBLANES] for i in range(0, c, SUBLANES)]
    k_b = [k[i:i + SUBLANES] for i in range(0, c, SUBLANES)]
    tot_b = [tot[i:i + 1] for i in range(0, c, SUBLANES)]
    for lev in range(sub_levels, n_lev):
        half = 1 << lev
        zero = jnp.zeros((half, w), _F32)
        yield (jnp.concatenate([q_b[i] if i % 2 else zero for i in range(len(q_b))], axis=0).astype(_BF16),
               jnp.concatenate([zero if i % 2 else k_b[i] for i in range(len(k_b))], axis=0).T.astype(_BF16))
        q_b = [jnp.concatenate([q_b[i], q_b[i + 1] * tot_b[i]], axis=0) for i in range(0, len(q_b), 2)]
        k_b = [jnp.concatenate([k_b[i] * tot_b[i + 1], k_b[i + 1]], axis=0) for i in range(0, len(k_b), 2)]
        tot_b = [tot_b[i] * tot_b[i + 1] for i in range(0, len(tot_b), 2)]
    yield q_b[0].astype(_BF16), k_b[0].astype(_BF16), tot_b[0]


def _proj_stages(x_ref, rows, ng_ref, win_ref, p_ref):
    h = _rms(x_ref[0, rows, :], ng_ref[...]).astype(_BF16)
    yield
    width = p_ref.shape[1] // PROJ_GROUPS
    for sec in range(PROJ_GROUPS):
        cols = slice(sec * width, (sec + 1) * width)
        p_ref[:, cols] = _dot(h, _weight(win_ref, cols))
        yield


def _zip_stages(main, side, every):
    for n, _ in enumerate(main, 1):
        if n % every == 0:
            next(side, None)
        yield


def _filler(stages):
    def fill(n):
        if n is None:
            for _ in stages:
                pass
        else:
            for _ in range(n):
                next(stages, None)
    return fill


def _log_gamma(hh):
    return math.log1p(-(2.0 ** (-5 - hh)))


def _retention_tables(rq_ref, rk_ref, rdec_ref, heads, tile):
    row = lax.broadcasted_iota(jnp.int32, rq_ref.shape[1:], 0).astype(_F32)
    ti = lax.broadcasted_iota(jnp.int32, (tile, tile), 0)
    si = lax.broadcasted_iota(jnp.int32, (tile, tile), 1)
    lag = jnp.maximum(ti - si, 0).astype(_F32)
    for hh in range(heads):
        lg = _log_gamma(hh)
        rq_ref[hh] = jnp.exp(lg * (row + 1.0))
        rk_ref[hh] = jnp.exp(lg * (tile - 1.0 - row))
        rdec_ref[hh] = jnp.where(ti >= si, jnp.exp(lg * lag), 0.0)


def _mixer(x_ref, rows, p_ref, fill, *, cos_ref, sin_ref, wout_ref, rg_ref, hgg_ref, lb_ref, fg_ref,
           y_ref, sret_ref, shg_ref, rq_ref, rk_ref, rdec_ref, heads, dh, tile, chunk):
    gw = heads * dh

    def p(sec, hh=None, rs=slice(None)):
        if hh is None:
            return p_ref[rs, sec * gw:(sec + 1) * gw]
        return p_ref[rs, sec * gw + hh * dh:sec * gw + (hh + 1) * dh]

    o_all = [None] * (2 * heads)

    cos, sin = cos_ref[rows, :], sin_ref[rows, :]

    def ret_issue(hh):
        q = _rotate(p(0, hh), cos, sin, dh // 2).astype(_BF16)
        kt = (_rotate(p(1, hh), cos, sin, dh // 2) * (dh ** -0.5)).astype(_BF16).T
        v = p(2, hh)
        s_old = sret_ref[0, hh]
        qk_qs = _dot(q, jnp.concatenate([kt, s_old.astype(_BF16)], axis=1))
        sret_ref[0, hh] = math.exp(tile * _log_gamma(hh)) * s_old + _dot(kt, (v * rk_ref[hh]).astype(_BF16))
        return hh, qk_qs, v

    def ret_finish(hh, qk_qs, v):
        sl = slice(hh * dh, (hh + 1) * dh)
        a = qk_qs[:, :tile] * rdec_ref[hh]
        o = qk_qs[:, tile:] * rq_ref[hh] + _dot(a.astype(_BF16), v.astype(_BF16))
        o_all[hh] = (_rms(o, rg_ref[:, sl]) * _silu(p(3, hh))).astype(_BF16)

    lbv = _lower_bound(lb_ref)
    ci = lax.broadcasted_iota(jnp.int32, (chunk, chunk), 0)
    cj = lax.broadcasted_iota(jnp.int32, (chunk, chunk), 1)
    cxor = ci ^ cj
    n_lev = chunk.bit_length() - 1
    diag = ci == cj
    lev_mask = [(ci > cj) & ((cxor >> lev) == 1) for lev in range(n_lev)]
    n_chunks = tile // chunk
    o_hg_chunks = [[None] * n_chunks for _ in range(heads)]

    def hg_issue(c, hh):
        rs = slice(c * chunk, (c + 1) * chunk)
        sl = slice(hh * dh, (hh + 1) * dh)
        f, gk = _forget_gate(p(5, hh, rs), lbv[:, sl])
        gq = _silu(p(4, hh, rs))
        gv = p(6, hh, rs).astype(_BF16)
        a = jnp.where(diag, jnp.sum(gq * gk, axis=-1, keepdims=True), 0.0).astype(_BF16)
        *levels, (q_top, k_top, f_tot) = _decay_levels(gq, gk, f)
        for lev, (q_lv, k_lv) in enumerate(levels):
            half = 1 << lev
            if half % MXU_ROW_GROUP:
                a = jnp.where(lev_mask[lev], _dot(q_lv, k_lv).astype(_BF16), a)
            else:
                ups = [slice(i, i + half) for i in range(half, chunk, 2 * half)]
                prod = _dot(jnp.concatenate([q_lv[u] for u in ups], axis=0), k_lv).astype(_BF16)
                parts, at = [], 0
                for j, u in enumerate(ups):
                    parts.append(a[at:u.start])
                    parts.append(jnp.where(lev_mask[lev][u], prod[j * half:(j + 1) * half], a[u]))
                    at = u.stop
                a = jnp.concatenate(parts, axis=0)
        s_old = shg_ref[0, hh]
        o_inter = _dot(q_top, s_old.astype(_BF16))
        f_col = jnp.broadcast_to(f_tot, (dh, dh)).T
        shg_ref[0, hh] = s_old * f_col + _dot_tn(k_top, gv)
        return c, hh, a, o_inter, gv

    def hg_finish(c, hh, a, o_inter, gv):
        o_hg_chunks[hh][c] = o_inter + _dot(a, gv)

    pend = None
    for hh in range(heads):
        ctx = ret_issue(hh)
        fill(1)
        if pend is not None:
            ret_finish(*pend)
        pend = ctx
    fill(2)
    ret_finish(*pend)
    pend = None
    for c in range(n_chunks):
        for hh in range(heads):
            ctx = hg_issue(c, hh)
            fill(1)
            if pend is not None:
                hg_finish(*pend)
            pend = ctx
    fill(2)
    hg_finish(*pend)
    fill(1)
    for hh in range(heads):
        sl = slice(hh * dh, (hh + 1) * dh)
        o = jnp.concatenate(o_hg_chunks[hh], axis=0)
        o_all[heads + hh] = (_rms(o, hgg_ref[:, sl]) * _silu(p(7, hh))).astype(_BF16)
    fill(None)
    o_cat = jnp.concatenate(o_all, axis=-1)
    y = x_ref[0, rows, :] + _dot(o_cat, _weight(wout_ref))
    y_ref[0, rows, :] = _rms(y, fg_ref[...])


def _decode_project(x_ref, cos_ref, sin_ref, win_ref, ng_ref, lb_ref, dec, *, heads, dh):
    rqt_s, rkt_s, rv_s, rgate_s, hqt_s, hkt_s, hi_s, hgate_s, hft_s = dec
    gw = heads * dh
    h = _rms(x_ref[...], ng_ref[...]).astype(_BF16)

    def proj(sec):
        return _dot(h, _weight(win_ref, slice(sec * gw, (sec + 1) * gw)))

    cos, sin = cos_ref[0:1, :], sin_ref[0:1, :]

    def rot(pj):
        return jnp.concatenate(
            [_rotate(pj[:, hh * dh:(hh + 1) * dh], cos, sin, dh // 2) for hh in range(heads)], axis=-1)

    rqt_s[...] = rot(proj(0)).T
    rkt_s[...] = (rot(proj(1)) * (dh ** -0.5)).T
    rv_s[...] = proj(2)
    rgate_s[...] = _silu(proj(3))
    f, kin = _forget_gate(proj(5), _lower_bound(lb_ref))
    hqt_s[...] = _silu(proj(4)).T
    hkt_s[...] = kin.T
    hft_s[...] = f.T
    hi_s[...] = proj(6)
    hgate_s[...] = _silu(proj(7))


def _decode_stages(step, x_ref, wout_ref, rg_ref, hgg_ref, fg_ref, sret_in, shg_in, y_ref, sret_out, shg_out, dec,
                   *, heads, dh, per_step):
    rqt_s, rkt_s, rv_s, rgate_s, hqt_s, hkt_s, hi_s, hgate_s, hft_s = dec
    nb = x_ref.shape[0]
    wrows = max(per_step, SUBLANES)
    share = wrows // per_step
    assert share in (1, 2) and wrows % SUBLANES == 0
    win = pl.ds(pl.multiple_of((step // share) * wrows, SUBLANES), wrows)
    base = (step % share) * per_step
    shift = (nb - step * per_step) % nb
    rolled = {}

    def col(ref_s, hh, bl):
        if id(ref_s) not in rolled:
            rolled[id(ref_s)] = pltpu.roll(ref_s[...], shift, 1)
        return jnp.broadcast_to(rolled[id(ref_s)][hh * dh:(hh + 1) * dh, bl:bl + 1], (dh, dh))

    def row_of(ref_s, sl, mine):
        return jnp.sum(jnp.where(mine, ref_s[win, sl], 0.0), axis=0, keepdims=True)

    rowi = lax.broadcasted_iota(jnp.int32, (wrows, dh), 0)
    o_ret = [jnp.zeros((wrows, dh), _F32)] * heads
    o_hg = [jnp.zeros((wrows, dh), _F32)] * heads
    for bl in range(per_step):
        mine = rowi == base + bl
        for hh in range(heads):
            sl = slice(hh * dh, (hh + 1) * dh)
            gamma = 1.0 - 2.0 ** (-5 - hh)
            s_new = gamma * sret_in[bl, hh] + col(rkt_s, hh, bl) * row_of(rv_s, sl, mine)
            sret_out[bl, hh] = s_new
            o = jnp.sum(s_new * col(rqt_s, hh, bl), axis=0, keepdims=True)
            o_ret[hh] = jnp.where(mine, o, o_ret[hh])
            g_new = col(hft_s, hh, bl) * shg_in[bl, hh] + col(hkt_s, hh, bl) * row_of(hi_s, sl, mine)
            shg_out[bl, hh] = g_new
            o = jnp.sum(g_new * col(hqt_s, hh, bl), axis=0, keepdims=True)
            o_hg[hh] = jnp.where(mine, o, o_hg[hh])
            yield

    def finish(o_rows, gain_ref, gate):
        return [_rms(o, gain_ref[:, hh * dh:(hh + 1) * dh]) * gate[:, hh * dh:(hh + 1) * dh]
                for hh, o in enumerate(o_rows)]

    o_cat = jnp.concatenate(
        finish(o_ret, rg_ref, rgate_s[win, :]) + finish(o_hg, hgg_ref, hgate_s[win, :]), axis=-1).astype(_BF16)
    y = _rms(x_ref[win, :] + _dot(o_cat, _weight(wout_ref)), fg_ref[...])
    if share == 1:
        y_ref[win, :] = y
    else:
        @pl.when(base == 0)
        def _():
            y_ref[win, :] = y

        @pl.when(base != 0)
        def _():
            keep = lax.broadcasted_iota(jnp.int32, y.shape, 0) >= base
            y_ref[win, :] = jnp.where(keep, y, y_ref[win, :])
    yield


def _mixer_kernel(xp_ref, xn_ref, cos_ref, sin_ref, win_ref, wout_ref, ng_ref, rg_ref, hgg_ref, lb_ref, fg_ref,
                  xs_ref, cos_s_ref, sin_s_ref, dret_in, dhg_in,
                  y_ref, sret_ref, shg_ref, ys_ref, dret_out, dhg_out,
                  pa_ref, pb_ref, rq_ref, rk_ref, rdec_ref, *dec,
                  heads, dh, tile, chunk, pairs, per_step):
    step = pl.program_id(0)
    pair = step % pairs
    rows_a, rows_b = slice(0, tile), slice(tile, 2 * tile)

    @pl.when(pair == 0)
    def _():
        sret_ref[...] = jnp.zeros_like(sret_ref)
        shg_ref[...] = jnp.zeros_like(shg_ref)

    @pl.when(step == 0)
    def _():
        _retention_tables(rq_ref, rk_ref, rdec_ref, heads, tile)
        _decode_project(xs_ref, cos_s_ref, sin_s_ref, win_ref, ng_ref, lb_ref, dec, heads=heads, dh=dh)
        for _ in _proj_stages(xp_ref, rows_a, ng_ref, win_ref, pa_ref):
            pass

    mixer = functools.partial(
        _mixer, cos_ref=cos_ref, sin_ref=sin_ref, wout_ref=wout_ref, rg_ref=rg_ref, hgg_ref=hgg_ref,
        lb_ref=lb_ref, fg_ref=fg_ref, y_ref=y_ref, sret_ref=sret_ref, shg_ref=shg_ref,
        rq_ref=rq_ref, rk_ref=rk_ref, rdec_ref=rdec_ref, heads=heads, dh=dh, tile=tile, chunk=chunk)
    decode = _decode_stages(step, xs_ref, wout_ref, rg_ref, hgg_ref, fg_ref, dret_in, dhg_in, ys_ref,
                            dret_out, dhg_out, dec, heads=heads, dh=dh, per_step=per_step)
    every = max(1, 2 * (1 + PROJ_GROUPS) // (per_step * heads + 1))
    mixer(xp_ref, rows_a, pa_ref,
          _filler(_zip_stages(_proj_stages(xp_ref, rows_b, ng_ref, win_ref, pb_ref), decode, every)))
    mixer(xp_ref, rows_b, pb_ref,
          _filler(_zip_stages(_proj_stages(xn_ref, rows_a, ng_ref, win_ref, pa_ref), decode, every)))
    for _ in decode:
        pass


def _mixer_call(x, cos, sin, win, wout, ng, rg, hgg, lb, fg, xs, cos_s, sin_s, dret, dhg):
    bsz, seq, d = x.shape
    nb, heads, dh, _ = dret.shape
    tile = min(PROMPT_TILE, seq // 2)
    chunk = min(HGRN_CHUNK, tile)
    assert seq % (2 * tile) == 0 and tile % chunk == 0 and chunk & (chunk - 1) == 0 and chunk >= 16
    gw = heads * dh
    pairs = seq // (2 * tile)
    steps = bsz * pairs
    per_step = nb // steps
    assert nb == per_step * steps and nb % 128 == 0

    def nxt(s):
        s1 = jnp.minimum(s + 1, steps - 1)
        return (s1 // pairs, 2 * (s1 % pairs), 0)

    const = lambda *shape: pl.BlockSpec(shape, lambda s: (0,) * len(shape))
    state_spec = pl.BlockSpec((1, heads, dh, dh), lambda s: (s // pairs, 0, 0, 0))
    dstate_spec = pl.BlockSpec((per_step, heads, dh, dh), lambda s: (s, 0, 0, 0))
    pair_spec = pl.BlockSpec((1, 2 * tile, d), lambda s: (s // pairs, s % pairs, 0))
    rope_spec = pl.BlockSpec((2 * tile, dh), lambda s: (s % pairs, 0))
    body = functools.partial(_mixer_kernel, heads=heads, dh=dh, tile=tile, chunk=chunk, pairs=pairs,
                             per_step=per_step)
    row_scratch = pltpu.VMEM((nb, gw), _F32)
    col_scratch = pltpu.VMEM((gw, nb), _F32)
    return pl.pallas_call(
        body,
        grid=(steps,),
        in_specs=[
            pair_spec, pl.BlockSpec((1, tile, d), nxt), rope_spec, rope_spec,
            const(d // 2, 8 * gw), const(gw, d), const(1, d), const(1, gw), const(1, gw),
            const(lb.shape[0], gw), const(1, d),
            const(nb, d), const(SUBLANES, dh), const(SUBLANES, dh), dstate_spec, dstate_spec,
        ],
        out_specs=[pair_spec, state_spec, state_spec, const(nb, d), dstate_spec, dstate_spec],
        out_shape=[
            jax.ShapeDtypeStruct((bsz, seq, d), _F32),
            jax.ShapeDtypeStruct((bsz, heads, dh, dh), _F32),
            jax.ShapeDtypeStruct((bsz, heads, dh, dh), _F32),
            jax.ShapeDtypeStruct((nb, d), _F32),
            jax.ShapeDtypeStruct(dret.shape, _F32),
            jax.ShapeDtypeStruct(dhg.shape, _F32),
        ],
        scratch_shapes=[pltpu.VMEM((tile, 8 * gw), _F32), pltpu.VMEM((tile, 8 * gw), _F32),
                        pltpu.VMEM((heads, tile, dh), _F32), pltpu.VMEM((heads, tile, dh), _F32),
                        pltpu.VMEM((heads, tile, tile), _F32)]
                       + [col_scratch, col_scratch, row_scratch, row_scratch, col_scratch, col_scratch,
                          row_scratch, row_scratch, col_scratch],
        compiler_params=pltpu.CompilerParams(
            dimension_semantics=("arbitrary",), vmem_limit_bytes=VMEM_LIMIT_BYTES),
        name="mixer_step",
    )(x, x, cos, sin, win, wout, ng, rg, hgg, lb, fg, xs, cos_s, sin_s, dret, dhg)


def kernel(x_prompt, x_sample, state_ret, state_hgrn, norm_g, w_in, ret_norm_g, hg_norm_g, hg_lb, w_out,
           final_norm_g):
    depth, _, heads, dh, _ = state_ret.shape
    assert depth == 1 and x_sample.shape[1] == 1
    lp = x_prompt.shape[1]
    win, wout, cos_p, sin_p, cos_s, sin_s, xs = _pack_weights((w_in[0], w_out[0]), lp, dh, x_sample)
    ng, fg = norm_g[0][None, :], final_norm_g[None, :]
    rg, hgg = ret_norm_g[0][None, :], hg_norm_g[0][None, :]

    y_p, ret_p, hg_p, y_s, ret_s, hg_s = _mixer_call(
        x_prompt, cos_p, sin_p, win, wout, ng, rg, hgg, hg_lb, fg,
        xs, cos_s, sin_s, state_ret[0], state_hgrn[0])
    return (y_p, y_s[:, None, :], ret_p[None], hg_p[None], ret_s[None], hg_s[None])
```

```python
import functools
import math

import jax
import jax.numpy as jnp
from jax import lax
from jax.experimental import pallas as pl
from jax.experimental.pallas import tpu as pltpu

PAST_LEN = 16384
ROPE_BASE = 10000.0
NORM_EPS = 1e-6

PROMPT_TILE = 256
HGRN_CHUNK = 128
PROJ_GROUPS = 16
PACK_ROWS = 512
MXU_ROW_GROUP = 32
SUBLANES = 8
VMEM_LIMIT_BYTES = 56 * 1024 * 1024

_F32 = jnp.float32
_BF16 = jnp.bfloat16


def _rms(x, g):
    return x * lax.rsqrt(jnp.mean(x * x, axis=-1, keepdims=True) + NORM_EPS) * g


def _sigmoid(x):
    return 1.0 / (1.0 + jnp.exp(-x))


def _silu(x):
    return x * _sigmoid(x)


def _dot(a, b):
    return jnp.dot(a, b, preferred_element_type=_F32)


def _dot_nt(a, b):
    return lax.dot_general(a, b, (((1,), (1,)), ((), ())), preferred_element_type=_F32)


def _dot_tn(a, b):
    return lax.dot_general(a, b, (((0,), (0,)), ((), ())), preferred_element_type=_F32)


def _weight(w_ref, cols=slice(None)):
    return pltpu.bitcast(w_ref[:, cols], _BF16)


def _pack_kernel(xs_ref, *refs, half):
    *refs, cos_ref, sin_ref, cos_d_ref, sin_d_ref, xs_out = refs
    xs_out[...] = xs_ref[:, 0, :]
    n = len(refs) // 2
    for w_ref, o_ref in zip(refs[:n], refs[n:]):
        o_ref[...] = pltpu.bitcast(w_ref[...].astype(_BF16), jnp.int32)
    rows, dh = cos_ref.shape
    lane = lax.broadcasted_iota(jnp.int32, (rows, dh), 1)
    pos = (lax.broadcasted_iota(jnp.int32, (rows, dh), 0) + pl.program_id(0) * rows).astype(_F32)
    freqs = 1.0 / (ROPE_BASE ** ((lane % half).astype(_F32) / half))
    ang = pos * freqs
    cos_ref[...] = jnp.cos(ang)
    sin_ref[...] = jnp.where(lane < half, -jnp.sin(ang), jnp.sin(ang))
    ang_d = float(PAST_LEN) * freqs[:SUBLANES]
    cos_d_ref[...] = jnp.cos(ang_d)
    sin_d_ref[...] = jnp.where(lane[:SUBLANES] < half, -jnp.sin(ang_d), jnp.sin(ang_d))


def _pack_weights(ws, seq, dh, xs):
    steps = max(w.shape[0] for w in ws) // PACK_ROWS
    rows = [w.shape[0] // steps for w in ws]
    assert all(w.shape[0] == r * steps and r % 16 == 0 for w, r in zip(ws, rows)) and seq % (8 * steps) == 0
    rope_spec = pl.BlockSpec((seq // steps, dh), lambda i: (i, 0))
    nb, _, d = xs.shape
    assert nb % (8 * steps) == 0
    return pl.pallas_call(
        functools.partial(_pack_kernel, half=dh // 2),
        grid=(steps,),
        in_specs=[pl.BlockSpec((nb // steps, 1, d), lambda i: (i, 0, 0))]
                 + [pl.BlockSpec((r, w.shape[1]), lambda i: (i, 0)) for w, r in zip(ws, rows)],
        out_specs=[pl.BlockSpec((r // 2, w.shape[1]), lambda i: (i, 0)) for w, r in zip(ws, rows)]
                  + [rope_spec, rope_spec] + [pl.BlockSpec((SUBLANES, dh), lambda i: (0, 0))] * 2
                  + [pl.BlockSpec((nb // steps, d), lambda i: (i, 0))],
        out_shape=[jax.ShapeDtypeStruct((w.shape[0] // 2, w.shape[1]), jnp.int32) for w in ws]
                  + [jax.ShapeDtypeStruct((seq, dh), _F32)] * 2 + [jax.ShapeDtypeStruct((SUBLANES, dh), _F32)] * 2
                  + [jax.ShapeDtypeStruct((nb, d), _F32)],
        compiler_params=pltpu.CompilerParams(
            dimension_semantics=("arbitrary",), vmem_limit_bytes=VMEM_LIMIT_BYTES),
        name="pack_weights",
    )(xs, *ws)


def _rotate(x, cos, sin_signed, half):
    return x * cos + pltpu.roll(x, half, 1) * sin_signed


def _lower_bound(lb_ref):
    a = lb_ref[...]
    m = jnp.max(a, axis=0, keepdims=True)
    e = jnp.exp(a - m)
    return e[0:1, :] / jnp.sum(e, axis=0, keepdims=True)


def _forget_gate(hf, lbv):
    z = jnp.exp(-jnp.abs(hf))
    r = 1.0 / (1.0 + z)
    zr = z * r
    pos = hf >= 0.0
    sig_p = jnp.where(pos, r, zr)
    sig_n = jnp.where(pos, zr, r)
    one_m = 1.0 - lbv
    return lbv + one_m * sig_p, one_m * sig_n


def _decay_levels(qa, kin, f):
    c, w = f.shape
    n_lev = c.bit_length() - 1
    rowi = lax.broadcasted_iota(jnp.int32, (c, w), 0)
    tot, q, k = f, qa * f, kin
    sub_levels = SUBLANES.bit_length() - 1
    for lev in range(sub_levels):
        half = 1 << lev
        up = (rowi & half) != 0
        yield jnp.where(up, q, 0.0).astype(_BF16), jnp.where(up, 0.0, k).T.astype(_BF16)
        sib = jnp.where(up, pltpu.roll(tot, half, 0), pltpu.roll(tot, c - half, 0))
        q = jnp.where(up, q * sib, q)
        k = jnp.where(up, k, k * sib)
        tot = tot * sib
    q_b = [q[i:i + SUBLANES] for i in range(0, c, SUBLANES)]
    k_b = [k[i:i + SUBLANES] for i in range(0, c, SUBLANES)]
    tot_b = [tot[i:i + 1] for i in range(0, c, SUBLANES)]
    for lev in range(sub_levels, n_lev):
        half = 1 << lev
        zero = jnp.zeros((half, w), _F32)
        yield (jnp.concatenate([q_b[i] if i % 2 else zero for i in range(len(q_b))], axis=0).astype(_BF16),
               jnp.concatenate([zero if i % 2 else k_b[i] for i in range(len(k_b))], axis=0).T.astype(_BF16))
        q_b = [jnp.concatenate([q_b[i], q_b[i + 1] * tot_b[i]], axis=0) for i in range(0, len(q_b), 2)]
        k_b = [jnp.concatenate([k_b[i] * tot_b[i + 1], k_b[i + 1]], axis=0) for i in range(0, len(k_b), 2)]
        tot_b = [tot_b[i] * tot_b[i + 1] for i in range(0, len(tot_b), 2)]
    yield q_b[0].astype(_BF16), k_b[0].astype(_BF16), tot_b[0]


def _proj_stages(x_ref, rows, ng_ref, win_ref, p_ref):
    h = _rms(x_ref[0, rows, :], ng_ref[...]).astype(_BF16)
    yield
    width = p_ref.shape[1] // PROJ_GROUPS
    for sec in range(PROJ_GROUPS):
        cols = slice(sec * width, (sec + 1) * width)
        p_ref[:, cols] = _dot(h, _weight(win_ref, cols))
        yield


def _zip_stages(main, side, every):
    for n, _ in enumerate(main, 1):
        if n % every == 0:
            next(side, None)
        yield


def _filler(stages):
    def fill(n):
        if n is None:
            for _ in stages:
                pass
        else:
            for _ in range(n):
                next(stages, None)
    return fill


def _log_gamma(hh):
    return math.log1p(-(2.0 ** (-5 - hh)))


def _retention_tables(rq_ref, rk_ref, rdec_ref, heads, tile):
    row = lax.broadcasted_iota(jnp.int32, rq_ref.shape[1:], 0).astype(_F32)
    ti = lax.broadcasted_iota(jnp.int32, (tile, tile), 0)
    si = lax.broadcasted_iota(jnp.int32, (tile, tile), 1)
    lag = jnp.maximum(ti - si, 0).astype(_F32)
    for hh in range(heads):
        lg = _log_gamma(hh)
        rq_ref[hh] = jnp.exp(lg * (row + 1.0))
        rk_ref[hh] = jnp.exp(lg * (tile - 1.0 - row))
        rdec_ref[hh] = jnp.where(ti >= si, jnp.exp(lg * lag), 0.0)


def _mixer(x_ref, rows, p_ref, fill, *, cos_ref, sin_ref, wout_ref, rg_ref, hgg_ref, lb_ref, fg_ref,
           y_ref, sret_ref, shg_ref, rq_ref, rk_ref, rdec_ref, heads, dh, tile, chunk):
    gw = heads * dh

    def p(sec, hh=None, rs=slice(None)):
        if hh is None:
            return p_ref[rs, sec * gw:(sec + 1) * gw]
        return p_ref[rs, sec * gw + hh * dh:sec * gw + (hh + 1) * dh]

    o_all = [None] * (2 * heads)

    cos, sin = cos_ref[rows, :], sin_ref[rows, :]

    def ret_issue(hh):
        q = _rotate(p(0, hh), cos, sin, dh // 2).astype(_BF16)
        kt = (_rotate(p(1, hh), cos, sin, dh // 2) * (dh ** -0.5)).astype(_BF16).T
        v = p(2, hh)
        s_old = sret_ref[0, hh]
        qk_qs = _dot(q, jnp.concatenate([kt, s_old.astype(_BF16)], axis=1))
        sret_ref[0, hh] = math.exp(tile * _log_gamma(hh)) * s_old + _dot(kt, (v * rk_ref[hh]).astype(_BF16))
        return hh, qk_qs, v

    def ret_finish(hh, qk_qs, v):
        sl = slice(hh * dh, (hh + 1) * dh)
        a = qk_qs[:, :tile] * rdec_ref[hh]
        o = qk_qs[:, tile:] * rq_ref[hh] + _dot(a.astype(_BF16), v.astype(_BF16))
        o_all[hh] = (_rms(o, rg_ref[:, sl]) * _silu(p(3, hh))).astype(_BF16)

    lbv = _lower_bound(lb_ref)
    ci = lax.broadcasted_iota(jnp.int32, (chunk, chunk), 0)
    cj = lax.broadcasted_iota(jnp.int32, (chunk, chunk), 1)
    cxor = ci ^ cj
    n_lev = chunk.bit_length() - 1
    diag = ci == cj
    class _LevelMask:
        def __getitem__(self, lev):
            return (ci > cj) & ((cxor >> lev) == 1)

    lev_mask = _LevelMask()
    n_chunks = tile // chunk
    o_hg_chunks = [[None] * n_chunks for _ in range(heads)]

    def hg_issue(c, hh):
        rs = slice(c * chunk, (c + 1) * chunk)
        sl = slice(hh * dh, (hh + 1) * dh)
        f, gk = _forget_gate(p(5, hh, rs), lbv[:, sl])
        gq = _silu(p(4, hh, rs))
        gv = p(6, hh, rs).astype(_BF16)
        a = jnp.where(diag, jnp.sum(gq * gk, axis=-1, keepdims=True), 0.0).astype(_BF16)
        *levels, (q_top, k_top, f_tot) = _decay_levels(gq, gk, f)
        for lev, (q_lv, k_lv) in enumerate(levels):
            half = 1 << lev
            if half % MXU_ROW_GROUP:
                a = jnp.where(lev_mask[lev], _dot(q_lv, k_lv).astype(_BF16), a)
            else:
                ups = [slice(i, i + half) for i in range(half, chunk, 2 * half)]
                prod = _dot(jnp.concatenate([q_lv[u] for u in ups], axis=0), k_lv).astype(_BF16)
                parts, at = [], 0
                for j, u in enumerate(ups):
                    parts.append(a[at:u.start])
                    parts.append(jnp.where(lev_mask[lev][u], prod[j * half:(j + 1) * half], a[u]))
                    at = u.stop
                a = jnp.concatenate(parts, axis=0)
        s_old = shg_ref[0, hh]
        o_inter = _dot(q_top, s_old.astype(_BF16))
        f_col = jnp.broadcast_to(f_tot, (dh, dh)).T
        shg_ref[0, hh] = s_old * f_col + _dot_tn(k_top, gv)
        return c, hh, a, o_inter, gv

    def hg_finish(c, hh, a, o_inter, gv):
        o_hg_chunks[hh][c] = o_inter + _dot(a, gv)

    pend = None
    for hh in range(heads):
        ctx = ret_issue(hh)
        fill(1)
        if pend is not None:
            ret_finish(*pend)
        pend = ctx
    fill(2)
    ret_finish(*pend)
    pend = None
    for c in range(n_chunks):
        for hh in range(heads):
            ctx = hg_issue(c, hh)
            fill(1)
            if pend is not None:
                hg_finish(*pend)
            pend = ctx
    fill(2)
    hg_finish(*pend)
    fill(1)
    for hh in range(heads):
        sl = slice(hh * dh, (hh + 1) * dh)
        o = jnp.concatenate(o_hg_chunks[hh], axis=0)
        o_all[heads + hh] = (_rms(o, hgg_ref[:, sl]) * _silu(p(7, hh))).astype(_BF16)
    fill(None)
    o_cat = jnp.concatenate(o_all, axis=-1)
    y = x_ref[0, rows, :] + _dot(o_cat, _weight(wout_ref))
    y_ref[0, rows, :] = _rms(y, fg_ref[...])


def _decode_project(x_ref, cos_ref, sin_ref, win_ref, ng_ref, lb_ref, dec, *, heads, dh):
    rqt_s, rkt_s, rv_s, rgate_s, hqt_s, hkt_s, hi_s, hgate_s, hft_s = dec
    gw = heads * dh
    h = _rms(x_ref[...], ng_ref[...]).astype(_BF16)

    def proj(sec):
        return _dot(h, _weight(win_ref, slice(sec * gw, (sec + 1) * gw)))

    cos, sin = cos_ref[0:1, :], sin_ref[0:1, :]

    def rot(pj):
        return jnp.concatenate(
            [_rotate(pj[:, hh * dh:(hh + 1) * dh], cos, sin, dh // 2) for hh in range(heads)], axis=-1)

    rqt_s[...] = rot(proj(0)).T
    rkt_s[...] = (rot(proj(1)) * (dh ** -0.5)).T
    rv_s[...] = proj(2)
    rgate_s[...] = _silu(proj(3))
    f, kin = _forget_gate(proj(5), _lower_bound(lb_ref))
    hqt_s[...] = _silu(proj(4)).T
    hkt_s[...] = kin.T
    hft_s[...] = f.T
    hi_s[...] = proj(6)
    hgate_s[...] = _silu(proj(7))


def _decode_stages(step, x_ref, wout_ref, rg_ref, hgg_ref, fg_ref, sret_in, shg_in, y_ref, sret_out, shg_out, dec,
                   *, heads, dh, per_step):
    rqt_s, rkt_s, rv_s, rgate_s, hqt_s, hkt_s, hi_s, hgate_s, hft_s = dec
    nb = x_ref.shape[0]
    wrows = max(per_step, SUBLANES)
    share = wrows // per_step
    assert share in (1, 2) and wrows % SUBLANES == 0
    win = pl.ds(pl.multiple_of((step // share) * wrows, SUBLANES), wrows)
    base = (step % share) * per_step
    shift = (nb - step * per_step) % nb
    rolled = {}

    def col(ref_s, hh, bl):
        if id(ref_s) not in rolled:
            rolled[id(ref_s)] = pltpu.roll(ref_s[...], shift, 1)
        return jnp.broadcast_to(rolled[id(ref_s)][hh * dh:(hh + 1) * dh, bl:bl + 1], (dh, dh))

    def row_of(ref_s, sl, mine):
        return jnp.sum(jnp.where(mine, ref_s[win, sl], 0.0), axis=0, keepdims=True)

    rowi = lax.broadcasted_iota(jnp.int32, (wrows, dh), 0)
    o_ret = [jnp.zeros((wrows, dh), _F32)] * heads
    o_hg = [jnp.zeros((wrows, dh), _F32)] * heads
    for bl in range(per_step):
        mine = rowi == base + bl
        for hh in range(heads):
            sl = slice(hh * dh, (hh + 1) * dh)
            gamma = 1.0 - 2.0 ** (-5 - hh)
            s_new = gamma * sret_in[bl, hh] + col(rkt_s, hh, bl) * row_of(rv_s, sl, mine)
            sret_out[bl, hh] = s_new
            o = jnp.sum(s_new * col(rqt_s, hh, bl), axis=0, keepdims=True)
            o_ret[hh] = jnp.where(mine, o, o_ret[hh])
            g_new = col(hft_s, hh, bl) * shg_in[bl, hh] + col(hkt_s, hh, bl) * row_of(hi_s, sl, mine)
            shg_out[bl, hh] = g_new
            o = jnp.sum(g_new * col(hqt_s, hh, bl), axis=0, keepdims=True)
            o_hg[hh] = jnp.where(mine, o, o_hg[hh])
            yield

    def finish(o_rows, gain_ref, gate):
        return [_rms(o, gain_ref[:, hh * dh:(hh + 1) * dh]) * gate[:, hh * dh:(hh + 1) * dh]
                for hh, o in enumerate(o_rows)]

    o_cat = jnp.concatenate(
        finish(o_ret, rg_ref, rgate_s[win, :]) + finish(o_hg, hgg_ref, hgate_s[win, :]), axis=-1).astype(_BF16)
    y = _rms(x_ref[win, :] + _dot(o_cat, _weight(wout_ref)), fg_ref[...])
    if share == 1:
        y_ref[win, :] = y
    else:
        @pl.when(base == 0)
        def _():
            y_ref[win, :] = y

        @pl.when(base != 0)
        def _():
            keep = lax.broadcasted_iota(jnp.int32, y.shape, 0) >= base
            y_ref[win, :] = jnp.where(keep, y, y_ref[win, :])
    yield


def _mixer_kernel(xp_ref, xn_ref, cos_ref, sin_ref, win_ref, wout_ref, ng_ref, rg_ref, hgg_ref, lb_ref, fg_ref,
                  xs_ref, cos_s_ref, sin_s_ref, dret_in, dhg_in,
                  y_ref, sret_ref, shg_ref, ys_ref, dret_out, dhg_out,
                  pa_ref, pb_ref, rq_ref, rk_ref, rdec_ref, *dec,
                  heads, dh, tile, chunk, pairs, per_step):
    step = pl.program_id(0)
    pair = step % pairs
    rows_a, rows_b = slice(0, tile), slice(tile, 2 * tile)

    @pl.when(pair == 0)
    def _():
        sret_ref[...] = jnp.zeros_like(sret_ref)
        shg_ref[...] = jnp.zeros_like(shg_ref)

    @pl.when(step == 0)
    def _():
        _retention_tables(rq_ref, rk_ref, rdec_ref, heads, tile)
        _decode_project(xs_ref, cos_s_ref, sin_s_ref, win_ref, ng_ref, lb_ref, dec, heads=heads, dh=dh)
        for _ in _proj_stages(xp_ref, rows_a, ng_ref, win_ref, pa_ref):
            pass

    mixer = functools.partial(
        _mixer, cos_ref=cos_ref, sin_ref=sin_ref, wout_ref=wout_ref, rg_ref=rg_ref, hgg_ref=hgg_ref,
        lb_ref=lb_ref, fg_ref=fg_ref, y_ref=y_ref, sret_ref=sret_ref, shg_ref=shg_ref,
        rq_ref=rq_ref, rk_ref=rk_ref, rdec_ref=rdec_ref, heads=heads, dh=dh, tile=tile, chunk=chunk)
    decode = _decode_stages(step, xs_ref, wout_ref, rg_ref, hgg_ref, fg_ref, dret_in, dhg_in, ys_ref,
                            dret_out, dhg_out, dec, heads=heads, dh=dh, per_step=per_step)
    every = max(1, 2 * (1 + PROJ_GROUPS) // (per_step * heads + 1))
    mixer(xp_ref, rows_a, pa_ref,
          _filler(_zip_stages(_proj_stages(xp_ref, rows_b, ng_ref, win_ref, pb_ref), decode, every)))
    mixer(xp_ref, rows_b, pb_ref,
          _filler(_zip_stages(_proj_stages(xn_ref, rows_a, ng_ref, win_ref, pa_ref), decode, every)))
    for _ in decode:
        pass


def _mixer_call(x, cos, sin, win, wout, ng, rg, hgg, lb, fg, xs, cos_s, sin_s, dret, dhg):
    bsz, seq, d = x.shape
    nb, heads, dh, _ = dret.shape
    tile = min(PROMPT_TILE, seq // 2)
    chunk = min(HGRN_CHUNK, tile)
    assert seq % (2 * tile) == 0 and tile % chunk == 0 and chunk & (chunk - 1) == 0 and chunk >= 16
    gw = heads * dh
    pairs = seq // (2 * tile)
    steps = bsz * pairs
    per_step = nb // steps
    assert nb == per_step * steps and nb % 128 == 0

    def nxt(s):
        s1 = jnp.minimum(s + 1, steps - 1)
        return (s1 // pairs, 2 * (s1 % pairs), 0)

    const = lambda *shape: pl.BlockSpec(shape, lambda s: (0,) * len(shape))
    state_spec = pl.BlockSpec((1, heads, dh, dh), lambda s: (s // pairs, 0, 0, 0))
    dstate_spec = pl.BlockSpec((per_step, heads, dh, dh), lambda s: (s, 0, 0, 0))
    pair_spec = pl.BlockSpec((1, 2 * tile, d), lambda s: (s // pairs, s % pairs, 0))
    rope_spec = pl.BlockSpec((2 * tile, dh), lambda s: (s % pairs, 0))
    body = functools.partial(_mixer_kernel, heads=heads, dh=dh, tile=tile, chunk=chunk, pairs=pairs,
                             per_step=per_step)
    row_scratch = pltpu.VMEM((nb, gw), _F32)
    col_scratch = pltpu.VMEM((gw, nb), _F32)
    return pl.pallas_call(
        body,
        grid=(steps,),
        in_specs=[
            pair_spec, pl.BlockSpec((1, tile, d), nxt), rope_spec, rope_spec,
            const(d // 2, 8 * gw), const(gw, d), const(1, d), const(1, gw), const(1, gw),
            const(lb.shape[0], gw), const(1, d),
            const(nb, d), const(SUBLANES, dh), const(SUBLANES, dh), dstate_spec, dstate_spec,
        ],
        out_specs=[pair_spec, state_spec, state_spec, const(nb, d), dstate_spec, dstate_spec],
        out_shape=[
            jax.ShapeDtypeStruct((bsz, seq, d), _F32),
            jax.ShapeDtypeStruct((bsz, heads, dh, dh), _F32),
            jax.ShapeDtypeStruct((bsz, heads, dh, dh), _F32),
            jax.ShapeDtypeStruct((nb, d), _F32),
            jax.ShapeDtypeStruct(dret.shape, _F32),
            jax.ShapeDtypeStruct(dhg.shape, _F32),
        ],
        scratch_shapes=[pltpu.VMEM((tile, 8 * gw), _F32), pltpu.VMEM((tile, 8 * gw), _F32),
                        pltpu.VMEM((heads, tile, dh), _F32), pltpu.VMEM((heads, tile, dh), _F32),
                        pltpu.VMEM((heads, tile, tile), _F32)]
                       + [col_scratch, col_scratch, row_scratch, row_scratch, col_scratch, col_scratch,
                          row_scratch, row_scratch, col_scratch],
        compiler_params=pltpu.CompilerParams(
            dimension_semantics=("arbitrary",), vmem_limit_bytes=VMEM_LIMIT_BYTES),
        name="mixer_step",
    )(x, x, cos, sin, win, wout, ng, rg, hgg, lb, fg, xs, cos_s, sin_s, dret, dhg)


def kernel(x_prompt, x_sample, state_ret, state_hgrn, norm_g, w_in, ret_norm_g, hg_norm_g, hg_lb, w_out,
           final_norm_g):
    depth, _, heads, dh, _ = state_ret.shape
    assert depth == 1 and x_sample.shape[1] == 1
    lp = x_prompt.shape[1]
    win, wout, cos_p, sin_p, cos_s, sin_s, xs = _pack_weights((w_in[0], w_out[0]), lp, dh, x_sample)
    ng, fg = norm_g[0][None, :], final_norm_g[None, :]
    rg, hgg = ret_norm_g[0][None, :], hg_norm_g[0][None, :]

    y_p, ret_p, hg_p, y_s, ret_s, hg_s = _mixer_call(
        x_prompt, cos_p, sin_p, win, wout, ng, rg, hgg, hg_lb, fg,
        xs, cos_s, sin_s, state_ret[0], state_hgrn[0])
    return (y_p, y_s[:, None, :], ret_p[None], hg_p[None], ret_s[None], hg_s[None])
```
